```python
import jax, jax.numpy as jnp
from jax import lax
import numpy as np

D_MODEL = 2048
BATCH = 4
SEQ = 2048
DEPTH = 4
DEC_BATCH = 128
DEC_SEQ = 8
PAST_LEN = 16384
PAGE_SIZE = 128

N_MIXERS = 3
N_LAYERS_A = len(range(0, DEPTH, N_MIXERS))
N_LAYERS_B = len(range(1, DEPTH, N_MIXERS))
N_LAYERS_C = len(range(2, DEPTH, N_MIXERS))

HG_FORGET_DIM = 128
HG_HEADS = D_MODEL // HG_FORGET_DIM
HG_HEAD_V = D_MODEL // HG_HEADS

GLA_HEADS = 4
GLA_DK = D_MODEL // 2
GLA_DV = D_MODEL
GLA_HEAD_K = GLA_DK // GLA_HEADS
GLA_HEAD_V = GLA_DV // GLA_HEADS
GLA_GATE_RANK = 16
GLA_GATE_NORM = 16.0

POOL_WINDOWS = (2, 4, 8, 16)
POOL_GROUPS = len(POOL_WINDOWS)
POOL_GROUP_DIM = D_MODEL // POOL_GROUPS
POOL_BUF = max(POOL_WINDOWS) - 1

D_FF = 5632
CHUNK = 32
EPS = 1e-6

kernel_name = "hgrn2_gla_pool_macaron_decoder_step"


def rmsnorm(x, g):
    xf = x.astype(jnp.float32)
    y = xf * lax.rsqrt(jnp.mean(xf * xf, axis=-1, keepdims=True) + EPS)
    return (y * g.astype(jnp.float32)).astype(x.dtype)


def swiglu(x, w_gate, w_up, w_down):
    return (jax.nn.silu(x @ w_gate) * (x @ w_up)) @ w_down


def gla_chunked(q, k, v, log_f, s0):
    B, T, H, K = q.shape
    C = min(CHUNK, T)
    n = -(-T // C)
    pad = n * C - T

    def chunks(a):
        a = jnp.pad(a.astype(jnp.float32), ((0, 0), (0, pad), (0, 0), (0, 0)))
        return jnp.moveaxis(a.reshape(B, n, C, H, a.shape[-1]), 1, 0)

    causal = jnp.tril(jnp.ones((C, C), dtype=bool))[None, :, :, None, None]

    def step(S, blk):
        qc, kc, vc, gc = blk
        b = jnp.cumsum(gc, axis=1)
        b_last = b[:, -1]
        o_inter = jnp.einsum('bthk,bhkv->bthv', qc * jnp.exp(b), S)
        decay = jnp.exp(jnp.where(causal, b[:, :, None] - b[:, None, :], -jnp.inf))
        att = jnp.einsum('bthk,bshk,btshk->bhts', qc, kc, decay)
        o = o_inter + jnp.einsum('bhts,bshv->bthv', att, vc)
        S = jnp.exp(b_last)[..., None] * S + jnp.einsum(
            'bshk,bshv->bhkv', kc * jnp.exp(b_last[:, None] - b), vc)
        return S, o

    s_fin, o = lax.scan(step, s0.astype(jnp.float32),
                        (chunks(q), chunks(k), chunks(v), chunks(log_f)))
    o = jnp.moveaxis(o, 0, 1).reshape(B, n * C, H, -1)[:, :T]
    return o, s_fin


def hgrn_lower_bounds(logits):
    p = jax.nn.softmax(logits.astype(jnp.float32), axis=0)
    gamma = jnp.cumsum(p, axis=0)
    return gamma - gamma[0]


def hgrn2_mixer(xn, s0, lb, w_in, o_gain, w_out):
    B, T, _ = xn.shape
    q, f, i, g = jnp.split(xn @ w_in, 4, axis=-1)
    ff = f.astype(jnp.float32)
    log_f = jnp.logaddexp(jnp.log(lb), jnp.log1p(-lb) + jax.nn.log_sigmoid(ff))
    k = (1.0 - lb) * jax.nn.sigmoid(-ff)
    heads = lambda a: a.reshape(B, T, HG_HEADS, -1)
    qh = heads(jax.nn.silu(q).astype(jnp.float32)) * (HG_FORGET_DIM ** -0.5)
    o, s = gla_chunked(qh, heads(k), heads(i), heads(log_f), s0)
    o = rmsnorm(o.reshape(B, T, D_MODEL).astype(xn.dtype), o_gain) * jax.nn.silu(g)
    return o @ w_out, s.astype(s0.dtype)


def gla_mixer(xn, s0, w_in, w_gate_up, b_gate, o_gain, w_out):
    B, T, _ = xn.shape
    q, k, v, r, glow = jnp.split(
        xn @ w_in, [GLA_DK, 2 * GLA_DK, 2 * GLA_DK + GLA_DV, 2 * GLA_DK + 2 * GLA_DV], axis=-1)
    log_f = jax.nn.log_sigmoid((glow @ w_gate_up + b_gate).astype(jnp.float32)) / GLA_GATE_NORM
    heads = lambda a: a.reshape(B, T, GLA_HEADS, -1)
    qh = heads(q.astype(jnp.float32)) * (GLA_HEAD_K ** -0.5)
    o, s = gla_chunked(qh, heads(k), heads(v), heads(log_f), s0)
    o = rmsnorm(o.astype(xn.dtype), o_gain.reshape(GLA_HEADS, GLA_HEAD_V))
    o = o.reshape(B, T, GLA_DV) * jax.nn.silu(r)
    return o @ w_out, s.astype(s0.dtype)


def pool_mixer(xn, buf, n_prev, w_group, scale):
    B, T, D = xn.shape
    xe = jnp.concatenate([buf.astype(xn.dtype), xn], axis=1)
    cs = jnp.pad(jnp.cumsum(xe.astype(jnp.float32), axis=1), ((0, 0), (1, 0), (0, 0)))
    t = jnp.arange(T)
    outs = []
    for gi, w in enumerate(POOL_WINDOWS):
        sl = slice(gi * POOL_GROUP_DIM, (gi + 1) * POOL_GROUP_DIM)
        win = cs[:, POOL_BUF + 1:POOL_BUF + 1 + T, sl] - cs[:, POOL_BUF + 1 - w:POOL_BUF + 1 - w + T, sl]
        cnt = jnp.minimum(w, t + 1 + n_prev).astype(jnp.float32)
        outs.append(win / cnt[None, :, None] - xn[:, :, sl].astype(jnp.float32))
    y = jnp.stack(outs, axis=2).astype(xn.dtype)
    y = jnp.einsum('btgc,gcd->btgd', y, w_group).reshape(B, T, D) * scale
    return y, xe[:, -POOL_BUF:]


def trunk(x, st_a, st_b, st_c, n_prev, lb, p):
    new_a, new_b, new_c = [], [], []
    for li in range(DEPTH):
        j = li // N_MIXERS
        x = x + 0.5 * swiglu(rmsnorm(x, p['norm_ffn1'][li]), p['ffn1_w_gate'][li],
                             p['ffn1_w_up'][li], p['ffn1_w_down'][li])
        xn = rmsnorm(x, p['norm_mix'][li])
        kind = li % N_MIXERS
        if kind == 0:
            h, s = hgrn2_mixer(xn, st_a[j], lb[li], p['hgrn_w_in'][j], p['hgrn_o_norm'][j], p['hgrn_w_out'][j])
            new_a.append(s)
        elif kind == 1:
            h, s = gla_mixer(xn, st_b[j], p['gla_w_in'][j], p['gla_w_gate_up'][j], p['gla_b_gate'][j],
                             p['gla_o_norm'][j], p['gla_w_out'][j])
            new_b.append(s)
        else:
            h, s = pool_mixer(xn, st_c[j], n_prev, p['pool_w_group'][j], p['pool_scale'][j])
            new_c.append(s)
        x = x + h
        x = x + 0.5 * swiglu(rmsnorm(x, p['norm_ffn2'][li]), p['ffn2_w_gate'][li],
                             p['ffn2_w_up'][li], p['ffn2_w_down'][li])
    return rmsnorm(x, p['final_norm']), jnp.stack(new_a), jnp.stack(new_b), jnp.stack(new_c)


def setup_inputs(seed: int = 0) -> dict:
    key = jax.random.key(seed)
    keys = iter(jax.random.split(key, 64))

    def nrm(shape, scale):
        return scale * jax.random.normal(next(keys), shape, jnp.float32)

    def gain(shape):
        return 1.0 + nrm(shape, 0.02)

    D = D_MODEL
    return {
        "x_prompt": nrm((BATCH, SEQ, D), 1.0),
        "x_sample": nrm((DEC_BATCH, DEC_SEQ, D), 1.0),
        "state_hgrn": nrm((N_LAYERS_A, DEC_BATCH, HG_HEADS, HG_FORGET_DIM, HG_HEAD_V), 0.5),
        "state_gla": nrm((N_LAYERS_B, DEC_BATCH, GLA_HEADS, GLA_HEAD_K, GLA_HEAD_V), 1.0),
        "state_pool": nrm((N_LAYERS_C, DEC_BATCH, POOL_BUF, D), 1.0),
        "norm_ffn1": gain((DEPTH, D)),
        "ffn1_w_gate": nrm((DEPTH, D, D_FF), D ** -0.5),
        "ffn1_w_up": nrm((DEPTH, D, D_FF), D ** -0.5),
        "ffn1_w_down": nrm((DEPTH, D_FF, D), D_FF ** -0.5),
        "norm_mix": gain((DEPTH, D)),
        "norm_ffn2": gain((DEPTH, D)),
        "ffn2_w_gate": nrm((DEPTH, D, D_FF), D ** -0.5),
        "ffn2_w_up": nrm((DEPTH, D, D_FF), D ** -0.5),
        "ffn2_w_down": nrm((DEPTH, D_FF, D), D_FF ** -0.5),
        "hgrn_lb_logits": nrm((DEPTH, HG_HEADS * HG_FORGET_DIM), 0.1),
        "hgrn_w_in": nrm((N_LAYERS_A, D, 4 * D), D ** -0.5),
        "hgrn_o_norm": gain((N_LAYERS_A, D)),
        "hgrn_w_out": nrm((N_LAYERS_A, D, D), D ** -0.5),
        "gla_w_in": nrm((N_LAYERS_B, D, 2 * GLA_DK + 2 * GLA_DV + GLA_GATE_RANK), D ** -0.5),
        "gla_w_gate_up": nrm((N_LAYERS_B, GLA_GATE_RANK, GLA_DK), GLA_GATE_RANK ** -0.5),
        "gla_b_gate": nrm((N_LAYERS_B, GLA_DK), 0.1),
        "gla_o_norm": gain((N_LAYERS_B, GLA_DV)),
        "gla_w_out": nrm((N_LAYERS_B, GLA_DV, D), GLA_DV ** -0.5),
        "pool_w_group": nrm((N_LAYERS_C, POOL_GROUPS, POOL_GROUP_DIM, POOL_GROUP_DIM), POOL_GROUP_DIM ** -0.5),
        "pool_scale": gain((N_LAYERS_C, D)),
        "final_norm": gain((D,)),
    }


def reference(x_prompt, x_sample, state_hgrn, state_gla, state_pool,
              norm_ffn1, ffn1_w_gate, ffn1_w_up, ffn1_w_down, norm_mix,
              norm_ffn2, ffn2_w_gate, ffn2_w_up, ffn2_w_down,
              hgrn_lb_logits, hgrn_w_in, hgrn_o_norm, hgrn_w_out,
              gla_w_in, gla_w_gate_up, gla_b_gate, gla_o_norm, gla_w_out,
              pool_w_group, pool_scale, final_norm):
    p = {
        'norm_ffn1': norm_ffn1, 'ffn1_w_gate': ffn1_w_gate, 'ffn1_w_up': ffn1_w_up, 'ffn1_w_down': ffn1_w_down,
        'norm_mix': norm_mix,
        'norm_ffn2': norm_ffn2, 'ffn2_w_gate': ffn2_w_gate, 'ffn2_w_up': ffn2_w_up, 'ffn2_w_down': ffn2_w_down,
        'hgrn_w_in': hgrn_w_in, 'hgrn_o_norm': hgrn_o_norm, 'hgrn_w_out': hgrn_w_out,
        'gla_w_in': gla_w_in, 'gla_w_gate_up': gla_w_gate_up, 'gla_b_gate': gla_b_gate,
        'gla_o_norm': gla_o_norm, 'gla_w_out': gla_w_out,
        'pool_w_group': pool_w_group, 'pool_scale': pool_scale, 'final_norm': final_norm,
    }
    lb = hgrn_lower_bounds(hgrn_lb_logits)
    dt = x_prompt.dtype
    z_a = jnp.zeros((N_LAYERS_A, BATCH, HG_HEADS, HG_FORGET_DIM, HG_HEAD_V), dt)
    z_b = jnp.zeros((N_LAYERS_B, BATCH, GLA_HEADS, GLA_HEAD_K, GLA_HEAD_V), dt)
    z_c = jnp.zeros((N_LAYERS_C, BATCH, POOL_BUF, D_MODEL), dt)
    y_prompt, hgrn_p, gla_p, pool_p = trunk(x_prompt, z_a, z_b, z_c, 0, lb, p)
    y_sample, hgrn_s, gla_s, pool_s = trunk(x_sample, state_hgrn, state_gla, state_pool,
                                            min(POOL_BUF, PAST_LEN), lb, p)
    return (y_prompt, y_sample, hgrn_p, gla_p, pool_p, hgrn_s, gla_s, pool_s)
```

```python
import functools

import jax
import jax.numpy as jnp
from jax import lax
from jax.experimental import pallas as pl
from jax.experimental.pallas import tpu as pltpu

F32 = jnp.float32
BF16 = jnp.bfloat16

D_MODEL = 2048
BATCH = 4
SEQ = 2048
DEPTH = 4
DEC_BATCH = 128
DEC_SEQ = 8
PAST_LEN = 16384
N_MIXERS = 3

HG_FORGET_DIM = 128
HG_HEADS = D_MODEL // HG_FORGET_DIM
HG_HEAD_V = D_MODEL // HG_HEADS

GLA_HEADS = 4
GLA_DK = D_MODEL // 2
GLA_DV = D_MODEL
GLA_HEAD_K = GLA_DK // GLA_HEADS
GLA_HEAD_V = GLA_DV // GLA_HEADS
GLA_GATE_RANK = 16
GLA_GATE_NORM = 16.0

POOL_WINDOWS = (2, 4, 8, 16)
POOL_GROUP_DIM = D_MODEL // len(POOL_WINDOWS)
POOL_BUF = max(POOL_WINDOWS) - 1

D_FF = 5632
EPS = 1e-6

N_PROMPT = BATCH * SEQ
N_SAMPLE = DEC_BATCH * DEC_SEQ
N_TOKENS = N_PROMPT + N_SAMPLE

LANES = 128
VMEM_LIMIT = 56 * 1024 * 1024

FFN_TM, FFN_TF = 768, 512
PROJ_TM, PROJ_TN = 1024, 512
OUT_TM, OUT_TN = 512, 512
REC_TT = 256
REC_C = 16
SAMPLE_BS = 4
POOL_TT = 256


def _params(*sem):
    return pltpu.CompilerParams(dimension_semantics=sem, vmem_limit_bytes=VMEM_LIMIT)


def _rms(xf, g):
    return xf * lax.rsqrt(jnp.mean(xf * xf, axis=-1, keepdims=True) + EPS) * g


def _silu(a):
    return a * jax.nn.sigmoid(a)


def _log_sigmoid(a):
    return jnp.minimum(a, 0.0) - jnp.log1p(jnp.exp(-jnp.abs(a)))


def _ffn_body(x_ref, g_ref, wg_ref, wu_ref, wd_ref, o_ref, xn_ref):
    f = pl.program_id(1)

    @pl.when(f == 0)
    def _():
        xn_ref[...] = _rms(x_ref[...], g_ref[...]).astype(BF16)
        o_ref[...] = jnp.zeros_like(o_ref)

    xn = xn_ref[...]
    a = jnp.dot(xn, wg_ref[...], preferred_element_type=F32)
    u = jnp.dot(xn, wu_ref[...], preferred_element_type=F32)
    h = (_silu(a) * u).astype(BF16)
    o_ref[...] += jnp.dot(h, wd_ref[...], preferred_element_type=F32)

    @pl.when(f == pl.num_programs(1) - 1)
    def _():
        o_ref[...] = x_ref[...] + 0.5 * o_ref[...]


def _ffn(x, gains, w_gate, w_up, w_down, li):
    m = x.shape[0]
    return pl.pallas_call(
        _ffn_body,
        grid=(m // FFN_TM, D_FF // FFN_TF),
        in_specs=[
            pl.BlockSpec((FFN_TM, D_MODEL), lambda i, f: (i, 0)),
            pl.BlockSpec((None, 1, D_MODEL), lambda i, f: (li, 0, 0)),
            pl.BlockSpec((None, D_MODEL, FFN_TF), lambda i, f: (li, 0, f)),
            pl.BlockSpec((None, D_MODEL, FFN_TF), lambda i, f: (li, 0, f)),
            pl.BlockSpec((None, FFN_TF, D_MODEL), lambda i, f: (li, f, 0)),
        ],
        out_specs=pl.BlockSpec((FFN_TM, D_MODEL), lambda i, f: (i, 0)),
        out_shape=jax.ShapeDtypeStruct((m, D_MODEL), F32),
        scratch_shapes=[pltpu.VMEM((FFN_TM, D_MODEL), BF16)],
        compiler_params=_params("parallel", "arbitrary"),
        name="ffn",
    )(x, gains, w_gate, w_up, w_down)


def _proj_body(x_ref, g_ref, w_ref, o_ref, xn_ref):
    @pl.when(pl.program_id(1) == 0)
    def _():
        xn_ref[...] = _rms(x_ref[...], g_ref[...]).astype(BF16)

    o_ref[...] = jnp.dot(xn_ref[...], w_ref[...], preferred_element_type=F32)


def _norm_proj(x, gains, li, w, name):
    m = x.shape[0]
    n = w.shape[1]
    return pl.pallas_call(
        _proj_body,
        grid=(m // PROJ_TM, n // PROJ_TN),
        in_specs=[
            pl.BlockSpec((PROJ_TM, D_MODEL), lambda i, j: (i, 0)),
            pl.BlockSpec((None, 1, D_MODEL), lambda i, j: (li, 0, 0)),
            pl.BlockSpec((D_MODEL, PROJ_TN), lambda i, j: (0, j)),
        ],
        out_specs=pl.BlockSpec((PROJ_TM, PROJ_TN), lambda i, j: (i, j)),
        out_shape=jax.ShapeDtypeStruct((m, n), F32),
        scratch_shapes=[pltpu.VMEM((PROJ_TM, D_MODEL), BF16)],
        compiler_params=_params("parallel", "arbitrary"),
        name=name,
    )(x, gains, w)


def _lb_body(l_ref, o_ref):
    l = l_ref[...]
    e = jnp.exp(l - jnp.max(l, axis=0, keepdims=True))
    p = e / jnp.sum(e, axis=0, keepdims=True)
    gamma0 = p[0:1]
    gamma = gamma0
    o_ref[0:1, :] = gamma - gamma0
    for i in range(1, DEPTH):
        gamma = gamma + p[i:i + 1]
        o_ref[i:i + 1, :] = gamma - gamma0


def _lower_bounds(logits):
    return pl.pallas_call(
        _lb_body,
        out_shape=jax.ShapeDtypeStruct((DEPTH, D_MODEL), F32),
        name="hgrn_lower_bounds",
    )(logits)


def _chunk_step(q, k, v, lf, s_t):
    c, kdim = q.shape
    rows = lax.broadcasted_iota(jnp.int32, (c, kdim), 0)
    b = lf
    d = 1
    while d < c:
        b = b + jnp.where(rows >= d, pltpu.roll(b, d, axis=0), 0.0)
        d *= 2
    b_last = b[c - 1:c, :]

    o = lax.dot_general((q * jnp.exp(b)).astype(BF16), s_t.astype(BF16),
                        (((1,), (1,)), ((), ())), preferred_element_type=F32)

    rows_c = lax.broadcasted_iota(jnp.int32, (c, 1), 0)
    for s in range(c):
        w = jnp.exp(b - b[s:s + 1, :]) * (q * k[s:s + 1, :])
        col = jnp.sum(w, axis=-1, keepdims=True)
        col = jnp.where(rows_c >= s, col, 0.0)
        o = o + col * v[s:s + 1, :]

    kd = k * jnp.exp(b_last - b)
    upd = lax.dot_general(v.astype(BF16), kd.astype(BF16),
                          (((0,), (0,)), ((), ())), preferred_element_type=F32)
    return o, s_t * jnp.exp(b_last) + upd


def _hgrn_gates(q_raw, f_raw, lb):
    q = _silu(q_raw) * (HG_FORGET_DIM ** -0.5)
    gate = jnp.log1p(-lb) + _log_sigmoid(f_raw)
    log_lb = jnp.log(lb)
    lf = jnp.maximum(log_lb, gate) + jnp.log1p(jnp.exp(-jnp.abs(log_lb - gate)))
    k = (1.0 - lb) * jax.nn.sigmoid(-f_raw)
    return q, k, lf


def _rec_prompt_body(kind, q_ref, k_ref, v_ref, aux_ref, o_ref, st_ref, s_ref):
    t = pl.program_id(2)

    @pl.when(t == 0)
    def _():
        s_ref[...] = jnp.zeros_like(s_ref)

    def sub(j, carry):
        r0 = pl.multiple_of(j * REC_C, REC_C)
        rs = pl.ds(r0, REC_C)
        if kind == "hgrn":
            q, k, lf = _hgrn_gates(q_ref[rs, :], k_ref[rs, :], aux_ref[...])
        else:
            q = q_ref[rs, :] * (GLA_HEAD_K ** -0.5)
            k = k_ref[rs, :]
            lf = aux_ref[rs, :]
        o, s_new = _chunk_step(q, k, v_ref[rs, :], lf, s_ref[...])
        s_ref[...] = s_new
        o_ref[rs, :] = o
        return carry

    lax.fori_loop(0, REC_TT // REC_C, sub, 0)

    @pl.when(t == pl.num_programs(2) - 1)
    def _():
        st_ref[...] = s_ref[...].T


def _rec_prompt(kind, proj, aux, heads, kdim, vdim, q_blk, k_blk, v_blk):
    nt = SEQ // REC_TT
    row = lambda b, h, t: b * nt + t
    if kind == "hgrn":
        aux_spec = pl.BlockSpec((1, kdim), lambda b, h, t: (0, h))
    else:
        aux_spec = pl.BlockSpec((REC_TT, kdim), lambda b, h, t: (row(b, h, t), h))
    return pl.pallas_call(
        functools.partial(_rec_prompt_body, kind),
        grid=(BATCH, heads, nt),
        in_specs=[
            pl.BlockSpec((REC_TT, kdim), lambda b, h, t: (row(b, h, t), q_blk + h)),
            pl.BlockSpec((REC_TT, kdim), lambda b, h, t: (row(b, h, t), k_blk + h)),
            pl.BlockSpec((REC_TT, vdim), lambda b, h, t: (row(b, h, t), v_blk + h)),
            aux_spec,
        ],
        out_specs=[
            pl.BlockSpec((REC_TT, vdim), lambda b, h, t: (row(b, h, t), h)),
            pl.BlockSpec((None, None, kdim, vdim), lambda b, h, t: (b, h, 0, 0)),
        ],
        out_shape=[
            jax.ShapeDtypeStruct((N_PROMPT, heads * vdim), F32),
            jax.ShapeDtypeStruct((BATCH, heads, kdim, vdim), F32),
        ],
        scratch_shapes=[pltpu.VMEM((vdim, kdim), F32)],
        compiler_params=_params("parallel", "parallel", "arbitrary"),
        name=kind + "_rec_prompt",
    )(proj, proj, proj, aux)


def _rec_sample_body(kind, heads, q_ref, k_ref, v_ref, aux_ref, st_ref, o_ref, nst_ref):
    def item(i, carry):
        s = i // heads
        h = i % heads
        if kind == "hgrn":
            q, k, lf = _hgrn_gates(q_ref[s, h], k_ref[s, h], aux_ref[h])
        else:
            q = q_ref[s, h] * (GLA_HEAD_K ** -0.5)
            k = k_ref[s, h]
            lf = aux_ref[s, h]
        o, s_new = _chunk_step(q, k, v_ref[s, h], lf, st_ref[s, h].T)
        o_ref[s, h] = o
        nst_ref[s, h] = s_new.T
        return carry

    lax.fori_loop(0, SAMPLE_BS * heads, item, 0)


def _rec_sample(kind, q, k, v, aux, state, heads, kdim, vdim):
    blk = lambda last2: pl.BlockSpec((SAMPLE_BS, heads) + last2, lambda i: (i, 0, 0, 0))
    if kind == "hgrn":
        aux_spec = pl.BlockSpec((heads, 1, kdim), lambda i: (0, 0, 0))
    else:
        aux_spec = blk((DEC_SEQ, kdim))
    return pl.pallas_call(
        functools.partial(_rec_sample_body, kind, heads),
        grid=(DEC_BATCH // SAMPLE_BS,),
        in_specs=[blk((DEC_SEQ, kdim)), blk((DEC_SEQ, kdim)), blk((DEC_SEQ, vdim)),
                  aux_spec, blk((kdim, vdim))],
        out_specs=[blk((DEC_SEQ, vdim)), blk((kdim, vdim))],
        out_shape=[
            jax.ShapeDtypeStruct((DEC_BATCH, heads, DEC_SEQ, vdim), F32),
            jax.ShapeDtypeStruct((DEC_BATCH, heads, kdim, vdim), F32),
        ],
        compiler_params=_params("parallel"),
        name=kind + "_rec_sample",
    )(q, k, v, aux, state)


def _sample_heads(proj, col0, heads, hdim):
    a = proj[N_PROMPT:, col0:col0 + heads * hdim]
    return a.reshape(DEC_BATCH, DEC_SEQ, heads, hdim).transpose(0, 2, 1, 3)


def _merge_heads(o_prompt, o_sample):
    o_s = o_sample.transpose(0, 2, 1, 3).reshape(N_SAMPLE, D_MODEL)
    return jnp.concatenate([o_prompt, o_s], axis=0)


def _mix_out_body(norm_dim, o_ref, gate_ref, gain_ref, x_ref, w_ref, y_ref, lhs_ref):
    @pl.when(pl.program_id(1) == 0)
    def _():
        for c0 in range(0, D_MODEL, norm_dim):
            sl = slice(c0, c0 + norm_dim)
            on = _rms(o_ref[:, sl], gain_ref[:, sl])
            lhs_ref[:, sl] = (on * _silu(gate_ref[:, sl])).astype(BF16)

    y_ref[...] = x_ref[...] + jnp.dot(lhs_ref[...], w_ref[...], preferred_element_type=F32)


def _mix_out(o, proj, gate_blk, gain, x, w_out, norm_dim, name):
    m = x.shape[0]
    return pl.pallas_call(
        functools.partial(_mix_out_body, norm_dim),
        grid=(m // OUT_TM, D_MODEL // OUT_TN),
        in_specs=[
            pl.BlockSpec((OUT_TM, D_MODEL), lambda i, j: (i, 0)),
            pl.BlockSpec((OUT_TM, D_MODEL), lambda i, j: (i, gate_blk)),
            pl.BlockSpec((1, D_MODEL), lambda i, j: (0, 0)),
            pl.BlockSpec((OUT_TM, OUT_TN), lambda i, j: (i, j)),
            pl.BlockSpec((D_MODEL, OUT_TN), lambda i, j: (0, j)),
        ],
        out_specs=pl.BlockSpec((OUT_TM, OUT_TN), lambda i, j: (i, j)),
        out_shape=jax.ShapeDtypeStruct((m, D_MODEL), F32),
        scratch_shapes=[pltpu.VMEM((OUT_TM, D_MODEL), BF16)],
        compiler_params=_params("parallel", "arbitrary"),
        name=name,
    )(o, proj, gain, x, w_out)


def _gla_gate_body(x_ref, g_ref, wl_ref, wu_ref, b_ref, o_ref):
    xn = _rms(x_ref[...], g_ref[...]).astype(BF16)
    low = jnp.dot(xn, wl_ref[...], preferred_element_type=F32)
    z = jnp.dot(low.astype(BF16), wu_ref[...], preferred_element_type=F32) + b_ref[...]
    o_ref[...] = _log_sigmoid(z) / GLA_GATE_NORM


def _gla_gate(x, gains, li, w_low, w_up, bias):
    m = x.shape[0]
    tm = 512
    return pl.pallas_call(
        _gla_gate_body,
        grid=(m // tm,),
        in_specs=[
            pl.BlockSpec((tm, D_MODEL), lambda i: (i, 0)),
            pl.BlockSpec((None, 1, D_MODEL), lambda i: (li, 0, 0)),
            pl.BlockSpec((D_MODEL, LANES), lambda i: (0, 0)),
            pl.BlockSpec((LANES, GLA_DK), lambda i: (0, 0)),
            pl.BlockSpec((1, GLA_DK), lambda i: (0, 0)),
        ],
        out_specs=pl.BlockSpec((tm, GLA_DK), lambda i: (i, 0)),
        out_shape=jax.ShapeDtypeStruct((m, GLA_DK), F32),
        compiler_params=_params("parallel"),
        name="gla_gate",
    )(x, gains, w_low, w_up, bias)


def _pool_body(tt, n_prev, x_ref, g_ref, buf_ref, w_ref, sc_ref, y_ref, nb_ref, xe_ref):
    t = pl.program_id(1)
    pad = POOL_BUF + 1

    @pl.when(t == 0)
    def _():
        xe_ref[0:1, :] = jnp.zeros((1, D_MODEL), F32)
        xe_ref[1:pad, :] = buf_ref[...]

    @pl.when(t > 0)
    def _():
        xe_ref[0:pad, :] = xe_ref[tt:tt + pad, :]

    x = x_ref[...]
    xn = _rms(x, g_ref[...])
    xe_ref[pad:pad + tt, :] = xn
    pos = t * tt + lax.broadcasted_iota(jnp.int32, (tt, 1), 0)
    for gi, w in enumerate(POOL_WINDOWS):
        sl = slice(gi * POOL_GROUP_DIM, (gi + 1) * POOL_GROUP_DIM)
        cur = xn[:, sl]
        win = cur
        for d in range(1, w):
            win = win + xe_ref[pad - d:pad - d + tt, sl]
        cnt = jnp.minimum(w, pos + 1 + n_prev).astype(F32)
        y = (win / cnt - cur).astype(BF16)
        h = jnp.dot(y, w_ref[gi], preferred_element_type=F32) * sc_ref[:, sl]
        y_ref[:, sl] = x[:, sl] + h

    @pl.when(t == pl.num_programs(1) - 1)
    def _():
        nb_ref[...] = xe_ref[tt + 1:tt + pad, :]


def _pool(x, gains, li, buf, w_group, scale, n_seq, seq_len, tt, row0, n_prev):
    nt = seq_len // tt
    r0 = row0 // tt
    row = lambda b, t: (r0 + b * nt + t, 0)
    return pl.pallas_call(
        functools.partial(_pool_body, tt, n_prev),
        grid=(n_seq, nt),
        in_specs=[
            pl.BlockSpec((tt, D_MODEL), row),
            pl.BlockSpec((None, 1, D_MODEL), lambda b, t: (li, 0, 0)),
            pl.BlockSpec((None, POOL_BUF, D_MODEL), lambda b, t: (b, 0, 0)),
            pl.BlockSpec((len(POOL_WINDOWS), POOL_GROUP_DIM, POOL_GROUP_DIM), lambda b, t: (0, 0, 0)),
            pl.BlockSpec((1, D_MODEL), lambda b, t: (0, 0)),
        ],
        out_specs=[
            pl.BlockSpec((tt, D_MODEL), row),
            pl.BlockSpec((None, POOL_BUF, D_MODEL), lambda b, t: (b, 0, 0)),
        ],
        out_shape=[
            jax.ShapeDtypeStruct(x.shape, F32),
            jax.ShapeDtypeStruct((n_seq, POOL_BUF, D_MODEL), F32),
        ],
        scratch_shapes=[pltpu.VMEM((tt + POOL_BUF + 1, D_MODEL), F32)],
        input_output_aliases={0: 0},
        compiler_params=_params("parallel", "arbitrary"),
        name="pool_mixer",
    )(x, gains, buf, w_group, scale)


def _norm_body(x_ref, g_ref, o_ref):
    o_ref[...] = _rms(x_ref[...], g_ref[...])


def _final_norm(x, gain):
    m = x.shape[0]
    tm = 512
    return pl.pallas_call(
        _norm_body,
        grid=(m // tm,),
        in_specs=[pl.BlockSpec((tm, D_MODEL), lambda i: (i, 0)),
                  pl.BlockSpec((1, D_MODEL), lambda i: (0, 0))],
        out_specs=pl.BlockSpec((tm, D_MODEL), lambda i: (i, 0)),
        out_shape=jax.ShapeDtypeStruct((m, D_MODEL), F32),
        compiler_params=_params("parallel"),
        name="final_norm",
    )(x, gain)


def kernel(x_prompt, x_sample, state_hgrn, state_gla, state_pool, norm_ffn1, ffn1_w_gate, ffn1_w_up, ffn1_w_down, norm_mix, norm_ffn2, ffn2_w_gate, ffn2_w_up, ffn2_w_down, hgrn_lb_logits, hgrn_w_in, hgrn_o_norm, hgrn_w_out, gla_w_in, gla_w_gate_up, gla_b_gate, gla_o_norm, gla_w_out, pool_w_group, pool_scale, final_norm):
    x = jnp.concatenate([x_prompt.reshape(N_PROMPT, D_MODEL),
                         x_sample.reshape(N_SAMPLE, D_MODEL)], axis=0)
    gains = lambda a: a.reshape(a.shape[0], 1, D_MODEL)
    norm_ffn1, norm_mix, norm_ffn2 = gains(norm_ffn1), gains(norm_mix), gains(norm_ffn2)
    ffn1 = (ffn1_w_gate.astype(BF16), ffn1_w_up.astype(BF16), ffn1_w_down.astype(BF16))
    ffn2 = (ffn2_w_gate.astype(BF16), ffn2_w_up.astype(BF16), ffn2_w_down.astype(BF16))
    lb = _lower_bounds(hgrn_lb_logits)

    new_hgrn_p, new_hgrn_s, new_gla_p, new_gla_s, new_pool_p, new_pool_s = [], [], [], [], [], []
    for li in range(DEPTH):
        j = li // N_MIXERS
        kind = li % N_MIXERS
        x = _ffn(x, norm_ffn1, *ffn1, li)
        if kind == 0:
            proj = _norm_proj(x, norm_mix, li, hgrn_w_in[j].astype(BF16), "hgrn_in_proj")
            lb_l = lb[li:li + 1]
            o_p, st_p = _rec_prompt("hgrn", proj, lb_l, HG_HEADS, HG_FORGET_DIM, HG_HEAD_V,
                                    0, HG_HEADS, 2 * HG_HEADS)
            o_s, st_s = _rec_sample(
                "hgrn",
                _sample_heads(proj, 0, HG_HEADS, HG_FORGET_DIM),
                _sample_heads(proj, D_MODEL, HG_HEADS, HG_FORGET_DIM),
                _sample_heads(proj, 2 * D_MODEL, HG_HEADS, HG_HEAD_V),
                lb_l.reshape(HG_HEADS, 1, HG_FORGET_DIM), state_hgrn[j],
                HG_HEADS, HG_FORGET_DIM, HG_HEAD_V)
            new_hgrn_p.append(st_p)
            new_hgrn_s.append(st_s)
            x = _mix_out(_merge_heads(o_p, o_s), proj, 3, hgrn_o_norm[j].reshape(1, D_MODEL), x,
                         hgrn_w_out[j].astype(BF16), D_MODEL, "hgrn_out")
        elif kind == 1:
            n_main = 2 * GLA_DK + 2 * GLA_DV
            proj = _norm_proj(x, norm_mix, li, gla_w_in[j, :, :n_main].astype(BF16), "gla_in_proj")
            w_low = jnp.pad(gla_w_in[j, :, n_main:], ((0, 0), (0, LANES - GLA_GATE_RANK))).astype(BF16)
            w_up = jnp.pad(gla_w_gate_up[j], ((0, LANES - GLA_GATE_RANK), (0, 0))).astype(BF16)
            lf = _gla_gate(x, norm_mix, li, w_low, w_up, gla_b_gate[j].reshape(1, GLA_DK))
            kb = GLA_DK // GLA_HEAD_K
            vb = 2 * GLA_DK // GLA_HEAD_V
            o_p, st_p = _rec_prompt("gla", proj, lf, GLA_HEADS, GLA_HEAD_K, GLA_HEAD_V, 0, kb, vb)
            o_s, st_s = _rec_sample(
                "gla",
                _sample_heads(proj, 0, GLA_HEADS, GLA_HEAD_K),
                _sample_heads(proj, GLA_DK, GLA_HEADS, GLA_HEAD_K),
                _sample_heads(proj, 2 * GLA_DK, GLA_HEADS, GLA_HEAD_V),
                _sample_heads(lf, 0, GLA_HEADS, GLA_HEAD_K), state_gla[j],
                GLA_HEADS, GLA_HEAD_K, GLA_HEAD_V)
            new_gla_p.append(st_p)
            new_gla_s.append(st_s)
            x = _mix_out(_merge_heads(o_p, o_s), proj, (2 * GLA_DK + GLA_DV) // D_MODEL,
                         gla_o_norm[j].reshape(1, GLA_DV), x, gla_w_out[j].astype(BF16),
                         GLA_HEAD_V, "gla_out")
        else:
            w_group = pool_w_group[j].astype(BF16)
            scale = pool_scale[j].reshape(1, D_MODEL)
            x, nb_p = _pool(x, norm_mix, li, jnp.zeros((BATCH, POOL_BUF, D_MODEL), F32), w_group, scale,
                            BATCH, SEQ, POOL_TT, 0, 0)
            x, nb_s = _pool(x, norm_mix, li, state_pool[j], w_group, scale,
                            DEC_BATCH, DEC_SEQ, DEC_SEQ, N_PROMPT, min(POOL_BUF, PAST_LEN))
            new_pool_p.append(nb_p)
            new_pool_s.append(nb_s)
        x = _ffn(x, norm_ffn2, *ffn2, li)

    y = _final_norm(x, final_norm.reshape(1, D_MODEL))
    return (y[:N_PROMPT].reshape(BATCH, SEQ, D_MODEL),
            y[N_PROMPT:].reshape(DEC_BATCH, DEC_SEQ, D_MODEL),
            jnp.stack(new_hgrn_p), jnp.stack(new_gla_p), jnp.stack(new_pool_p),
            jnp.stack(new_hgrn_s), jnp.stack(new_gla_s), jnp.stack(new_pool_s))
```

```python
import functools

import jax
import jax.numpy as jnp
from jax import lax
from jax.experimental import pallas as pl
from jax.experimental.pallas import tpu as pltpu

F32 = jnp.float32
BF16 = jnp.bfloat16

D_MODEL = 2048
BATCH = 4
SEQ = 2048
DEPTH = 4
DEC_BATCH = 128
DEC_SEQ = 8
PAST_LEN = 16384
N_MIXERS = 3

HG_FORGET_DIM = 128
HG_HEADS = D_MODEL // HG_FORGET_DIM
HG_HEAD_V = D_MODEL // HG_HEADS

GLA_HEADS = 4
GLA_DK = D_MODEL // 2
GLA_DV = D_MODEL
GLA_HEAD_K = GLA_DK // GLA_HEADS
GLA_HEAD_V = GLA_DV // GLA_HEADS
GLA_GATE_RANK = 16
GLA_GATE_NORM = 16.0

POOL_WINDOWS = (2, 4, 8, 16)
POOL_GROUP_DIM = D_MODEL // len(POOL_WINDOWS)
POOL_BUF = max(POOL_WINDOWS) - 1

D_FF = 5632
EPS = 1e-6

N_PROMPT = BATCH * SEQ
N_SAMPLE = DEC_BATCH * DEC_SEQ
N_TOKENS = N_PROMPT + N_SAMPLE

LANES = 128
VMEM_LIMIT = 56 * 1024 * 1024

FFN_TM, FFN_TF = 768, 512
PROJ_TM, PROJ_TN = 1024, 512
OUT_TM, OUT_TN = 512, 512
REC_TT = 256
REC_C = 16
HGRN_HB, HGRN_SEG = 4, 32
GLA_HB, GLA_SEG = 1, 64
MXU_EXP_RANGE = 60.0
MXU_KEY_RANGE = 1e8
SAMPLE_BS = 4
POOL_TT = 256


def _params(*sem):
    return pltpu.CompilerParams(dimension_semantics=sem, vmem_limit_bytes=VMEM_LIMIT)


def _rms(xf, g):
    return xf * lax.rsqrt(jnp.mean(xf * xf, axis=-1, keepdims=True) + EPS) * g


def _silu(a):
    return a * jax.nn.sigmoid(a)


def _log1p_exp_neg_abs(a):
    return jnp.log(1.0 + jnp.exp(-jnp.abs(a)))


def _log_sigmoid(a):
    return jnp.minimum(a, 0.0) - _log1p_exp_neg_abs(a)


def _ffn_body(x_ref, g_ref, wg_ref, wu_ref, wd_ref, o_ref, xn_ref):
    f = pl.program_id(1)

    @pl.when(f == 0)
    def _():
        xn_ref[...] = _rms(x_ref[...], g_ref[...]).astype(BF16)
        o_ref[...] = jnp.zeros_like(o_ref)

    xn = xn_ref[...]
    a = jnp.dot(xn, wg_ref[...], preferred_element_type=F32)
    u = jnp.dot(xn, wu_ref[...], preferred_element_type=F32)
    h = (_silu(a) * u).astype(BF16)
    o_ref[...] += jnp.dot(h, wd_ref[...], preferred_element_type=F32)

    @pl.when(f == pl.num_programs(1) - 1)
    def _():
        o_ref[...] = x_ref[...] + 0.5 * o_ref[...]


def _ffn(x, gains, w_gate, w_up, w_down, li):
    m = x.shape[0]
    return pl.pallas_call(
        _ffn_body,
        grid=(m // FFN_TM, D_FF // FFN_TF),
        in_specs=[
            pl.BlockSpec((FFN_TM, D_MODEL), lambda i, f: (i, 0)),
            pl.BlockSpec((None, 1, D_MODEL), lambda i, f: (li, 0, 0)),
            pl.BlockSpec((None, D_MODEL, FFN_TF), lambda i, f: (li, 0, f)),
            pl.BlockSpec((None, D_MODEL, FFN_TF), lambda i, f: (li, 0, f)),
            pl.BlockSpec((None, FFN_TF, D_MODEL), lambda i, f: (li, f, 0)),
        ],
        out_specs=pl.BlockSpec((FFN_TM, D_MODEL), lambda i, f: (i, 0)),
        out_shape=jax.ShapeDtypeStruct((m, D_MODEL), F32),
        scratch_shapes=[pltpu.VMEM((FFN_TM, D_MODEL), BF16)],
        compiler_params=_params("parallel", "arbitrary"),
        name="ffn",
    )(x, gains, w_gate, w_up, w_down)


def _proj_body(x_ref, g_ref, w_ref, o_ref, xn_ref):
    @pl.when(pl.program_id(1) == 0)
    def _():
        xn_ref[...] = _rms(x_ref[...], g_ref[...]).astype(BF16)

    o_ref[...] = jnp.dot(xn_ref[...], w_ref[...], preferred_element_type=F32)


def _norm_proj(x, gains, li, w, wi, n, name):
    m = x.shape[0]
    return pl.pallas_call(
        _proj_body,
        grid=(m // PROJ_TM, n // PROJ_TN),
        in_specs=[
            pl.BlockSpec((PROJ_TM, D_MODEL), lambda i, j: (i, 0)),
            pl.BlockSpec((None, 1, D_MODEL), lambda i, j: (li, 0, 0)),
            pl.BlockSpec((None, D_MODEL, PROJ_TN), lambda i, j: (wi, 0, j)),
        ],
        out_specs=pl.BlockSpec((PROJ_TM, PROJ_TN), lambda i, j: (i, j)),
        out_shape=jax.ShapeDtypeStruct((m, n), F32),
        scratch_shapes=[pltpu.VMEM((PROJ_TM, D_MODEL), BF16)],
        compiler_params=_params("parallel", "arbitrary"),
        name=name,
    )(x, gains, w)


def _lb_body(l_ref, o_ref):
    l = l_ref[...]
    e = jnp.exp(l - jnp.max(l, axis=0, keepdims=True))
    p = e / jnp.sum(e, axis=0, keepdims=True)
    gamma0 = p[0:1]
    gamma = gamma0
    o_ref[0:1, :] = gamma - gamma0
    for i in range(1, DEPTH):
        gamma = gamma + p[i:i + 1]
        o_ref[i:i + 1, :] = gamma - gamma0


def _lower_bounds(logits):
    return pl.pallas_call(
        _lb_body,
        out_shape=jax.ShapeDtypeStruct((DEPTH, D_MODEL), F32),
        name="hgrn_lower_bounds",
    )(logits)


def _chunk_step(q, k, v, lf, s_t):
    c, kdim = q.shape
    rows = lax.broadcasted_iota(jnp.int32, (c, kdim), 0)
    b = lf
    d = 1
    while d < c:
        b = b + jnp.where(rows >= d, pltpu.roll(b, d, axis=0), 0.0)
        d *= 2
    b_last = b[c - 1:c, :]

    o = lax.dot_general((q * jnp.exp(b)).astype(BF16), s_t.astype(BF16),
                        (((1,), (1,)), ((), ())), preferred_element_type=F32)

    rows_c = lax.broadcasted_iota(jnp.int32, (c, 1), 0)
    for s in range(c):
        w = jnp.exp(b - b[s:s + 1, :]) * (q * k[s:s + 1, :])
        col = jnp.sum(w, axis=-1, keepdims=True)
        col = jnp.where(rows_c >= s, col, 0.0)
        o = o + col * v[s:s + 1, :]

    kd = k * jnp.exp(b_last - b)
    upd = lax.dot_general(v.astype(BF16), kd.astype(BF16),
                          (((0,), (0,)), ((), ())), preferred_element_type=F32)
    return o, s_t * jnp.exp(b_last) + upd


def _hgrn_gates(q_raw, f_raw, lb):
    q = _silu(q_raw) * (HG_FORGET_DIM ** -0.5)
    gate = jnp.log1p(-lb) + _log_sigmoid(f_raw)
    log_lb = jnp.log(lb)
    lf = jnp.maximum(log_lb, gate) + _log1p_exp_neg_abs(log_lb - gate)
    k = (1.0 - lb) * jax.nn.sigmoid(-f_raw)
    return q, k, lf


def _seg_cumsum(x, c):
    r = lax.broadcasted_iota(jnp.int32, x.shape, 0) & (c - 1)
    d = 1
    while d < c:
        x = x + jnp.where(r >= d, pltpu.roll(x, d, axis=0), 0.0)
        d *= 2
    return x


def _rec_prompt_body(kind, hb, kdim, vdim, c, q_ref, k_ref, v_ref, aux_ref, o_ref, st_ref,
                     s_ref, qs_ref, ks_ref, ls_ref, bs_ref):
    t = pl.program_id(2)
    tt = REC_TT
    nseg = tt // c

    @pl.when(t == 0)
    def _():
        s_ref[...] = jnp.zeros_like(s_ref)

    worst = jnp.zeros((tt, kdim), F32)
    for h in range(hb):
        ksl = slice(h * kdim, (h + 1) * kdim)
        if kind == "hgrn":
            q, k, lf = _hgrn_gates(q_ref[:, ksl], k_ref[:, ksl], aux_ref[:, ksl])
        else:
            q = q_ref[:, ksl] * (GLA_HEAD_K ** -0.5)
            k = k_ref[:, ksl]
            lf = aux_ref[:, ksl]
        b = _seg_cumsum(lf, c)
        qs_ref[h] = q
        ks_ref[h] = k
        ls_ref[h] = lf
        bs_ref[h] = b
        worst = jnp.maximum(worst, jnp.maximum(-b, jnp.abs(k) * (MXU_EXP_RANGE / MXU_KEY_RANGE)))
    in_range = jnp.max(worst) <= MXU_EXP_RANGE

    @pl.when(in_range)
    def _():
        ri = lax.broadcasted_iota(jnp.int32, (tt, tt), 0)
        ci = lax.broadcasted_iota(jnp.int32, (tt, tt), 1)
        shift = c.bit_length() - 1
        mask = ((ri >> shift) == (ci >> shift)) & (ci <= ri)
        for h in range(hb):
            vsl = slice(h * vdim, (h + 1) * vdim)
            q, k, b = qs_ref[h], ks_ref[h], bs_ref[h]
            b3 = b.reshape(nseg, c, kdim)
            bl3 = b3[:, c - 1:c, :]
            qt = (q * jnp.exp(b)).astype(BF16)
            kt = (k * jnp.exp(-b)).astype(BF16)
            kd = (k * jnp.exp(bl3 - b3).reshape(tt, kdim)).astype(BF16)
            e = jnp.exp(bl3)
            vb = v_ref[:, vsl].astype(BF16)
            a = lax.dot_general(qt, kt, (((1,), (1,)), ((), ())), preferred_element_type=F32)
            att = jnp.where(mask, a, 0.0).astype(BF16)
            o = jnp.dot(att, vb, preferred_element_type=F32)
            segs = [slice(j * c, (j + 1) * c) for j in range(nseg)]
            upd = [lax.dot_general(vb[rs], kd[rs], (((0,), (0,)), ((), ())),
                                   preferred_element_type=F32) for rs in segs]
            s_t = s_ref[h]
            states = []
            for j in range(nseg):
                states.append(s_t.astype(BF16))
                s_t = s_t * e[j] + upd[j]
            s_ref[h] = s_t
            for j, rs in enumerate(segs):
                o_ref[rs, vsl] = o[rs] + lax.dot_general(
                    qt[rs], states[j], (((1,), (1,)), ((), ())), preferred_element_type=F32)

    @pl.when(jnp.logical_not(in_range))
    def _():
        for h in range(hb):
            vsl = slice(h * vdim, (h + 1) * vdim)

            def sub(j, carry):
                rs = pl.ds(pl.multiple_of(j * REC_C, REC_C), REC_C)
                o, s_new = _chunk_step(qs_ref[h, rs, :], ks_ref[h, rs, :], v_ref[rs, vsl],
                                       ls_ref[h, rs, :], s_ref[h])
                s_ref[h] = s_new
                o_ref[rs, vsl] = o
                return carry

            lax.fori_loop(0, tt // REC_C, sub, 0)

    @pl.when(t == pl.num_programs(2) - 1)
    def _():
        for h in range(hb):
            st_ref[h] = s_ref[h].T


def _rec_prompt(kind, proj, aux, heads, hb, kdim, vdim, c, q_blk, k_blk, v_blk):
    nt = SEQ // REC_TT
    row = lambda b, g, t: b * nt + t
    if kind == "hgrn":
        aux_spec = pl.BlockSpec((1, hb * kdim), lambda b, g, t: (0, g))
    else:
        aux_spec = pl.BlockSpec((REC_TT, hb * kdim), lambda b, g, t: (row(b, g, t), g))
    return pl.pallas_call(
        functools.partial(_rec_prompt_body, kind, hb, kdim, vdim, c),
        grid=(BATCH, heads // hb, nt),
        in_specs=[
            pl.BlockSpec((REC_TT, hb * kdim), lambda b, g, t: (row(b, g, t), q_blk + g)),
            pl.BlockSpec((REC_TT, hb * kdim), lambda b, g, t: (row(b, g, t), k_blk + g)),
            pl.BlockSpec((REC_TT, hb * vdim), lambda b, g, t: (row(b, g, t), v_blk + g)),
            aux_spec,
        ],
        out_specs=[
            pl.BlockSpec((REC_TT, hb * vdim), lambda b, g, t: (row(b, g, t), g)),
            pl.BlockSpec((None, hb, kdim, vdim), lambda b, g, t: (b, g, 0, 0)),
        ],
        out_shape=[
            jax.ShapeDtypeStruct((N_PROMPT, heads * vdim), F32),
            jax.ShapeDtypeStruct((BATCH, heads, kdim, vdim), F32),
        ],
        scratch_shapes=[pltpu.VMEM((hb, vdim, kdim), F32)]
        + [pltpu.VMEM((hb, REC_TT, kdim), F32)] * 4,
        compiler_params=_params("parallel", "parallel", "arbitrary"),
        name=kind + "_rec_prompt",
    )(proj, proj, proj, aux)


def _rec_sample_body(kind, heads, q_ref, k_ref, v_ref, aux_ref, st_ref, *rest):
    o_ref, nst_ref = rest[-2:]

    def item(i, carry):
        s = i // heads
        h = i % heads
        if kind == "hgrn":
            q, k, lf = _hgrn_gates(q_ref[s, h], k_ref[s, h], aux_ref[h])
        else:
            q = q_ref[s, h] * (GLA_HEAD_K ** -0.5)
            k = k_ref[s, h]
            lf = aux_ref[s, h]
        o, s_new = _chunk_step(q, k, v_ref[s, h], lf, st_ref[s, h].T)
        o_ref[s, h] = o
        nst_ref[s, h] = s_new.T
        return carry

    lax.fori_loop(0, SAMPLE_BS * heads, item, 0, unroll=4)


def _rec_sample(kind, q, k, v, aux, states, j, new_states, heads, kdim, vdim):
    blk = lambda last2: pl.BlockSpec((SAMPLE_BS, heads) + last2, lambda i: (i, 0, 0, 0))
    st_spec = pl.BlockSpec((None, SAMPLE_BS, heads, kdim, vdim), lambda i: (j, i, 0, 0, 0))
    if kind == "hgrn":
        aux_spec = pl.BlockSpec((heads, 1, kdim), lambda i: (0, 0, 0))
    else:
        aux_spec = blk((DEC_SEQ, kdim))
    in_specs = [blk((DEC_SEQ, kdim)), blk((DEC_SEQ, kdim)), blk((DEC_SEQ, vdim)), aux_spec, st_spec]
    args = [q, k, v, aux, states]
    aliases = {}
    if new_states is not None:
        in_specs.append(pl.BlockSpec(memory_space=pl.ANY))
        args.append(new_states)
        aliases = {len(args) - 1: 1}
    return pl.pallas_call(
        functools.partial(_rec_sample_body, kind, heads),
        grid=(DEC_BATCH // SAMPLE_BS,),
        in_specs=in_specs,
        out_specs=[blk((DEC_SEQ, vdim)), st_spec],
        out_shape=[
            jax.ShapeDtypeStruct((DEC_BATCH, heads, DEC_SEQ, vdim), F32),
            jax.ShapeDtypeStruct(states.shape, F32),
        ],
        input_output_aliases=aliases,
        compiler_params=_params("parallel"),
        name=kind + "_rec_sample",
    )(*args)


def _sample_heads(proj, col0, heads, hdim):
    a = proj[N_PROMPT:, col0:col0 + heads * hdim]
    return a.reshape(DEC_BATCH, DEC_SEQ, heads, hdim).transpose(0, 2, 1, 3)


def _unsplit_heads(o_sample):
    return o_sample.transpose(0, 2, 1, 3).reshape(N_SAMPLE, D_MODEL)


def _prompt_tile(i, tile):
    return jnp.minimum(i, N_PROMPT // tile - 1)


def _sample_tile(i, tile):
    return jnp.maximum(i - N_PROMPT // tile, 0)


def _mix_out_body(norm_dim, op_ref, os_ref, gate_ref, gain_ref, x_ref, w_ref, y_ref, lhs_ref):
    def fill(o_ref):
        for c0 in range(0, D_MODEL, norm_dim):
            sl = slice(c0, c0 + norm_dim)
            on = _rms(o_ref[:, sl], gain_ref[:, sl])
            lhs_ref[:, sl] = (on * _silu(gate_ref[:, sl])).astype(BF16)

    first = pl.program_id(1) == 0
    is_prompt = pl.program_id(0) < N_PROMPT // OUT_TM
    pl.when(first & is_prompt)(lambda: fill(op_ref))
    pl.when(first & jnp.logical_not(is_prompt))(lambda: fill(os_ref))

    y_ref[...] = x_ref[...] + jnp.dot(lhs_ref[...], w_ref[...], preferred_element_type=F32)


def _mix_out(o_prompt, o_sample, proj, gate_blk, gain, x, w_out, norm_dim, name):
    m = x.shape[0]
    return pl.pallas_call(
        functools.partial(_mix_out_body, norm_dim),
        grid=(m // OUT_TM, D_MODEL // OUT_TN),
        in_specs=[
            pl.BlockSpec((OUT_TM, D_MODEL), lambda i, j: (_prompt_tile(i, OUT_TM), 0)),
            pl.BlockSpec((OUT_TM, D_MODEL), lambda i, j: (_sample_tile(i, OUT_TM), 0)),
            pl.BlockSpec((OUT_TM, D_MODEL), lambda i, j: (i, gate_blk)),
            pl.BlockSpec((1, D_MODEL), lambda i, j: (0, 0)),
            pl.BlockSpec((OUT_TM, OUT_TN), lambda i, j: (i, j)),
            pl.BlockSpec((D_MODEL, OUT_TN), lambda i, j: (0, j)),
        ],
        out_specs=pl.BlockSpec((OUT_TM, OUT_TN), lambda i, j: (i, j)),
        out_shape=jax.ShapeDtypeStruct((m, D_MODEL), F32),
        scratch_shapes=[pltpu.VMEM((OUT_TM, D_MODEL), BF16)],
        compiler_params=_params("parallel", "arbitrary"),
        name=name,
    )(o_prompt, o_sample, proj, gain, x, w_out)


def _gla_gate_body(x_ref, g_ref, wl_ref, wu_ref, b_ref, o_ref):
    xn = _rms(x_ref[...], g_ref[...]).astype(BF16)
    low = jnp.dot(xn, wl_ref[...], preferred_element_type=F32)
    z = jnp.dot(low.astype(BF16), wu_ref[...], preferred_element_type=F32) + b_ref[...]
    o_ref[...] = _log_sigmoid(z) / GLA_GATE_NORM


def _gla_gate(x, gains, li, w_low, w_up, bias):
    m = x.shape[0]
    tm = 512
    return pl.pallas_call(
        _gla_gate_body,
        grid=(m // tm,),
        in_specs=[
            pl.BlockSpec((tm, D_MODEL), lambda i: (i, 0)),
            pl.BlockSpec((None, 1, D_MODEL), lambda i: (li, 0, 0)),
            pl.BlockSpec((D_MODEL, LANES), lambda i: (0, 0)),
            pl.BlockSpec((LANES, GLA_DK), lambda i: (0, 0)),
            pl.BlockSpec((1, GLA_DK), lambda i: (0, 0)),
        ],
        out_specs=pl.BlockSpec((tm, GLA_DK), lambda i: (i, 0)),
        out_shape=jax.ShapeDtypeStruct((m, GLA_DK), F32),
        compiler_params=_params("parallel"),
        name="gla_gate",
    )(x, gains, w_low, w_up, bias)


def _pool_body(tt, n_prev, x_ref, g_ref, buf_ref, w_ref, sc_ref, y_ref, nb_ref, xe_ref):
    t = pl.program_id(1)
    pad = POOL_BUF + 1

    @pl.when(t == 0)
    def _():
        xe_ref[0:1, :] = jnp.zeros((1, D_MODEL), F32)
        xe_ref[1:pad, :] = buf_ref[...]

    @pl.when(t > 0)
    def _():
        xe_ref[0:pad, :] = xe_ref[tt:tt + pad, :]

    x = x_ref[...]
    xn = _rms(x, g_ref[...])
    xe_ref[pad:pad + tt, :] = xn
    pos = t * tt + lax.broadcasted_iota(jnp.int32, (tt, 1), 0)
    for gi, w in enumerate(POOL_WINDOWS):
        sl = slice(gi * POOL_GROUP_DIM, (gi + 1) * POOL_GROUP_DIM)
        cur = xn[:, sl]
        win = cur
        for d in range(1, w):
            win = win + xe_ref[pad - d:pad - d + tt, sl]
        cnt = jnp.minimum(w, pos + 1 + n_prev).astype(F32)
        y = (win / cnt - cur).astype(BF16)
        h = jnp.dot(y, w_ref[gi], preferred_element_type=F32) * sc_ref[:, sl]
        y_ref[:, sl] = x[:, sl] + h

    @pl.when(t == pl.num_programs(1) - 1)
    def _():
        nb_ref[...] = xe_ref[tt + 1:tt + pad, :]


def _pool(x, gains, li, buf, w_group, scale, n_seq, seq_len, tt, row0, n_prev):
    nt = seq_len // tt
    r0 = row0 // tt
    row = lambda b, t: (r0 + b * nt + t, 0)
    return pl.pallas_call(
        functools.partial(_pool_body, tt, n_prev),
        grid=(n_seq, nt),
        in_specs=[
            pl.BlockSpec((tt, D_MODEL), row),
            pl.BlockSpec((None, 1, D_MODEL), lambda b, t: (li, 0, 0)),
            pl.BlockSpec((None, POOL_BUF, D_MODEL), lambda b, t: (b, 0, 0)),
            pl.BlockSpec((len(POOL_WINDOWS), POOL_GROUP_DIM, POOL_GROUP_DIM), lambda b, t: (0, 0, 0)),
            pl.BlockSpec((1, D_MODEL), lambda b, t: (0, 0)),
        ],
        out_specs=[
            pl.BlockSpec((tt, D_MODEL), row),
            pl.BlockSpec((None, POOL_BUF, D_MODEL), lambda b, t: (b, 0, 0)),
        ],
        out_shape=[
            jax.ShapeDtypeStruct(x.shape, F32),
            jax.ShapeDtypeStruct((n_seq, POOL_BUF, D_MODEL), F32),
        ],
        scratch_shapes=[pltpu.VMEM((tt + POOL_BUF + 1, D_MODEL), F32)],
        input_output_aliases={0: 0},
        compiler_params=_params("parallel", "arbitrary"),
        name="pool_mixer",
    )(x, gains, buf, w_group, scale)


GROUP_TM = 1024


def _join_body(p_ref, s_ref, o_ref):
    is_prompt = pl.program_id(0) < N_PROMPT // GROUP_TM

    @pl.when(is_prompt)
    def _():
        o_ref[...] = p_ref[...]

    @pl.when(jnp.logical_not(is_prompt))
    def _():
        o_ref[...] = s_ref[...]


def _join_groups(x_prompt, x_sample):
    return pl.pallas_call(
        _join_body,
        grid=(N_TOKENS // GROUP_TM,),
        in_specs=[pl.BlockSpec((GROUP_TM, D_MODEL), lambda i: (_prompt_tile(i, GROUP_TM), 0)),
                  pl.BlockSpec((GROUP_TM, D_MODEL), lambda i: (_sample_tile(i, GROUP_TM), 0))],
        out_specs=pl.BlockSpec((GROUP_TM, D_MODEL), lambda i: (i, 0)),
        out_shape=jax.ShapeDtypeStruct((N_TOKENS, D_MODEL), F32),
        compiler_params=_params("arbitrary"),
        name="join_groups",
    )(x_prompt, x_sample)


def _final_norm_body(x_ref, g_ref, p_ref, s_ref):
    is_prompt = pl.program_id(0) < N_PROMPT // GROUP_TM

    @pl.when(is_prompt)
    def _():
        p_ref[...] = _rms(x_ref[...], g_ref[...])

    @pl.when(jnp.logical_not(is_prompt))
    def _():
        s_ref[...] = _rms(x_ref[...], g_ref[...])


def _final_norm(x, gain):
    return pl.pallas_call(
        _final_norm_body,
        grid=(N_TOKENS // GROUP_TM,),
        in_specs=[pl.BlockSpec((GROUP_TM, D_MODEL), lambda i: (i, 0)),
                  pl.BlockSpec((1, D_MODEL), lambda i: (0, 0))],
        out_specs=[pl.BlockSpec((GROUP_TM, D_MODEL), lambda i: (_prompt_tile(i, GROUP_TM), 0)),
                   pl.BlockSpec((GROUP_TM, D_MODEL), lambda i: (_sample_tile(i, GROUP_TM), 0))],
        out_shape=[jax.ShapeDtypeStruct((N_PROMPT, D_MODEL), F32),
                   jax.ShapeDtypeStruct((N_SAMPLE, D_MODEL), F32)],
        compiler_params=_params("arbitrary"),
        name="final_norm",
    )(x, gain)


def kernel(x_prompt, x_sample, state_hgrn, state_gla, state_pool, norm_ffn1, ffn1_w_gate, ffn1_w_up, ffn1_w_down, norm_mix, norm_ffn2, ffn2_w_gate, ffn2_w_up, ffn2_w_down, hgrn_lb_logits, hgrn_w_in, hgrn_o_norm, hgrn_w_out, gla_w_in, gla_w_gate_up, gla_b_gate, gla_o_norm, gla_w_out, pool_w_group, pool_scale, final_norm):
    x = _join_groups(x_prompt.reshape(N_PROMPT, D_MODEL), x_sample.reshape(N_SAMPLE, D_MODEL))
    gains = lambda a: a.reshape(a.shape[0], 1, D_MODEL)
    norm_ffn1, norm_mix, norm_ffn2 = gains(norm_ffn1), gains(norm_mix), gains(norm_ffn2)
    ffn1 = (ffn1_w_gate.astype(BF16), ffn1_w_up.astype(BF16), ffn1_w_down.astype(BF16))
    ffn2 = (ffn2_w_gate.astype(BF16), ffn2_w_up.astype(BF16), ffn2_w_down.astype(BF16))
    hgrn_w_in_b = hgrn_w_in.astype(BF16)
    gla_w_in_b = gla_w_in.astype(BF16)
    lb = _lower_bounds(hgrn_lb_logits)

    new_hgrn_p, new_gla_p, new_pool_p, new_pool_s = [], [], [], []
    new_hgrn_s = new_gla_s = None
    for li in range(DEPTH):
        j = li // N_MIXERS
        kind = li % N_MIXERS
        x = _ffn(x, norm_ffn1, *ffn1, li)
        if kind == 0:
            proj = _norm_proj(x, norm_mix, li, hgrn_w_in_b, j, 4 * D_MODEL, "hgrn_in_proj")
            lb_l = lb[li:li + 1]
            groups = HG_HEADS // HGRN_HB
            o_p, st_p = _rec_prompt("hgrn", proj, lb_l, HG_HEADS, HGRN_HB, HG_FORGET_DIM, HG_HEAD_V,
                                    HGRN_SEG, 0, groups, 2 * groups)
            o_s, new_hgrn_s = _rec_sample(
                "hgrn",
                _sample_heads(proj, 0, HG_HEADS, HG_FORGET_DIM),
                _sample_heads(proj, D_MODEL, HG_HEADS, HG_FORGET_DIM),
                _sample_heads(proj, 2 * D_MODEL, HG_HEADS, HG_HEAD_V),
                lb_l.reshape(HG_HEADS, 1, HG_FORGET_DIM), state_hgrn, j, new_hgrn_s,
                HG_HEADS, HG_FORGET_DIM, HG_HEAD_V)
            new_hgrn_p.append(st_p)
            x = _mix_out(o_p, _unsplit_heads(o_s), proj, 3, hgrn_o_norm[j].reshape(1, D_MODEL), x,
                         hgrn_w_out[j].astype(BF16), D_MODEL, "hgrn_out")
        elif kind == 1:
            n_main = 2 * GLA_DK + 2 * GLA_DV
            proj = _norm_proj(x, norm_mix, li, gla_w_in_b, j, n_main, "gla_in_proj")
            w_low = jnp.pad(gla_w_in[j, :, n_main:], ((0, 0), (0, LANES - GLA_GATE_RANK))).astype(BF16)
            w_up = jnp.pad(gla_w_gate_up[j], ((0, LANES - GLA_GATE_RANK), (0, 0))).astype(BF16)
            lf = _gla_gate(x, norm_mix, li, w_low, w_up, gla_b_gate[j].reshape(1, GLA_DK))
            kb = GLA_DK // GLA_HEAD_K
            vb = 2 * GLA_DK // GLA_HEAD_V
            o_p, st_p = _rec_prompt("gla", proj, lf, GLA_HEADS, GLA_HB, GLA_HEAD_K, GLA_HEAD_V,
                                    GLA_SEG, 0, kb, vb)
            o_s, new_gla_s = _rec_sample(
                "gla",
                _sample_heads(proj, 0, GLA_HEADS, GLA_HEAD_K),
                _sample_heads(proj, GLA_DK, GLA_HEADS, GLA_HEAD_K),
                _sample_heads(proj, 2 * GLA_DK, GLA_HEADS, GLA_HEAD_V),
                _sample_heads(lf, 0, GLA_HEADS, GLA_HEAD_K), state_gla, j, new_gla_s,
                GLA_HEADS, GLA_HEAD_K, GLA_HEAD_V)
            new_gla_p.append(st_p)
            x = _mix_out(o_p, _unsplit_heads(o_s), proj, (2 * GLA_DK + GLA_DV) // D_MODEL,
                         gla_o_norm[j].reshape(1, GLA_DV), x, gla_w_out[j].astype(BF16),
                         GLA_HEAD_V, "gla_out")
        else:
            w_group = pool_w_group[j].astype(BF16)
            scale = pool_scale[j].reshape(1, D_MODEL)
            x, nb_p = _pool(x, norm_mix, li, jnp.zeros((BATCH, POOL_BUF, D_MODEL), F32), w_group, scale,
                            BATCH, SEQ, POOL_TT, 0, 0)
            x, nb_s = _pool(x, norm_mix, li, state_pool[j], w_group, scale,
                            DEC_BATCH, DEC_SEQ, DEC_SEQ, N_PROMPT, min(POOL_BUF, PAST_LEN))
            new_pool_p.append(nb_p)
            new_pool_s.append(nb_s)
        x = _ffn(x, norm_ffn2, *ffn2, li)

    y_p, y_s = _final_norm(x, final_norm.reshape(1, D_MODEL))
    return (y_p.reshape(BATCH, SEQ, D_MODEL), y_s.reshape(DEC_BATCH, DEC_SEQ, D_MODEL),
            jnp.stack(new_hgrn_p), jnp.stack(new_gla_p), jnp.stack(new_pool_p),
            new_hgrn_s, new_gla_s, jnp.stack(new_pool_s))
```

```python
import functools

import jax
import jax.numpy as jnp
from jax import lax
from jax.experimental import pallas as pl
from jax.experimental.pallas import tpu as pltpu

F32 = jnp.float32
BF16 = jnp.bfloat16

D_MODEL = 2048
BATCH = 4
SEQ = 2048
DEPTH = 4
DEC_BATCH = 128
DEC_SEQ = 8
PAST_LEN = 16384
N_MIXERS = 3

HG_FORGET_DIM = 128
HG_HEADS = D_MODEL // HG_FORGET_DIM
HG_HEAD_V = D_MODEL // HG_HEADS

GLA_HEADS = 4
GLA_DK = D_MODEL // 2
GLA_DV = D_MODEL
GLA_HEAD_K = GLA_DK // GLA_HEADS
GLA_HEAD_V = GLA_DV // GLA_HEADS
GLA_GATE_RANK = 16
GLA_GATE_NORM = 16.0

POOL_WINDOWS = (2, 4, 8, 16)
POOL_GROUP_DIM = D_MODEL // len(POOL_WINDOWS)
POOL_BUF = max(POOL_WINDOWS) - 1

D_FF = 5632
EPS = 1e-6

N_PROMPT = BATCH * SEQ
N_SAMPLE = DEC_BATCH * DEC_SEQ
N_TOKENS = N_PROMPT + N_SAMPLE

LANES = 128
VMEM_LIMIT = 56 * 1024 * 1024

FFN_TM, FFN_TF = 1024, 256
PROJ_TM, PROJ_TN = 1024, 512
OUT_TM = 256
REC_TT = 256
REC_C = 16
HGRN_HB = 4
HGRN_SEG = (32, 128)
GLA_HB, GLA_SEG = 1, 128
MXU_EXP_RANGE = 60.0
MXU_KEY_RANGE = 1e8
SAMPLE_BS = 4
POOL_TT = 256


def _params(*sem):
    return pltpu.CompilerParams(dimension_semantics=sem, vmem_limit_bytes=VMEM_LIMIT)


def _rms(xf, g):
    return xf * lax.rsqrt(jnp.mean(xf * xf, axis=-1, keepdims=True) + EPS) * g


def _silu(a):
    return a * jax.nn.sigmoid(a)


def _log1p_exp_neg_abs(a):
    return jnp.log(1.0 + jnp.exp(-jnp.abs(a)))


def _log_sigmoid(a):
    return jnp.minimum(a, 0.0) - _log1p_exp_neg_abs(a)


def _ffn_body(x_ref, g_ref, wg_ref, wu_ref, wd_ref, o_ref, xn_ref):
    f = pl.program_id(1)

    @pl.when(f == 0)
    def _():
        xn_ref[...] = _rms(x_ref[...], g_ref[...]).astype(BF16)
        o_ref[...] = jnp.zeros_like(o_ref)

    xn = xn_ref[...]
    a = jnp.dot(xn, wg_ref[...].astype(BF16), preferred_element_type=F32)
    u = jnp.dot(xn, wu_ref[...].astype(BF16), preferred_element_type=F32)
    h = (_silu(a) * u).astype(BF16)
    o_ref[...] += jnp.dot(h, wd_ref[...].astype(BF16), preferred_element_type=F32)

    @pl.when(f == pl.num_programs(1) - 1)
    def _():
        o_ref[...] = x_ref[...] + 0.5 * o_ref[...]


def _ffn(x, gains, w_gate, w_up, w_down, li):
    m = x.shape[0]
    return pl.pallas_call(
        _ffn_body,
        grid=(m // FFN_TM, D_FF // FFN_TF),
        in_specs=[
            pl.BlockSpec((FFN_TM, D_MODEL), lambda i, f: (i, 0)),
            pl.BlockSpec((None, 1, D_MODEL), lambda i, f: (li, 0, 0)),
            pl.BlockSpec((None, D_MODEL, FFN_TF), lambda i, f: (li, 0, f)),
            pl.BlockSpec((None, D_MODEL, FFN_TF), lambda i, f: (li, 0, f)),
            pl.BlockSpec((None, FFN_TF, D_MODEL), lambda i, f: (li, f, 0)),
        ],
        out_specs=pl.BlockSpec((FFN_TM, D_MODEL), lambda i, f: (i, 0)),
        out_shape=jax.ShapeDtypeStruct((m, D_MODEL), F32),
        scratch_shapes=[pltpu.VMEM((FFN_TM, D_MODEL), BF16)],
        compiler_params=_params("parallel", "arbitrary"),
        name="ffn",
    )(x, gains, w_gate, w_up, w_down)


def _proj_body(x_ref, g_ref, w_ref, o_ref, xn_ref):
    @pl.when(pl.program_id(1) == 0)
    def _():
        xn_ref[...] = _rms(x_ref[...], g_ref[...]).astype(BF16)

    o_ref[...] = jnp.dot(xn_ref[...], w_ref[...], preferred_element_type=F32)


def _norm_proj(x, gains, li, w, wi, n, name):
    m = x.shape[0]
    return pl.pallas_call(
        _proj_body,
        grid=(m // PROJ_TM, n // PROJ_TN),
        in_specs=[
            pl.BlockSpec((PROJ_TM, D_MODEL), lambda i, j: (i, 0)),
            pl.BlockSpec((None, 1, D_MODEL), lambda i, j: (li, 0, 0)),
            pl.BlockSpec((None, D_MODEL, PROJ_TN), lambda i, j: (wi, 0, j)),
        ],
        out_specs=pl.BlockSpec((PROJ_TM, PROJ_TN), lambda i, j: (i, j)),
        out_shape=jax.ShapeDtypeStruct((m, n), F32),
        scratch_shapes=[pltpu.VMEM((PROJ_TM, D_MODEL), BF16)],
        compiler_params=_params("parallel", "arbitrary"),
        name=name,
    )(x, gains, w)


def _lb_body(l_ref, o_ref):
    l = l_ref[...]
    e = jnp.exp(l - jnp.max(l, axis=0, keepdims=True))
    p = e / jnp.sum(e, axis=0, keepdims=True)
    gamma0 = p[0:1]
    gamma = gamma0
    o_ref[0:1, :] = gamma - gamma0
    for i in range(1, DEPTH):
        gamma = gamma + p[i:i + 1]
        o_ref[i:i + 1, :] = gamma - gamma0


def _lower_bounds(logits):
    return pl.pallas_call(
        _lb_body,
        out_shape=jax.ShapeDtypeStruct((DEPTH, D_MODEL), F32),
        name="hgrn_lower_bounds",
    )(logits)


def _chunk_step(q, k, v, lf, s_t):
    c, kdim = q.shape
    rows = lax.broadcasted_iota(jnp.int32, (c, kdim), 0)
    b = lf
    d = 1
    while d < c:
        b = b + jnp.where(rows >= d, pltpu.roll(b, d, axis=0), 0.0)
        d *= 2
    b_last = b[c - 1:c, :]

    o = lax.dot_general((q * jnp.exp(b)).astype(BF16), s_t.astype(BF16),
                        (((1,), (1,)), ((), ())), preferred_element_type=F32)

    rows_c = lax.broadcasted_iota(jnp.int32, (c, 1), 0)
    for s in range(c):
        w = jnp.exp(b - b[s:s + 1, :]) * (q * k[s:s + 1, :])
        col = jnp.sum(w, axis=-1, keepdims=True)
        col = jnp.where(rows_c >= s, col, 0.0)
        o = o + col * v[s:s + 1, :]

    kd = k * jnp.exp(b_last - b)
    upd = lax.dot_general(v.astype(BF16), kd.astype(BF16),
                          (((0,), (0,)), ((), ())), preferred_element_type=F32)
    return o, s_t * jnp.exp(b_last) + upd


def _hgrn_gates(q_raw, f_raw, lb):
    q = _silu(q_raw) * (HG_FORGET_DIM ** -0.5)
    gate = jnp.log1p(-lb) + _log_sigmoid(f_raw)
    log_lb = jnp.log(lb)
    lf = jnp.maximum(log_lb, gate) + _log1p_exp_neg_abs(log_lb - gate)
    k = (1.0 - lb) * jax.nn.sigmoid(-f_raw)
    return q, k, lf


def _seg_cumsum(x, c):
    r = lax.broadcasted_iota(jnp.int32, x.shape, 0) & (c - 1)
    d = 1
    while d < c:
        x = x + jnp.where(r >= d, pltpu.roll(x, d, axis=0), 0.0)
        d *= 2
    return x


def _rec_prompt_body(kind, hb, kdim, vdim, c, q_ref, k_ref, v_ref, aux_ref, o_ref, st_ref,
                     s_ref, qs_ref, ks_ref, ls_ref, bs_ref):
    t = pl.program_id(2)
    tt = REC_TT
    nseg = tt // c

    @pl.when(t == 0)
    def _():
        s_ref[...] = jnp.zeros_like(s_ref)

    worst = jnp.zeros((tt, kdim), F32)
    for h in range(hb):
        ksl = slice(h * kdim, (h + 1) * kdim)
        if kind == "hgrn":
            q, k, lf = _hgrn_gates(q_ref[:, ksl], k_ref[:, ksl], aux_ref[:, ksl])
        else:
            q = q_ref[:, ksl] * (GLA_HEAD_K ** -0.5)
            k = k_ref[:, ksl]
            lf = aux_ref[:, ksl]
        b = _seg_cumsum(lf, c)
        qs_ref[h] = q
        ks_ref[h] = k
        ls_ref[h] = lf
        bs_ref[h] = b
        worst = jnp.maximum(worst, jnp.maximum(-b, jnp.abs(k) * (MXU_EXP_RANGE / MXU_KEY_RANGE)))
    in_range = jnp.max(worst) <= MXU_EXP_RANGE

    @pl.when(in_range)
    def _():
        ri = lax.broadcasted_iota(jnp.int32, (tt, tt), 0)
        ci = lax.broadcasted_iota(jnp.int32, (tt, tt), 1)
        shift = c.bit_length() - 1
        mask = ((ri >> shift) == (ci >> shift)) & (ci <= ri)
        for h in range(hb):
            vsl = slice(h * vdim, (h + 1) * vdim)
            q, k, b = qs_ref[h], ks_ref[h], bs_ref[h]
            b3 = b.reshape(nseg, c, kdim)
            bl3 = b3[:, c - 1:c, :]
            qt = (q * jnp.exp(b)).astype(BF16)
            kt = (k * jnp.exp(-b)).astype(BF16)
            kd = (k * jnp.exp(bl3 - b3).reshape(tt, kdim)).astype(BF16)
            e = jnp.exp(bl3)
            vb = v_ref[:, vsl].astype(BF16)
            a = lax.dot_general(qt, kt, (((1,), (1,)), ((), ())), preferred_element_type=F32)
            att = jnp.where(mask, a, 0.0).astype(BF16)
            o = jnp.dot(att, vb, preferred_element_type=F32)
            segs = [slice(j * c, (j + 1) * c) for j in range(nseg)]
            upd = [lax.dot_general(vb[rs], kd[rs], (((0,), (0,)), ((), ())),
                                   preferred_element_type=F32) for rs in segs]
            s_t = s_ref[h]
            states = []
            for j in range(nseg):
                states.append(s_t.astype(BF16))
                s_t = s_t * e[j] + upd[j]
            s_ref[h] = s_t
            for j, rs in enumerate(segs):
                o_ref[rs, vsl] = o[rs] + lax.dot_general(
                    qt[rs], states[j], (((1,), (1,)), ((), ())), preferred_element_type=F32)

    @pl.when(jnp.logical_not(in_range))
    def _():
        for h in range(hb):
            vsl = slice(h * vdim, (h + 1) * vdim)

            def sub(j, carry):
                rs = pl.ds(pl.multiple_of(j * REC_C, REC_C), REC_C)
                o, s_new = _chunk_step(qs_ref[h, rs, :], ks_ref[h, rs, :], v_ref[rs, vsl],
                                       ls_ref[h, rs, :], s_ref[h])
                s_ref[h] = s_new
                o_ref[rs, vsl] = o
                return carry

            lax.fori_loop(0, tt // REC_C, sub, 0)

    @pl.when(t == pl.num_programs(2) - 1)
    def _():
        for h in range(hb):
            st_ref[h] = s_ref[h].T


def _rec_prompt(kind, proj, aux, heads, hb, kdim, vdim, c, q_blk, k_blk, v_blk):
    nt = SEQ // REC_TT
    row = lambda b, g, t: b * nt + t
    if kind == "hgrn":
        aux_spec = pl.BlockSpec((1, hb * kdim), lambda b, g, t: (0, g))
    else:
        aux_spec = pl.BlockSpec((REC_TT, hb * kdim), lambda b, g, t: (row(b, g, t), g))
    return pl.pallas_call(
        functools.partial(_rec_prompt_body, kind, hb, kdim, vdim, c),
        grid=(BATCH, heads // hb, nt),
        in_specs=[
            pl.BlockSpec((REC_TT, hb * kdim), lambda b, g, t: (row(b, g, t), q_blk + g)),
            pl.BlockSpec((REC_TT, hb * kdim), lambda b, g, t: (row(b, g, t), k_blk + g)),
            pl.BlockSpec((REC_TT, hb * vdim), lambda b, g, t: (row(b, g, t), v_blk + g)),
            aux_spec,
        ],
        out_specs=[
            pl.BlockSpec((REC_TT, hb * vdim), lambda b, g, t: (row(b, g, t), g)),
            pl.BlockSpec((None, hb, kdim, vdim), lambda b, g, t: (b, g, 0, 0)),
        ],
        out_shape=[
            jax.ShapeDtypeStruct((N_PROMPT, heads * vdim), F32),
            jax.ShapeDtypeStruct((BATCH, heads, kdim, vdim), F32),
        ],
        scratch_shapes=[pltpu.VMEM((hb, vdim, kdim), F32)]
        + [pltpu.VMEM((hb, REC_TT, kdim), F32)] * 4,
        compiler_params=_params("parallel", "parallel", "arbitrary"),
        name=kind + "_rec_prompt",
    )(proj, proj, proj, aux)


def _rec_sample_body(kind, heads, q_ref, k_ref, v_ref, aux_ref, st_ref, *rest):
    o_ref, nst_ref = rest[-2:]

    def item(i, carry):
        s = i // heads
        h = i % heads
        if kind == "hgrn":
            q, k, lf = _hgrn_gates(q_ref[s, h], k_ref[s, h], aux_ref[h])
        else:
            q = q_ref[s, h] * (GLA_HEAD_K ** -0.5)
            k = k_ref[s, h]
            lf = aux_ref[s, h]
        o, s_new = _chunk_step(q, k, v_ref[s, h], lf, st_ref[s, h].T)
        o_ref[s, h] = o
        nst_ref[s, h] = s_new.T
        return carry

    lax.fori_loop(0, SAMPLE_BS * heads, item, 0, unroll=4)


def _rec_sample(kind, q, k, v, aux, states, j, new_states, heads, kdim, vdim):
    blk = lambda last2: pl.BlockSpec((SAMPLE_BS, heads) + last2, lambda i: (i, 0, 0, 0))
    st_spec = pl.BlockSpec((None, SAMPLE_BS, heads, kdim, vdim), lambda i: (j, i, 0, 0, 0))
    if kind == "hgrn":
        aux_spec = pl.BlockSpec((heads, 1, kdim), lambda i: (0, 0, 0))
    else:
        aux_spec = blk((DEC_SEQ, kdim))
    in_specs = [blk((DEC_SEQ, kdim)), blk((DEC_SEQ, kdim)), blk((DEC_SEQ, vdim)), aux_spec, st_spec]
    args = [q, k, v, aux, states]
    aliases = {}
    if new_states is not None:
        in_specs.append(pl.BlockSpec(memory_space=pl.ANY))
        args.append(new_states)
        aliases = {len(args) - 1: 1}
    return pl.pallas_call(
        functools.partial(_rec_sample_body, kind, heads),
        grid=(DEC_BATCH // SAMPLE_BS,),
        in_specs=in_specs,
        out_specs=[blk((DEC_SEQ, vdim)), st_spec],
        out_shape=[
            jax.ShapeDtypeStruct((DEC_BATCH, heads, DEC_SEQ, vdim), F32),
            jax.ShapeDtypeStruct(states.shape, F32),
        ],
        input_output_aliases=aliases,
        compiler_params=_params("parallel"),
        name=kind + "_rec_sample",
    )(*args)


def _sample_heads(proj, col0, heads, hdim):
    a = proj[N_PROMPT:, col0:col0 + heads * hdim]
    return a.reshape(DEC_BATCH, DEC_SEQ, heads, hdim).transpose(0, 2, 1, 3)


def _unsplit_heads(o_sample):
    return o_sample.transpose(0, 2, 1, 3).reshape(N_SAMPLE, D_MODEL)


def _prompt_tile(i, tile):
    return jnp.minimum(i, N_PROMPT // tile - 1)


def _sample_tile(i, tile):
    return jnp.maximum(i - N_PROMPT // tile, 0)


def _mix_out_body(norm_dim, op_ref, os_ref, gate_ref, gain_ref, x_ref, w_ref, y_ref):
    def run(o_ref):
        half = OUT_TM // 2
        for r0 in (0, half):
            rs = slice(r0, r0 + half)
            parts = []
            for c0 in range(0, D_MODEL, norm_dim):
                sl = slice(c0, c0 + norm_dim)
                on = _rms(o_ref[rs, sl], gain_ref[:, sl])
                parts.append((on * _silu(gate_ref[rs, sl])).astype(BF16))
            lhs = parts[0] if len(parts) == 1 else jnp.concatenate(parts, axis=1)
            y_ref[rs, :] = x_ref[rs, :] + jnp.dot(lhs, w_ref[...], preferred_element_type=F32)

    is_prompt = pl.program_id(0) < N_PROMPT // OUT_TM
    pl.when(is_prompt)(lambda: run(op_ref))
    pl.when(jnp.logical_not(is_prompt))(lambda: run(os_ref))


def _mix_out(o_prompt, o_sample, proj, gate_blk, gain, x, w_out, wi, norm_dim, name):
    m = x.shape[0]
    row = lambda i: (i, 0)
    return pl.pallas_call(
        functools.partial(_mix_out_body, norm_dim),
        grid=(m // OUT_TM,),
        in_specs=[
            pl.BlockSpec((OUT_TM, D_MODEL), lambda i: (_prompt_tile(i, OUT_TM), 0)),
            pl.BlockSpec((OUT_TM, D_MODEL), lambda i: (_sample_tile(i, OUT_TM), 0)),
            pl.BlockSpec((OUT_TM, D_MODEL), lambda i: (i, gate_blk)),
            pl.BlockSpec((1, D_MODEL), lambda i: (0, 0)),
            pl.BlockSpec((OUT_TM, D_MODEL), row),
            pl.BlockSpec((None, D_MODEL, D_MODEL), lambda i: (wi, 0, 0)),
        ],
        out_specs=pl.BlockSpec((OUT_TM, D_MODEL), row),
        out_shape=jax.ShapeDtypeStruct((m, D_MODEL), F32),
        compiler_params=_params("arbitrary"),
        name=name,
    )(o_prompt, o_sample, proj, gain, x, w_out)


def _gla_gate_body(x_ref, g_ref, wl_ref, wu_ref, b_ref, o_ref):
    xn = _rms(x_ref[...], g_ref[...]).astype(BF16)
    low = jnp.dot(xn, wl_ref[...], preferred_element_type=F32)
    z = jnp.dot(low.astype(BF16), wu_ref[...], preferred_element_type=F32) + b_ref[...]
    o_ref[...] = _log_sigmoid(z) / GLA_GATE_NORM


def _gla_gate(x, gains, li, w_low, w_up, bias):
    m = x.shape[0]
    tm = 512
    return pl.pallas_call(
        _gla_gate_body,
        grid=(m // tm,),
        in_specs=[
            pl.BlockSpec((tm, D_MODEL), lambda i: (i, 0)),
            pl.BlockSpec((None, 1, D_MODEL), lambda i: (li, 0, 0)),
            pl.BlockSpec((D_MODEL, LANES), lambda i: (0, 0)),
            pl.BlockSpec((LANES, GLA_DK), lambda i: (0, 0)),
            pl.BlockSpec((1, GLA_DK), lambda i: (0, 0)),
        ],
        out_specs=pl.BlockSpec((tm, GLA_DK), lambda i: (i, 0)),
        out_shape=jax.ShapeDtypeStruct((m, GLA_DK), F32),
        compiler_params=_params("parallel"),
        name="gla_gate",
    )(x, gains, w_low, w_up, bias)


def _pool_body(tt, n_prev, x_ref, g_ref, buf_ref, w_ref, sc_ref, y_ref, nb_ref, xe_ref):
    t = pl.program_id(1)
    pad = POOL_BUF + 1

    @pl.when(t == 0)
    def _():
        xe_ref[0:1, :] = jnp.zeros((1, D_MODEL), F32)
        xe_ref[1:pad, :] = buf_ref[...]

    @pl.when(t > 0)
    def _():
        xe_ref[0:pad, :] = xe_ref[tt:tt + pad, :]

    x = x_ref[...]
    xn = _rms(x, g_ref[...])
    xe_ref[pad:pad + tt, :] = xn
    pos = t * tt + lax.broadcasted_iota(jnp.int32, (tt, 1), 0)
    for gi, w in enumerate(POOL_WINDOWS):
        sl = slice(gi * POOL_GROUP_DIM, (gi + 1) * POOL_GROUP_DIM)
        cur = xn[:, sl]
        win = cur
        for d in range(1, w):
            win = win + xe_ref[pad - d:pad - d + tt, sl]
        cnt = jnp.minimum(w, pos + 1 + n_prev).astype(F32)
        y = (win / cnt - cur).astype(BF16)
        h = jnp.dot(y, w_ref[gi], preferred_element_type=F32) * sc_ref[:, sl]
        y_ref[:, sl] = x[:, sl] + h

    @pl.when(t == pl.num_programs(1) - 1)
    def _():
        nb_ref[...] = xe_ref[tt + 1:tt + pad, :]


def _pool(x, gains, li, buf, w_group, scale, n_seq, seq_len, tt, row0, n_prev):
    nt = seq_len // tt
    r0 = row0 // tt
    row = lambda b, t: (r0 + b * nt + t, 0)
    return pl.pallas_call(
        functools.partial(_pool_body, tt, n_prev),
        grid=(n_seq, nt),
        in_specs=[
            pl.BlockSpec((tt, D_MODEL), row),
            pl.BlockSpec((None, 1, D_MODEL), lambda b, t: (li, 0, 0)),
            pl.BlockSpec((None, POOL_BUF, D_MODEL), lambda b, t: (b, 0, 0)),
            pl.BlockSpec((len(POOL_WINDOWS), POOL_GROUP_DIM, POOL_GROUP_DIM), lambda b, t: (0, 0, 0)),
            pl.BlockSpec((1, D_MODEL), lambda b, t: (0, 0)),
        ],
        out_specs=[
            pl.BlockSpec((tt, D_MODEL), row),
            pl.BlockSpec((None, POOL_BUF, D_MODEL), lambda b, t: (b, 0, 0)),
        ],
        out_shape=[
            jax.ShapeDtypeStruct(x.shape, F32),
            jax.ShapeDtypeStruct((n_seq, POOL_BUF, D_MODEL), F32),
        ],
        scratch_shapes=[pltpu.VMEM((tt + POOL_BUF + 1, D_MODEL), F32)],
        input_output_aliases={0: 0},
        compiler_params=_params("parallel", "arbitrary"),
        name="pool_mixer",
    )(x, gains, buf, w_group, scale)


GROUP_TM = 1024


def _join_body(p_ref, s_ref, o_ref):
    is_prompt = pl.program_id(0) < N_PROMPT // GROUP_TM

    @pl.when(is_prompt)
    def _():
        o_ref[...] = p_ref[...]

    @pl.when(jnp.logical_not(is_prompt))
    def _():
        o_ref[...] = s_ref[...]


def _join_groups(x_prompt, x_sample):
    return pl.pallas_call(
        _join_body,
        grid=(N_TOKENS // GROUP_TM,),
        in_specs=[pl.BlockSpec((GROUP_TM, D_MODEL), lambda i: (_prompt_tile(i, GROUP_TM), 0)),
                  pl.BlockSpec((GROUP_TM, D_MODEL), lambda i: (_sample_tile(i, GROUP_TM), 0))],
        out_specs=pl.BlockSpec((GROUP_TM, D_MODEL), lambda i: (i, 0)),
        out_shape=jax.ShapeDtypeStruct((N_TOKENS, D_MODEL), F32),
        compiler_params=_params("arbitrary"),
        name="join_groups",
    )(x_prompt, x_sample)


def _final_norm_body(x_ref, g_ref, p_ref, s_ref):
    is_prompt = pl.program_id(0) < N_PROMPT // GROUP_TM

    @pl.when(is_prompt)
    def _():
        p_ref[...] = _rms(x_ref[...], g_ref[...])

    @pl.when(jnp.logical_not(is_prompt))
    def _():
        s_ref[...] = _rms(x_ref[...], g_ref[...])


def _final_norm(x, gain):
    return pl.pallas_call(
        _final_norm_body,
        grid=(N_TOKENS // GROUP_TM,),
        in_specs=[pl.BlockSpec((GROUP_TM, D_MODEL), lambda i: (i, 0)),
                  pl.BlockSpec((1, D_MODEL), lambda i: (0, 0))],
        out_specs=[pl.BlockSpec((GROUP_TM, D_MODEL), lambda i: (_prompt_tile(i, GROUP_TM), 0)),
                   pl.BlockSpec((GROUP_TM, D_MODEL), lambda i: (_sample_tile(i, GROUP_TM), 0))],
        out_shape=[jax.ShapeDtypeStruct((N_PROMPT, D_MODEL), F32),
                   jax.ShapeDtypeStruct((N_SAMPLE, D_MODEL), F32)],
        compiler_params=_params("arbitrary"),
        name="final_norm",
    )(x, gain)


def kernel(x_prompt, x_sample, state_hgrn, state_gla, state_pool, norm_ffn1, ffn1_w_gate, ffn1_w_up, ffn1_w_down, norm_mix, norm_ffn2, ffn2_w_gate, ffn2_w_up, ffn2_w_down, hgrn_lb_logits, hgrn_w_in, hgrn_o_norm, hgrn_w_out, gla_w_in, gla_w_gate_up, gla_b_gate, gla_o_norm, gla_w_out, pool_w_group, pool_scale, final_norm):
    x = _join_groups(x_prompt.reshape(N_PROMPT, D_MODEL), x_sample.reshape(N_SAMPLE, D_MODEL))
    gains = lambda a: a.reshape(a.shape[0], 1, D_MODEL)
    norm_ffn1, norm_mix, norm_ffn2 = gains(norm_ffn1), gains(norm_mix), gains(norm_ffn2)
    ffn1 = (ffn1_w_gate, ffn1_w_up, ffn1_w_down)
    ffn2 = (ffn2_w_gate, ffn2_w_up, ffn2_w_down)
    hgrn_w_in_b = hgrn_w_in.astype(BF16)
    gla_w_in_b = gla_w_in.astype(BF16)
    hgrn_w_out_b = hgrn_w_out.astype(BF16)
    gla_w_out_b = gla_w_out.astype(BF16)
    lb = _lower_bounds(hgrn_lb_logits)

    new_hgrn_p, new_gla_p, new_pool_p, new_pool_s = [], [], [], []
    new_hgrn_s = new_gla_s = None
    for li in range(DEPTH):
        j = li // N_MIXERS
        kind = li % N_MIXERS
        x = _ffn(x, norm_ffn1, *ffn1, li)
        if kind == 0:
            proj = _norm_proj(x, norm_mix, li, hgrn_w_in_b, j, 4 * D_MODEL, "hgrn_in_proj")
            lb_l = lb[li:li + 1]
            groups = HG_HEADS // HGRN_HB
            o_p, st_p = _rec_prompt("hgrn", proj, lb_l, HG_HEADS, HGRN_HB, HG_FORGET_DIM, HG_HEAD_V,
                                    HGRN_SEG[j], 0, groups, 2 * groups)
            o_s, new_hgrn_s = _rec_sample(
                "hgrn",
                _sample_heads(proj, 0, HG_HEADS, HG_FORGET_DIM),
                _sample_heads(proj, D_MODEL, HG_HEADS, HG_FORGET_DIM),
                _sample_heads(proj, 2 * D_MODEL, HG_HEADS, HG_HEAD_V),
                lb_l.reshape(HG_HEADS, 1, HG_FORGET_DIM), state_hgrn, j, new_hgrn_s,
                HG_HEADS, HG_FORGET_DIM, HG_HEAD_V)
            new_hgrn_p.append(st_p)
            x = _mix_out(o_p, _unsplit_heads(o_s), proj, 3, hgrn_o_norm[j].reshape(1, D_MODEL), x,
                         hgrn_w_out_b, j, D_MODEL, "hgrn_out")
        elif kind == 1:
            n_main = 2 * GLA_DK + 2 * GLA_DV
            proj = _norm_proj(x, norm_mix, li, gla_w_in_b, j, n_main, "gla_in_proj")
            w_low = jnp.pad(gla_w_in[j, :, n_main:], ((0, 0), (0, LANES - GLA_GATE_RANK))).astype(BF16)
            w_up = jnp.pad(gla_w_gate_up[j], ((0, LANES - GLA_GATE_RANK), (0, 0))).astype(BF16)
            lf = _gla_gate(x, norm_mix, li, w_low, w_up, gla_b_gate[j].reshape(1, GLA_DK))
            kb = GLA_DK // GLA_HEAD_K
            vb = 2 * GLA_DK // GLA_HEAD_V
            o_p, st_p = _rec_prompt("gla", proj, lf, GLA_HEADS, GLA_HB, GLA_HEAD_K, GLA_HEAD_V,
                                    GLA_SEG, 0, kb, vb)
            o_s, new_gla_s = _rec_sample(
                "gla",
                _sample_heads(proj, 0, GLA_HEADS, GLA_HEAD_K),
                _sample_heads(proj, GLA_DK, GLA_HEADS, GLA_HEAD_K),
                _sample_heads(proj, 2 * GLA_DK, GLA_HEADS, GLA_HEAD_V),
                _sample_heads(lf, 0, GLA_HEADS, GLA_HEAD_K), state_gla, j, new_gla_s,
                GLA_HEADS, GLA_HEAD_K, GLA_HEAD_V)
            new_gla_p.append(st_p)
            x = _mix_out(o_p, _unsplit_heads(o_s), proj, (2 * GLA_DK + GLA_DV) // D_MODEL,
                         gla_o_norm[j].reshape(1, GLA_DV), x, gla_w_out_b, j,
                         GLA_HEAD_V, "gla_out")
        else:
            w_group = pool_w_group[j].astype(BF16)
            scale = pool_scale[j].reshape(1, D_MODEL)
            x, nb_p = _pool(x, norm_mix, li, jnp.zeros((BATCH, POOL_BUF, D_MODEL), F32), w_group, scale,
                            BATCH, SEQ, POOL_TT, 0, 0)
            x, nb_s = _pool(x, norm_mix, li, state_pool[j], w_group, scale,
                            DEC_BATCH, DEC_SEQ, DEC_SEQ, N_PROMPT, min(POOL_BUF, PAST_LEN))
            new_pool_p.append(nb_p)
            new_pool_s.append(nb_s)
        x = _ffn(x, norm_ffn2, *ffn2, li)

    y_p, y_s = _final_norm(x, final_norm.reshape(1, D_MODEL))
    return (y_p.reshape(BATCH, SEQ, D_MODEL), y_s.reshape(DEC_BATCH, DEC_SEQ, D_MODEL),
            jnp.stack(new_hgrn_p), jnp.stack(new_gla_p), jnp.stack(new_pool_p),
            new_hgrn_s, new_gla_s, jnp.stack(new_pool_s))
```

```python
import functools

import jax
import jax.numpy as jnp
from jax import lax
from jax.experimental import pallas as pl
from jax.experimental.pallas import tpu as pltpu

F32 = jnp.float32
BF16 = jnp.bfloat16

D_MODEL = 2048
BATCH = 4
SEQ = 2048
DEPTH = 4
DEC_BATCH = 128
DEC_SEQ = 8
PAST_LEN = 16384
N_MIXERS = 3

HG_FORGET_DIM = 128
HG_HEADS = D_MODEL // HG_FORGET_DIM
HG_HEAD_V = D_MODEL // HG_HEADS

GLA_HEADS = 4
GLA_DK = D_MODEL // 2
GLA_DV = D_MODEL
GLA_HEAD_K = GLA_DK // GLA_HEADS
GLA_HEAD_V = GLA_DV // GLA_HEADS
GLA_GATE_RANK = 16
GLA_GATE_NORM = 16.0

POOL_WINDOWS = (2, 4, 8, 16)
POOL_GROUP_DIM = D_MODEL // len(POOL_WINDOWS)
POOL_BUF = max(POOL_WINDOWS) - 1

D_FF = 5632
EPS = 1e-6

N_PROMPT = BATCH * SEQ
N_SAMPLE = DEC_BATCH * DEC_SEQ
N_TOKENS = N_PROMPT + N_SAMPLE

LANES = 128
VMEM_LIMIT = 56 * 1024 * 1024

FFN_TM, FFN_TF = 1024, 256
PROJ_TM, PROJ_TN = 1024, 1024
OUT_TM = 256
REC_TT = 256
REC_C = 16
HGRN_HB = 4
HGRN_SEG = (32, 128)
GLA_HB, GLA_SEG = 1, 128
MXU_EXP_RANGE = 60.0
MXU_KEY_RANGE = 1e8
SAMPLE_BS = 4
POOL_TT = 256


def _params(*sem):
    return pltpu.CompilerParams(dimension_semantics=sem, vmem_limit_bytes=VMEM_LIMIT)


def _rms(xf, g):
    return xf * lax.rsqrt(jnp.mean(xf * xf, axis=-1, keepdims=True) + EPS) * g


def _silu(a):
    return a * jax.nn.sigmoid(a)


def _log1p_exp_neg_abs(a):
    return jnp.log(1.0 + jnp.exp(-jnp.abs(a)))


def _log_sigmoid(a):
    return jnp.minimum(a, 0.0) - _log1p_exp_neg_abs(a)


def _ffn_body(x_ref, g_ref, wg_ref, wu_ref, wd_ref, o_ref, xn_ref):
    f = pl.program_id(1)

    @pl.when(f == 0)
    def _():
        xn_ref[...] = _rms(x_ref[...], g_ref[...]).astype(BF16)
        o_ref[...] = jnp.zeros_like(o_ref)

    xn = xn_ref[...]
    a = jnp.dot(xn, wg_ref[...].astype(BF16), preferred_element_type=F32)
    u = jnp.dot(xn, wu_ref[...].astype(BF16), preferred_element_type=F32)
    h = (_silu(a) * u).astype(BF16)
    o_ref[...] += jnp.dot(h, wd_ref[...].astype(BF16), preferred_element_type=F32)

    @pl.when(f == pl.num_programs(1) - 1)
    def _():
        o_ref[...] = x_ref[...] + 0.5 * o_ref[...]


def _ffn(x, gains, w_gate, w_up, w_down, li):
    m = x.shape[0]
    return pl.pallas_call(
        _ffn_body,
        grid=(m // FFN_TM, D_FF // FFN_TF),
        in_specs=[
            pl.BlockSpec((FFN_TM, D_MODEL), lambda i, f: (i, 0)),
            pl.BlockSpec((None, 1, D_MODEL), lambda i, f: (li, 0, 0)),
            pl.BlockSpec((None, D_MODEL, FFN_TF), lambda i, f: (li, 0, f)),
            pl.BlockSpec((None, D_MODEL, FFN_TF), lambda i, f: (li, 0, f)),
            pl.BlockSpec((None, FFN_TF, D_MODEL), lambda i, f: (li, f, 0)),
        ],
        out_specs=pl.BlockSpec((FFN_TM, D_MODEL), lambda i, f: (i, 0)),
        out_shape=jax.ShapeDtypeStruct((m, D_MODEL), F32),
        scratch_shapes=[pltpu.VMEM((FFN_TM, D_MODEL), BF16)],
        compiler_params=_params("parallel", "arbitrary"),
        name="ffn",
    )(x, gains, w_gate, w_up, w_down)


def _proj_body(x_ref, g_ref, w_ref, o_ref, xn_ref):
    @pl.when(pl.program_id(1) == 0)
    def _():
        xn_ref[...] = _rms(x_ref[...], g_ref[...]).astype(BF16)

    o_ref[...] = jnp.dot(xn_ref[...], w_ref[...].astype(BF16), preferred_element_type=F32)


def _norm_proj(x, gains, li, w, wi, n, name):
    m = x.shape[0]
    return pl.pallas_call(
        _proj_body,
        grid=(m // PROJ_TM, n // PROJ_TN),
        in_specs=[
            pl.BlockSpec((PROJ_TM, D_MODEL), lambda i, j: (i, 0)),
            pl.BlockSpec((None, 1, D_MODEL), lambda i, j: (li, 0, 0)),
            pl.BlockSpec((None, D_MODEL, PROJ_TN), lambda i, j: (wi, 0, j)),
        ],
        out_specs=pl.BlockSpec((PROJ_TM, PROJ_TN), lambda i, j: (i, j)),
        out_shape=jax.ShapeDtypeStruct((m, n), F32),
        scratch_shapes=[pltpu.VMEM((PROJ_TM, D_MODEL), BF16)],
        compiler_params=_params("parallel", "arbitrary"),
        name=name,
    )(x, gains, w)


def _lb_body(l_ref, o_ref):
    l = l_ref[...]
    e = jnp.exp(l - jnp.max(l, axis=0, keepdims=True))
    p = e / jnp.sum(e, axis=0, keepdims=True)
    gamma0 = p[0:1]
    gamma = gamma0
    o_ref[0:1, :] = gamma - gamma0
    for i in range(1, DEPTH):
        gamma = gamma + p[i:i + 1]
        o_ref[i:i + 1, :] = gamma - gamma0


def _lower_bounds(logits):
    return pl.pallas_call(
        _lb_body,
        out_shape=jax.ShapeDtypeStruct((DEPTH, D_MODEL), F32),
        name="hgrn_lower_bounds",
    )(logits)


def _chunk_step(q, k, v, lf, s_t, state_is_kv=False):
    c, kdim = q.shape
    rows = lax.broadcasted_iota(jnp.int32, (c, kdim), 0)
    b = lf
    d = 1
    while d < c:
        b = b + jnp.where(rows >= d, pltpu.roll(b, d, axis=0), 0.0)
        d *= 2
    b_last = b[c - 1:c, :]

    state_contract = 0 if state_is_kv else 1
    o = lax.dot_general((q * jnp.exp(b)).astype(BF16), s_t.astype(BF16),
                        (((1,), (state_contract,)), ((), ())), preferred_element_type=F32)

    rows_c = lax.broadcasted_iota(jnp.int32, (c, 1), 0)
    for s in range(c):
        w = jnp.exp(b - b[s:s + 1, :]) * (q * k[s:s + 1, :])
        col = jnp.sum(w, axis=-1, keepdims=True)
        col = jnp.where(rows_c >= s, col, 0.0)
        o = o + col * v[s:s + 1, :]

    kd = (k * jnp.exp(b_last - b)).astype(BF16)
    contract_rows = (((0,), (0,)), ((), ()))
    if state_is_kv:
        upd = lax.dot_general(kd, v.astype(BF16), contract_rows, preferred_element_type=F32)
        decay = jnp.broadcast_to(jnp.exp(b_last), (LANES, kdim)).T
        return o, s_t * jnp.tile(decay, (1, v.shape[1] // LANES)) + upd
    upd = lax.dot_general(v.astype(BF16), kd, contract_rows, preferred_element_type=F32)
    return o, s_t * jnp.exp(b_last) + upd


def _hgrn_gates(q_raw, f_raw, lb):
    q = _silu(q_raw) * (HG_FORGET_DIM ** -0.5)
    gate = jnp.log1p(-lb) + _log_sigmoid(f_raw)
    log_lb = jnp.log(lb)
    lf = jnp.maximum(log_lb, gate) + _log1p_exp_neg_abs(log_lb - gate)
    k = (1.0 - lb) * jax.nn.sigmoid(-f_raw)
    return q, k, lf


def _seg_cumsum(x, c):
    r = lax.broadcasted_iota(jnp.int32, x.shape, 0) & (c - 1)
    d = 1
    while d < c:
        x = x + jnp.where(r >= d, pltpu.roll(x, d, axis=0), 0.0)
        d *= 2
    return x


def _rec_prompt_body(kind, hb, kdim, vdim, c, q_ref, k_ref, v_ref, aux_ref, o_ref, st_ref,
                     s_ref, qs_ref, ks_ref, ls_ref, bs_ref):
    t = pl.program_id(2)
    tt = REC_TT
    nseg = tt // c

    @pl.when(t == 0)
    def _():
        s_ref[...] = jnp.zeros_like(s_ref)

    worst = jnp.zeros((tt, kdim), F32)
    for h in range(hb):
        ksl = slice(h * kdim, (h + 1) * kdim)
        if kind == "hgrn":
            q, k, lf = _hgrn_gates(q_ref[:, ksl], k_ref[:, ksl], aux_ref[:, ksl])
        else:
            q = q_ref[:, ksl] * (GLA_HEAD_K ** -0.5)
            k = k_ref[:, ksl]
            lf = aux_ref[:, ksl]
        b = _seg_cumsum(lf, c)
        qs_ref[h] = q
        ks_ref[h] = k
        ls_ref[h] = lf
        bs_ref[h] = b
        worst = jnp.maximum(worst, jnp.maximum(-b, jnp.abs(k) * (MXU_EXP_RANGE / MXU_KEY_RANGE)))
    in_range = jnp.max(worst) <= MXU_EXP_RANGE

    @pl.when(in_range)
    def _():
        ri = lax.broadcasted_iota(jnp.int32, (tt, tt), 0)
        ci = lax.broadcasted_iota(jnp.int32, (tt, tt), 1)
        shift = c.bit_length() - 1
        mask = ((ri >> shift) == (ci >> shift)) & (ci <= ri)
        for h in range(hb):
            vsl = slice(h * vdim, (h + 1) * vdim)
            q, k, b = qs_ref[h], ks_ref[h], bs_ref[h]
            b3 = b.reshape(nseg, c, kdim)
            bl3 = b3[:, c - 1:c, :]
            qt = (q * jnp.exp(b)).astype(BF16)
            kt = (k * jnp.exp(-b)).astype(BF16)
            kd = (k * jnp.exp(bl3 - b3).reshape(tt, kdim)).astype(BF16)
            e = jnp.exp(bl3)
            vb = v_ref[:, vsl].astype(BF16)
            a = lax.dot_general(qt, kt, (((1,), (1,)), ((), ())), preferred_element_type=F32)
            att = jnp.where(mask, a, 0.0).astype(BF16)
            o = jnp.dot(att, vb, preferred_element_type=F32)
            segs = [slice(j * c, (j + 1) * c) for j in range(nseg)]
            upd = [lax.dot_general(vb[rs], kd[rs], (((0,), (0,)), ((), ())),
                                   preferred_element_type=F32) for rs in segs]
            s_t = s_ref[h]
            states = []
            for j in range(nseg):
                states.append(s_t.astype(BF16))
                s_t = s_t * e[j] + upd[j]
            s_ref[h] = s_t
            for j, rs in enumerate(segs):
                o_ref[rs, vsl] = o[rs] + lax.dot_general(
                    qt[rs], states[j], (((1,), (1,)), ((), ())), preferred_element_type=F32)

    @pl.when(jnp.logical_not(in_range))
    def _():
        for h in range(hb):
            vsl = slice(h * vdim, (h + 1) * vdim)

            def sub(j, carry):
                rs = pl.ds(pl.multiple_of(j * REC_C, REC_C), REC_C)
                o, s_new = _chunk_step(qs_ref[h, rs, :], ks_ref[h, rs, :], v_ref[rs, vsl],
                                       ls_ref[h, rs, :], s_ref[h])
                s_ref[h] = s_new
                o_ref[rs, vsl] = o
                return carry

            lax.fori_loop(0, tt // REC_C, sub, 0)

    @pl.when(t == pl.num_programs(2) - 1)
    def _():
        for h in range(hb):
            st_ref[h] = s_ref[h].T


def _rec_prompt(kind, proj, aux, heads, hb, kdim, vdim, c, q_blk, k_blk, v_blk):
    nt = SEQ // REC_TT
    row = lambda b, g, t: b * nt + t
    if kind == "hgrn":
        aux_spec = pl.BlockSpec((1, hb * kdim), lambda b, g, t: (0, g))
    else:
        aux_spec = pl.BlockSpec((REC_TT, hb * kdim), lambda b, g, t: (row(b, g, t), g))
    return pl.pallas_call(
        functools.partial(_rec_prompt_body, kind, hb, kdim, vdim, c),
        grid=(BATCH, heads // hb, nt),
        in_specs=[
            pl.BlockSpec((REC_TT, hb * kdim), lambda b, g, t: (row(b, g, t), q_blk + g)),
            pl.BlockSpec((REC_TT, hb * kdim), lambda b, g, t: (row(b, g, t), k_blk + g)),
            pl.BlockSpec((REC_TT, hb * vdim), lambda b, g, t: (row(b, g, t), v_blk + g)),
            aux_spec,
        ],
        out_specs=[
            pl.BlockSpec((REC_TT, hb * vdim), lambda b, g, t: (row(b, g, t), g)),
            pl.BlockSpec((None, hb, kdim, vdim), lambda b, g, t: (b, g, 0, 0)),
        ],
        out_shape=[
            jax.ShapeDtypeStruct((N_PROMPT, heads * vdim), F32),
            jax.ShapeDtypeStruct((BATCH, heads, kdim, vdim), F32),
        ],
        scratch_shapes=[pltpu.VMEM((hb, vdim, kdim), F32)]
        + [pltpu.VMEM((hb, REC_TT, kdim), F32)] * 4,
        compiler_params=_params("parallel", "parallel", "arbitrary"),
        name=kind + "_rec_prompt",
    )(proj, proj, proj, aux)


def _rec_sample_body(kind, heads, q_ref, k_ref, v_ref, aux_ref, st_ref, *rest):
    o_ref, nst_ref = rest[-2:]

    def item(i, carry):
        s = i // heads
        h = i % heads
        if kind == "hgrn":
            q, k, lf = _hgrn_gates(q_ref[s, h], k_ref[s, h], aux_ref[h])
        else:
            q = q_ref[s, h] * (GLA_HEAD_K ** -0.5)
            k = k_ref[s, h]
            lf = aux_ref[s, h]
        o, s_new = _chunk_step(q, k, v_ref[s, h], lf, st_ref[s, h], state_is_kv=True)
        o_ref[s, h] = o
        nst_ref[s, h] = s_new
        return carry

    lax.fori_loop(0, SAMPLE_BS * heads, item, 0, unroll=16 if kind == "hgrn" else 4)


def _rec_sample(kind, q, k, v, aux, states, j, new_states, heads, kdim, vdim):
    blk = lambda last2: pl.BlockSpec((SAMPLE_BS, heads) + last2, lambda i: (i, 0, 0, 0))
    st_spec = pl.BlockSpec((None, SAMPLE_BS, heads, kdim, vdim), lambda i: (j, i, 0, 0, 0))
    if kind == "hgrn":
        aux_spec = pl.BlockSpec((heads, 1, kdim), lambda i: (0, 0, 0))
    else:
        aux_spec = blk((DEC_SEQ, kdim))
    in_specs = [blk((DEC_SEQ, kdim)), blk((DEC_SEQ, kdim)), blk((DEC_SEQ, vdim)), aux_spec, st_spec]
    args = [q, k, v, aux, states]
    aliases = {}
    if new_states is not None:
        in_specs.append(pl.BlockSpec(memory_space=pl.ANY))
        args.append(new_states)
        aliases = {len(args) - 1: 1}
    return pl.pallas_call(
        functools.partial(_rec_sample_body, kind, heads),
        grid=(DEC_BATCH // SAMPLE_BS,),
        in_specs=in_specs,
        out_specs=[blk((DEC_SEQ, vdim)), st_spec],
        out_shape=[
            jax.ShapeDtypeStruct((DEC_BATCH, heads, DEC_SEQ, vdim), F32),
            jax.ShapeDtypeStruct(states.shape, F32),
        ],
        input_output_aliases=aliases,
        compiler_params=_params("parallel"),
        name=kind + "_rec_sample",
    )(*args)


def _sample_heads(proj, col0, heads, hdim):
    a = proj[N_PROMPT:, col0:col0 + heads * hdim]
    return a.reshape(DEC_BATCH, DEC_SEQ, heads, hdim).transpose(0, 2, 1, 3)


def _unsplit_heads(o_sample):
    return o_sample.transpose(0, 2, 1, 3).reshape(N_SAMPLE, D_MODEL)


def _prompt_tile(i, tile):
    return jnp.minimum(i, N_PROMPT // tile - 1)


def _sample_tile(i, tile):
    return jnp.maximum(i - N_PROMPT // tile, 0)


def _mix_out_body(norm_dim, op_ref, os_ref, gate_ref, gain_ref, x_ref, w_ref, y_ref):
    def run(o_ref):
        half = OUT_TM // 2
        for r0 in (0, half):
            rs = slice(r0, r0 + half)
            parts = []
            for c0 in range(0, D_MODEL, norm_dim):
                sl = slice(c0, c0 + norm_dim)
                on = _rms(o_ref[rs, sl], gain_ref[:, sl])
                parts.append((on * _silu(gate_ref[rs, sl])).astype(BF16))
            lhs = parts[0] if len(parts) == 1 else jnp.concatenate(parts, axis=1)
            y_ref[rs, :] = x_ref[rs, :] + jnp.dot(lhs, w_ref[...], preferred_element_type=F32)

    is_prompt = pl.program_id(0) < N_PROMPT // OUT_TM
    pl.when(is_prompt)(lambda: run(op_ref))
    pl.when(jnp.logical_not(is_prompt))(lambda: run(os_ref))


def _mix_out(o_prompt, o_sample, proj, gate_blk, gain, x, w_out, wi, norm_dim, name):
    m = x.shape[0]
    row = lambda i: (i, 0)
    return pl.pallas_call(
        functools.partial(_mix_out_body, norm_dim),
        grid=(m // OUT_TM,),
        in_specs=[
            pl.BlockSpec((OUT_TM, D_MODEL), lambda i: (_prompt_tile(i, OUT_TM), 0)),
            pl.BlockSpec((OUT_TM, D_MODEL), lambda i: (_sample_tile(i, OUT_TM), 0)),
            pl.BlockSpec((OUT_TM, D_MODEL), lambda i: (i, gate_blk)),
            pl.BlockSpec((1, D_MODEL), lambda i: (0, 0)),
            pl.BlockSpec((OUT_TM, D_MODEL), row),
            pl.BlockSpec((None, D_MODEL, D_MODEL), lambda i: (wi, 0, 0)),
        ],
        out_specs=pl.BlockSpec((OUT_TM, D_MODEL), row),
        out_shape=jax.ShapeDtypeStruct((m, D_MODEL), F32),
        compiler_params=_params("arbitrary"),
        name=name,
    )(o_prompt, o_sample, proj, gain, x, w_out)


def _gla_gate_body(x_ref, g_ref, wl_ref, wu_ref, b_ref, o_ref):
    xn = _rms(x_ref[...], g_ref[...]).astype(BF16)
    low = jnp.dot(xn, wl_ref[...], preferred_element_type=F32)
    z = jnp.dot(low.astype(BF16), wu_ref[...], preferred_element_type=F32) + b_ref[...]
    o_ref[...] = _log_sigmoid(z) / GLA_GATE_NORM


def _gla_gate(x, gains, li, w_low, w_up, bias):
    m = x.shape[0]
    tm = 512
    return pl.pallas_call(
        _gla_gate_body,
        grid=(m // tm,),
        in_specs=[
            pl.BlockSpec((tm, D_MODEL), lambda i: (i, 0)),
            pl.BlockSpec((None, 1, D_MODEL), lambda i: (li, 0, 0)),
            pl.BlockSpec((D_MODEL, LANES), lambda i: (0, 0)),
            pl.BlockSpec((LANES, GLA_DK), lambda i: (0, 0)),
            pl.BlockSpec((1, GLA_DK), lambda i: (0, 0)),
        ],
        out_specs=pl.BlockSpec((tm, GLA_DK), lambda i: (i, 0)),
        out_shape=jax.ShapeDtypeStruct((m, GLA_DK), F32),
        compiler_params=_params("parallel"),
        name="gla_gate",
    )(x, gains, w_low, w_up, bias)


def _pool_body(tt, n_prev, x_ref, g_ref, buf_ref, w_ref, sc_ref, y_ref, nb_ref, xe_ref):
    t = pl.program_id(1)
    pad = POOL_BUF + 1

    @pl.when(t == 0)
    def _():
        xe_ref[0:1, :] = jnp.zeros((1, D_MODEL), F32)
        xe_ref[1:pad, :] = buf_ref[...]

    @pl.when(t > 0)
    def _():
        xe_ref[0:pad, :] = xe_ref[tt:tt + pad, :]

    x = x_ref[...]
    xn = _rms(x, g_ref[...])
    xe_ref[pad:pad + tt, :] = xn
    pos = t * tt + lax.broadcasted_iota(jnp.int32, (tt, 1), 0)
    for gi, w in enumerate(POOL_WINDOWS):
        sl = slice(gi * POOL_GROUP_DIM, (gi + 1) * POOL_GROUP_DIM)
        cur = xn[:, sl]
        win = cur
        for d in range(1, w):
            win = win + xe_ref[pad - d:pad - d + tt, sl]
        cnt = jnp.minimum(w, pos + 1 + n_prev).astype(F32)
        y = (win / cnt - cur).astype(BF16)
        h = jnp.dot(y, w_ref[gi], preferred_element_type=F32) * sc_ref[:, sl]
        y_ref[:, sl] = x[:, sl] + h

    @pl.when(t == pl.num_programs(1) - 1)
    def _():
        nb_ref[...] = xe_ref[tt + 1:tt + pad, :]


def _pool(x, gains, li, buf, w_group, scale, n_seq, seq_len, tt, row0, n_prev):
    nt = seq_len // tt
    r0 = row0 // tt
    row = lambda b, t: (r0 + b * nt + t, 0)
    return pl.pallas_call(
        functools.partial(_pool_body, tt, n_prev),
        grid=(n_seq, nt),
        in_specs=[
            pl.BlockSpec((tt, D_MODEL), row),
            pl.BlockSpec((None, 1, D_MODEL), lambda b, t: (li, 0, 0)),
            pl.BlockSpec((None, POOL_BUF, D_MODEL), lambda b, t: (b, 0, 0)),
            pl.BlockSpec((len(POOL_WINDOWS), POOL_GROUP_DIM, POOL_GROUP_DIM), lambda b, t: (0, 0, 0)),
            pl.BlockSpec((1, D_MODEL), lambda b, t: (0, 0)),
        ],
        out_specs=[
            pl.BlockSpec((tt, D_MODEL), row),
            pl.BlockSpec((None, POOL_BUF, D_MODEL), lambda b, t: (b, 0, 0)),
        ],
        out_shape=[
            jax.ShapeDtypeStruct(x.shape, F32),
            jax.ShapeDtypeStruct((n_seq, POOL_BUF, D_MODEL), F32),
        ],
        scratch_shapes=[pltpu.VMEM((tt + POOL_BUF + 1, D_MODEL), F32)],
        input_output_aliases={0: 0},
        compiler_params=_params("parallel", "arbitrary"),
        name="pool_mixer",
    )(x, gains, buf, w_group, scale)


GROUP_TM = 1024


def _join_body(p_ref, s_ref, o_ref):
    is_prompt = pl.program_id(0) < N_PROMPT // GROUP_TM

    @pl.when(is_prompt)
    def _():
        o_ref[...] = p_ref[...]

    @pl.when(jnp.logical_not(is_prompt))
    def _():
        o_ref[...] = s_ref[...]


def _join_groups(x_prompt, x_sample):
    return pl.pallas_call(
        _join_body,
        grid=(N_TOKENS // GROUP_TM,),
        in_specs=[pl.BlockSpec((GROUP_TM, D_MODEL), lambda i: (_prompt_tile(i, GROUP_TM), 0)),
                  pl.BlockSpec((GROUP_TM, D_MODEL), lambda i: (_sample_tile(i, GROUP_TM), 0))],
        out_specs=pl.BlockSpec((GROUP_TM, D_MODEL), lambda i: (i, 0)),
        out_shape=jax.ShapeDtypeStruct((N_TOKENS, D_MODEL), F32),
        compiler_params=_params("arbitrary"),
        name="join_groups",
    )(x_prompt, x_sample)


def _final_norm_body(x_ref, g_ref, p_ref, s_ref):
    is_prompt = pl.program_id(0) < N_PROMPT // GROUP_TM

    @pl.when(is_prompt)
    def _():
        p_ref[...] = _rms(x_ref[...], g_ref[...])

    @pl.when(jnp.logical_not(is_prompt))
    def _():
        s_ref[...] = _rms(x_ref[...], g_ref[...])


def _final_norm(x, gain):
    return pl.pallas_call(
        _final_norm_body,
        grid=(N_TOKENS // GROUP_TM,),
        in_specs=[pl.BlockSpec((GROUP_TM, D_MODEL), lambda i: (i, 0)),
                  pl.BlockSpec((1, D_MODEL), lambda i: (0, 0))],
        out_specs=[pl.BlockSpec((GROUP_TM, D_MODEL), lambda i: (_prompt_tile(i, GROUP_TM), 0)),
                   pl.BlockSpec((GROUP_TM, D_MODEL), lambda i: (_sample_tile(i, GROUP_TM), 0))],
        out_shape=[jax.ShapeDtypeStruct((N_PROMPT, D_MODEL), F32),
                   jax.ShapeDtypeStruct((N_SAMPLE, D_MODEL), F32)],
        compiler_params=_params("arbitrary"),
        name="final_norm",
    )(x, gain)


def kernel(x_prompt, x_sample, state_hgrn, state_gla, state_pool, norm_ffn1, ffn1_w_gate, ffn1_w_up, ffn1_w_down, norm_mix, norm_ffn2, ffn2_w_gate, ffn2_w_up, ffn2_w_down, hgrn_lb_logits, hgrn_w_in, hgrn_o_norm, hgrn_w_out, gla_w_in, gla_w_gate_up, gla_b_gate, gla_o_norm, gla_w_out, pool_w_group, pool_scale, final_norm):
    x = _join_groups(x_prompt.reshape(N_PROMPT, D_MODEL), x_sample.reshape(N_SAMPLE, D_MODEL))
    gains = lambda a: a.reshape(a.shape[0], 1, D_MODEL)
    norm_ffn1, norm_mix, norm_ffn2 = gains(norm_ffn1), gains(norm_mix), gains(norm_ffn2)
    ffn1 = (ffn1_w_gate, ffn1_w_up, ffn1_w_down)
    ffn2 = (ffn2_w_gate, ffn2_w_up, ffn2_w_down)
    hgrn_w_out_b = hgrn_w_out.astype(BF16)
    gla_w_out_b = gla_w_out.astype(BF16)
    lb = _lower_bounds(hgrn_lb_logits)

    new_hgrn_p, new_gla_p, new_pool_p, new_pool_s = [], [], [], []
    new_hgrn_s = new_gla_s = None
    for li in range(DEPTH):
        j = li // N_MIXERS
        kind = li % N_MIXERS
        x = _ffn(x, norm_ffn1, *ffn1, li)
        if kind == 0:
            proj = _norm_proj(x, norm_mix, li, hgrn_w_in, j, 4 * D_MODEL, "hgrn_in_proj")
            lb_l = lb[li:li + 1]
            groups = HG_HEADS // HGRN_HB
            o_p, st_p = _rec_prompt("hgrn", proj, lb_l, HG_HEADS, HGRN_HB, HG_FORGET_DIM, HG_HEAD_V,
                                    HGRN_SEG[j], 0, groups, 2 * groups)
            o_s, new_hgrn_s = _rec_sample(
                "hgrn",
                _sample_heads(proj, 0, HG_HEADS, HG_FORGET_DIM),
                _sample_heads(proj, D_MODEL, HG_HEADS, HG_FORGET_DIM),
                _sample_heads(proj, 2 * D_MODEL, HG_HEADS, HG_HEAD_V),
                lb_l.reshape(HG_HEADS, 1, HG_FORGET_DIM), state_hgrn, j, new_hgrn_s,
                HG_HEADS, HG_FORGET_DIM, HG_HEAD_V)
            new_hgrn_p.append(st_p)
            x = _mix_out(o_p, _unsplit_heads(o_s), proj, 3, hgrn_o_norm[j].reshape(1, D_MODEL), x,
                         hgrn_w_out_b, j, D_MODEL, "hgrn_out")
        elif kind == 1:
            n_main = 2 * GLA_DK + 2 * GLA_DV
            proj = _norm_proj(x, norm_mix, li, gla_w_in, j, n_main, "gla_in_proj")
            w_low = jnp.pad(gla_w_in[j, :, n_main:], ((0, 0), (0, LANES - GLA_GATE_RANK))).astype(BF16)
            w_up = jnp.pad(gla_w_gate_up[j], ((0, LANES - GLA_GATE_RANK), (0, 0))).astype(BF16)
            lf = _gla_gate(x, norm_mix, li, w_low, w_up, gla_b_gate[j].reshape(1, GLA_DK))
            kb = GLA_DK // GLA_HEAD_K
            vb = 2 * GLA_DK // GLA_HEAD_V
            o_p, st_p = _rec_prompt("gla", proj, lf, GLA_HEADS, GLA_HB, GLA_HEAD_K, GLA_HEAD_V,
                                    GLA_SEG, 0, kb, vb)
            o_s, new_gla_s = _rec_sample(
                "gla",
                _sample_heads(proj, 0, GLA_HEADS, GLA_HEAD_K),
                _sample_heads(proj, GLA_DK, GLA_HEADS, GLA_HEAD_K),
                _sample_heads(proj, 2 * GLA_DK, GLA_HEADS, GLA_HEAD_V),
                _sample_heads(lf, 0, GLA_HEADS, GLA_HEAD_K), state_gla, j, new_gla_s,
                GLA_HEADS, GLA_HEAD_K, GLA_HEAD_V)
            new_gla_p.append(st_p)
            x = _mix_out(o_p, _unsplit_heads(o_s), proj, (2 * GLA_DK + GLA_DV) // D_MODEL,
                         gla_o_norm[j].reshape(1, GLA_DV), x, gla_w_out_b, j,
                         GLA_HEAD_V, "gla_out")
        else:
            w_group = pool_w_group[j].astype(BF16)
            scale = pool_scale[j].reshape(1, D_MODEL)
            x, nb_p = _pool(x, norm_mix, li, jnp.zeros((BATCH, POOL_BUF, D_MODEL), F32), w_group, scale,
                            BATCH, SEQ, POOL_TT, 0, 0)
            x, nb_s = _pool(x, norm_mix, li, state_pool[j], w_group, scale,
                            DEC_BATCH, DEC_SEQ, DEC_SEQ, N_PROMPT, min(POOL_BUF, PAST_LEN))
            new_pool_p.append(nb_p)
            new_pool_s.append(nb_s)
        x = _ffn(x, norm_ffn2, *ffn2, li)

    y_p, y_s = _final_norm(x, final_norm.reshape(1, D_MODEL))
    return (y_p.reshape(BATCH, SEQ, D_MODEL), y_s.reshape(DEC_BATCH, DEC_SEQ, D_MODEL),
            jnp.stack(new_hgrn_p), jnp.stack(new_gla_p), jnp.stack(new_pool_p),
            new_hgrn_s, new_gla_s, jnp.stack(new_pool_s))
```

```python
import functools

import jax
import jax.numpy as jnp
from jax import lax
from jax.experimental import pallas as pl
from jax.experimental.pallas import tpu as pltpu

F32 = jnp.float32
BF16 = jnp.bfloat16

D_MODEL = 2048
BATCH = 4
SEQ = 2048
DEPTH = 4
DEC_BATCH = 128
DEC_SEQ = 8
PAST_LEN = 16384
N_MIXERS = 3

HG_FORGET_DIM = 128
HG_HEADS = D_MODEL // HG_FORGET_DIM
HG_HEAD_V = D_MODEL // HG_HEADS

GLA_HEADS = 4
GLA_DK = D_MODEL // 2
GLA_DV = D_MODEL
GLA_HEAD_K = GLA_DK // GLA_HEADS
GLA_HEAD_V = GLA_DV // GLA_HEADS
GLA_GATE_RANK = 16
GLA_GATE_NORM = 16.0

POOL_WINDOWS = (2, 4, 8, 16)
POOL_GROUP_DIM = D_MODEL // len(POOL_WINDOWS)
POOL_BUF = max(POOL_WINDOWS) - 1

D_FF = 5632
EPS = 1e-6

N_PROMPT = BATCH * SEQ
N_SAMPLE = DEC_BATCH * DEC_SEQ
N_TOKENS = N_PROMPT + N_SAMPLE

LANES = 128
VMEM_LIMIT = 56 * 1024 * 1024

FFN_TM, FFN_TF = 1024, 256
PROJ_TM, PROJ_TN = 1024, 1024
OUT_TM = 256
REC_TT = 256
REC_C = 16
HGRN_HB = 8
HGRN_SEG = (32, 128)
GLA_HB, GLA_SEG = 1, 128
MXU_EXP_RANGE = 60.0
MXU_KEY_RANGE = 1e8
SAMPLE_BS = 4
POOL_TT = 256
POOL_SAMPLE_NS = 16


def _params(*sem):
    return pltpu.CompilerParams(dimension_semantics=sem, vmem_limit_bytes=VMEM_LIMIT)


def _rms(xf, g):
    return xf * lax.rsqrt(jnp.mean(xf * xf, axis=-1, keepdims=True) + EPS) * g


def _silu(a):
    return a * jax.nn.sigmoid(a)


def _log1p_exp_neg_abs(a):
    return jnp.log(1.0 + jnp.exp(-jnp.abs(a)))


def _log_sigmoid(a):
    return jnp.minimum(a, 0.0) - _log1p_exp_neg_abs(a)


def _ffn_body(x_ref, g_ref, wg_ref, wu_ref, wd_ref, o_ref, xn_ref):
    f = pl.program_id(1)

    @pl.when(f == 0)
    def _():
        xn_ref[...] = _rms(x_ref[...], g_ref[...]).astype(BF16)
        o_ref[...] = jnp.zeros_like(o_ref)

    xn = xn_ref[...]
    a = jnp.dot(xn, wg_ref[...].astype(BF16), preferred_element_type=F32)
    u = jnp.dot(xn, wu_ref[...].astype(BF16), preferred_element_type=F32)
    h = (_silu(a) * u).astype(BF16)
    o_ref[...] += jnp.dot(h, wd_ref[...].astype(BF16), preferred_element_type=F32)

    @pl.when(f == pl.num_programs(1) - 1)
    def _():
        o_ref[...] = x_ref[...] + 0.5 * o_ref[...]


def _ffn(x, gains, w_gate, w_up, w_down, li):
    m = x.shape[0]
    return pl.pallas_call(
        _ffn_body,
        grid=(m // FFN_TM, D_FF // FFN_TF),
        in_specs=[
            pl.BlockSpec((FFN_TM, D_MODEL), lambda i, f: (i, 0)),
            pl.BlockSpec((None, 1, D_MODEL), lambda i, f: (li, 0, 0)),
            pl.BlockSpec((None, D_MODEL, FFN_TF), lambda i, f: (li, 0, f)),
            pl.BlockSpec((None, D_MODEL, FFN_TF), lambda i, f: (li, 0, f)),
            pl.BlockSpec((None, FFN_TF, D_MODEL), lambda i, f: (li, f, 0)),
        ],
        out_specs=pl.BlockSpec((FFN_TM, D_MODEL), lambda i, f: (i, 0)),
        out_shape=jax.ShapeDtypeStruct((m, D_MODEL), F32),
        scratch_shapes=[pltpu.VMEM((FFN_TM, D_MODEL), BF16)],
        compiler_params=_params("parallel", "arbitrary"),
        name="ffn",
    )(x, gains, w_gate, w_up, w_down)


def _proj_body(x_ref, g_ref, w_ref, o_ref, xn_ref):
    @pl.when(pl.program_id(1) == 0)
    def _():
        xn_ref[...] = _rms(x_ref[...], g_ref[...]).astype(BF16)

    o_ref[...] = jnp.dot(xn_ref[...], w_ref[...].astype(BF16), preferred_element_type=F32)


def _norm_proj(x, gains, li, w, wi, n, name):
    m = x.shape[0]
    return pl.pallas_call(
        _proj_body,
        grid=(m // PROJ_TM, n // PROJ_TN),
        in_specs=[
            pl.BlockSpec((PROJ_TM, D_MODEL), lambda i, j: (i, 0)),
            pl.BlockSpec((None, 1, D_MODEL), lambda i, j: (li, 0, 0)),
            pl.BlockSpec((None, D_MODEL, PROJ_TN), lambda i, j: (wi, 0, j)),
        ],
        out_specs=pl.BlockSpec((PROJ_TM, PROJ_TN), lambda i, j: (i, j)),
        out_shape=jax.ShapeDtypeStruct((m, n), F32),
        scratch_shapes=[pltpu.VMEM((PROJ_TM, D_MODEL), BF16)],
        compiler_params=_params("parallel", "arbitrary"),
        name=name,
    )(x, gains, w)


def _lb_body(l_ref, o_ref):
    l = l_ref[...]
    e = jnp.exp(l - jnp.max(l, axis=0, keepdims=True))
    p = e / jnp.sum(e, axis=0, keepdims=True)
    gamma0 = p[0:1]
    gamma = gamma0
    o_ref[0:1, :] = gamma - gamma0
    for i in range(1, DEPTH):
        gamma = gamma + p[i:i + 1]
        o_ref[i:i + 1, :] = gamma - gamma0


def _lower_bounds(logits):
    return pl.pallas_call(
        _lb_body,
        out_shape=jax.ShapeDtypeStruct((DEPTH, D_MODEL), F32),
        name="hgrn_lower_bounds",
    )(logits)


def _chunk_step(q, k, v, lf, s_t, state_is_kv=False):
    c, kdim = q.shape
    rows = lax.broadcasted_iota(jnp.int32, (c, kdim), 0)
    b = lf
    d = 1
    while d < c:
        b = b + jnp.where(rows >= d, pltpu.roll(b, d, axis=0), 0.0)
        d *= 2
    b_last = b[c - 1:c, :]

    state_contract = 0 if state_is_kv else 1
    o = lax.dot_general((q * jnp.exp(b)).astype(BF16), s_t.astype(BF16),
                        (((1,), (state_contract,)), ((), ())), preferred_element_type=F32)

    rows_c = lax.broadcasted_iota(jnp.int32, (c, 1), 0)
    for s in range(c):
        w = jnp.exp(b - b[s:s + 1, :]) * (q * k[s:s + 1, :])
        col = jnp.sum(w, axis=-1, keepdims=True)
        col = jnp.where(rows_c >= s, col, 0.0)
        o = o + col * v[s:s + 1, :]

    kd = (k * jnp.exp(b_last - b)).astype(BF16)
    contract_rows = (((0,), (0,)), ((), ()))
    if state_is_kv:
        upd = lax.dot_general(kd, v.astype(BF16), contract_rows, preferred_element_type=F32)
        decay = jnp.broadcast_to(jnp.exp(b_last), (LANES, kdim)).T
        return o, s_t * jnp.tile(decay, (1, v.shape[1] // LANES)) + upd
    upd = lax.dot_general(v.astype(BF16), kd, contract_rows, preferred_element_type=F32)
    return o, s_t * jnp.exp(b_last) + upd


def _hgrn_gates(q_raw, f_raw, lb):
    q = _silu(q_raw) * (HG_FORGET_DIM ** -0.5)
    gate = jnp.log1p(-lb) + _log_sigmoid(f_raw)
    log_lb = jnp.log(lb)
    lf = jnp.maximum(log_lb, gate) + _log1p_exp_neg_abs(log_lb - gate)
    k = (1.0 - lb) * jax.nn.sigmoid(-f_raw)
    return q, k, lf


def _seg_cumsum(x, c):
    r = lax.broadcasted_iota(jnp.int32, x.shape, 0) & (c - 1)
    d = 1
    while d < c:
        x = x + jnp.where(r >= d, pltpu.roll(x, d, axis=0), 0.0)
        d *= 2
    return x


def _rec_prompt_body(kind, hb, kdim, vdim, c, q_ref, k_ref, v_ref, aux_ref, o_ref, st_ref,
                     s_ref, sn_ref, qs_ref, ks_ref, ls_ref):
    t = pl.program_id(2)
    tt = REC_TT
    nseg = tt // c

    @pl.when(t == 0)
    def _():
        s_ref[...] = jnp.zeros_like(s_ref)

    ri = lax.broadcasted_iota(jnp.int32, (tt, tt), 0)
    ci = lax.broadcasted_iota(jnp.int32, (tt, tt), 1)
    shift = c.bit_length() - 1
    mask = ((ri >> shift) == (ci >> shift)) & (ci <= ri)
    segs = [slice(j * c, (j + 1) * c) for j in range(nseg)]
    contract_last = (((1,), (1,)), ((), ()))
    contract_rows = (((0,), (0,)), ((), ()))

    worst = jnp.zeros((tt, kdim), F32)
    for h in range(hb):
        ksl = slice(h * kdim, (h + 1) * kdim)
        vsl = slice(h * vdim, (h + 1) * vdim)
        if kind == "hgrn":
            q, k, lf = _hgrn_gates(q_ref[:, ksl], k_ref[:, ksl], aux_ref[:, ksl])
        else:
            q = q_ref[:, ksl] * (GLA_HEAD_K ** -0.5)
            k = k_ref[:, ksl]
            lf = aux_ref[:, ksl]
        b = _seg_cumsum(lf, c)
        qs_ref[h] = q
        ks_ref[h] = k
        ls_ref[h] = lf
        worst = jnp.maximum(worst, jnp.maximum(-b, jnp.abs(k) * (MXU_EXP_RANGE / MXU_KEY_RANGE)))

        b3 = b.reshape(nseg, c, kdim)
        bl3 = b3[:, c - 1:c, :]
        qt = (q * jnp.exp(b)).astype(BF16)
        kt = (k * jnp.exp(-b)).astype(BF16)
        kd = (k * jnp.exp(bl3 - b3).reshape(tt, kdim)).astype(BF16)
        e = jnp.exp(bl3)
        vb = v_ref[:, vsl].astype(BF16)
        a = lax.dot_general(qt, kt, contract_last, preferred_element_type=F32)
        att = jnp.where(mask, a, 0.0).astype(BF16)
        o = jnp.dot(att, vb, preferred_element_type=F32)
        upd = [lax.dot_general(vb[rs], kd[rs], contract_rows, preferred_element_type=F32) for rs in segs]
        s_t = s_ref[h]
        states = []
        for j in range(nseg):
            states.append(s_t.astype(BF16))
            s_t = s_t * e[j] + upd[j]
        sn_ref[h] = s_t
        for j, rs in enumerate(segs):
            o_ref[rs, vsl] = o[rs] + lax.dot_general(qt[rs], states[j], contract_last,
                                                     preferred_element_type=F32)

    in_range = jnp.max(worst) <= MXU_EXP_RANGE

    @pl.when(jnp.logical_not(in_range))
    def _():
        for h in range(hb):
            vsl = slice(h * vdim, (h + 1) * vdim)
            sn_ref[h] = s_ref[h]

            def sub(j, carry):
                rs = pl.ds(pl.multiple_of(j * REC_C, REC_C), REC_C)
                o, s_new = _chunk_step(qs_ref[h, rs, :], ks_ref[h, rs, :], v_ref[rs, vsl],
                                       ls_ref[h, rs, :], sn_ref[h])
                sn_ref[h] = s_new
                o_ref[rs, vsl] = o
                return carry

            lax.fori_loop(0, tt // REC_C, sub, 0)

    s_ref[...] = sn_ref[...]

    @pl.when(t == pl.num_programs(2) - 1)
    def _():
        for h in range(hb):
            st_ref[h] = sn_ref[h].T


def _rec_prompt(kind, proj, aux, heads, hb, kdim, vdim, c, q_blk, k_blk, v_blk):
    nt = SEQ // REC_TT
    row = lambda b, g, t: b * nt + t
    if kind == "hgrn":
        aux_spec = pl.BlockSpec((1, hb * kdim), lambda b, g, t: (0, g))
    else:
        aux_spec = pl.BlockSpec((REC_TT, hb * kdim), lambda b, g, t: (row(b, g, t), g))
    return pl.pallas_call(
        functools.partial(_rec_prompt_body, kind, hb, kdim, vdim, c),
        grid=(BATCH, heads // hb, nt),
        in_specs=[
            pl.BlockSpec((REC_TT, hb * kdim), lambda b, g, t: (row(b, g, t), q_blk + g)),
            pl.BlockSpec((REC_TT, hb * kdim), lambda b, g, t: (row(b, g, t), k_blk + g)),
            pl.BlockSpec((REC_TT, hb * vdim), lambda b, g, t: (row(b, g, t), v_blk + g)),
            aux_spec,
        ],
        out_specs=[
            pl.BlockSpec((REC_TT, hb * vdim), lambda b, g, t: (row(b, g, t), g)),
            pl.BlockSpec((None, hb, kdim, vdim), lambda b, g, t: (b, g, 0, 0)),
        ],
        out_shape=[
            jax.ShapeDtypeStruct((N_PROMPT, heads * vdim), F32),
            jax.ShapeDtypeStruct((BATCH, heads, kdim, vdim), F32),
        ],
        scratch_shapes=[pltpu.VMEM((hb, vdim, kdim), F32)] * 2
        + [pltpu.VMEM((hb, REC_TT, kdim), F32)] * 3,
        compiler_params=_params("parallel", "parallel", "arbitrary"),
        name=kind + "_rec_prompt",
    )(proj, proj, proj, aux)


def _rec_sample_body(kind, heads, slot, q_ref, k_ref, v_ref, aux_ref, st_ref, *rest):
    o_ref, nst_ref = rest[-2:]

    for other in range(nst_ref.shape[0]):
        if other != slot:
            nst_ref[other] = jnp.zeros(nst_ref.shape[1:], F32)

    def item(i, carry):
        s = i // heads
        h = i % heads
        if kind == "hgrn":
            q, k, lf = _hgrn_gates(q_ref[s, h], k_ref[s, h], aux_ref[h])
        else:
            q = q_ref[s, h] * (GLA_HEAD_K ** -0.5)
            k = k_ref[s, h]
            lf = aux_ref[s, h]
        o, s_new = _chunk_step(q, k, v_ref[s, h], lf, st_ref[s, h], state_is_kv=True)
        o_ref[s, h] = o
        nst_ref[slot, s, h] = s_new
        return carry

    lax.fori_loop(0, SAMPLE_BS * heads, item, 0, unroll=16 if kind == "hgrn" else 4)


def _rec_sample(kind, q, k, v, aux, states, j, new_states, heads, kdim, vdim):
    blk = lambda last2: pl.BlockSpec((SAMPLE_BS, heads) + last2, lambda i: (i, 0, 0, 0))
    st_spec = pl.BlockSpec((None, SAMPLE_BS, heads, kdim, vdim), lambda i: (j, i, 0, 0, 0))
    if kind == "hgrn":
        aux_spec = pl.BlockSpec((heads, 1, kdim), lambda i: (0, 0, 0))
    else:
        aux_spec = blk((DEC_SEQ, kdim))
    in_specs = [blk((DEC_SEQ, kdim)), blk((DEC_SEQ, kdim)), blk((DEC_SEQ, vdim)), aux_spec, st_spec]
    args = [q, k, v, aux, states]
    if new_states is None:
        aliases, slot = {}, j
        new_spec = pl.BlockSpec((states.shape[0], SAMPLE_BS, heads, kdim, vdim), lambda i: (0, i, 0, 0, 0))
    else:
        in_specs.append(pl.BlockSpec(memory_space=pl.ANY))
        args.append(new_states)
        aliases, slot = {len(args) - 1: 1}, 0
        new_spec = pl.BlockSpec((1, SAMPLE_BS, heads, kdim, vdim), lambda i: (j, i, 0, 0, 0))
    return pl.pallas_call(
        functools.partial(_rec_sample_body, kind, heads, slot),
        grid=(DEC_BATCH // SAMPLE_BS,),
        in_specs=in_specs,
        out_specs=[blk((DEC_SEQ, vdim)), new_spec],
        out_shape=[
            jax.ShapeDtypeStruct((DEC_BATCH, heads, DEC_SEQ, vdim), F32),
            jax.ShapeDtypeStruct(states.shape, F32),
        ],
        input_output_aliases=aliases,
        compiler_params=_params("parallel"),
        name=kind + "_rec_sample",
    )(*args)


def _sample_heads(proj, col0, heads, hdim):
    a = proj[N_PROMPT:, col0:col0 + heads * hdim]
    return a.reshape(DEC_BATCH, DEC_SEQ, heads, hdim).transpose(0, 2, 1, 3)


def _unsplit_heads(o_sample):
    return o_sample.transpose(0, 2, 1, 3).reshape(N_SAMPLE, D_MODEL)


def _prompt_tile(i, tile):
    return jnp.minimum(i, N_PROMPT // tile - 1)


def _sample_tile(i, tile):
    return jnp.maximum(i - N_PROMPT // tile, 0)


def _mix_out_body(norm_dim, op_ref, os_ref, gate_ref, gain_ref, x_ref, w_ref, y_ref):
    def run(o_ref):
        half = OUT_TM // 2
        for r0 in (0, half):
            rs = slice(r0, r0 + half)
            parts = []
            for c0 in range(0, D_MODEL, norm_dim):
                sl = slice(c0, c0 + norm_dim)
                on = _rms(o_ref[rs, sl], gain_ref[:, sl])
                parts.append((on * _silu(gate_ref[rs, sl])).astype(BF16))
            lhs = parts[0] if len(parts) == 1 else jnp.concatenate(parts, axis=1)
            y_ref[rs, :] = x_ref[rs, :] + jnp.dot(lhs, w_ref[...], preferred_element_type=F32)

    is_prompt = pl.program_id(0) < N_PROMPT // OUT_TM
    pl.when(is_prompt)(lambda: run(op_ref))
    pl.when(jnp.logical_not(is_prompt))(lambda: run(os_ref))


def _mix_out(o_prompt, o_sample, proj, gate_blk, gain, x, w_out, wi, norm_dim, name):
    m = x.shape[0]
    row = lambda i: (i, 0)
    return pl.pallas_call(
        functools.partial(_mix_out_body, norm_dim),
        grid=(m // OUT_TM,),
        in_specs=[
            pl.BlockSpec((OUT_TM, D_MODEL), lambda i: (_prompt_tile(i, OUT_TM), 0)),
            pl.BlockSpec((OUT_TM, D_MODEL), lambda i: (_sample_tile(i, OUT_TM), 0)),
            pl.BlockSpec((OUT_TM, D_MODEL), lambda i: (i, gate_blk)),
            pl.BlockSpec((1, D_MODEL), lambda i: (0, 0)),
            pl.BlockSpec((OUT_TM, D_MODEL), row),
            pl.BlockSpec((None, D_MODEL, D_MODEL), lambda i: (wi, 0, 0)),
        ],
        out_specs=pl.BlockSpec((OUT_TM, D_MODEL), row),
        out_shape=jax.ShapeDtypeStruct((m, D_MODEL), F32),
        compiler_params=_params("arbitrary"),
        name=name,
    )(o_prompt, o_sample, proj, gain, x, w_out)


def _gla_gate_body(x_ref, g_ref, wl_ref, wu_ref, b_ref, o_ref):
    xn = _rms(x_ref[...], g_ref[...]).astype(BF16)
    low = jnp.dot(xn, wl_ref[...], preferred_element_type=F32)
    z = jnp.dot(low.astype(BF16), wu_ref[...], preferred_element_type=F32) + b_ref[...]
    o_ref[...] = _log_sigmoid(z) / GLA_GATE_NORM


def _gla_gate(x, gains, li, w_low, w_up, bias):
    m = x.shape[0]
    tm = 512
    return pl.pallas_call(
        _gla_gate_body,
        grid=(m // tm,),
        in_specs=[
            pl.BlockSpec((tm, D_MODEL), lambda i: (i, 0)),
            pl.BlockSpec((None, 1, D_MODEL), lambda i: (li, 0, 0)),
            pl.BlockSpec((D_MODEL, LANES), lambda i: (0, 0)),
            pl.BlockSpec((LANES, GLA_DK), lambda i: (0, 0)),
            pl.BlockSpec((1, GLA_DK), lambda i: (0, 0)),
        ],
        out_specs=pl.BlockSpec((tm, GLA_DK), lambda i: (i, 0)),
        out_shape=jax.ShapeDtypeStruct((m, GLA_DK), F32),
        compiler_params=_params("parallel"),
        name="gla_gate",
    )(x, gains, w_low, w_up, bias)


def _pool_body(ns, tt, n_prev, x_ref, g_ref, buf_ref, w_ref, sc_ref, y_ref, nb_ref, xe_ref):
    t = pl.program_id(1)
    pad = POOL_BUF + 1

    @pl.when(t == 0)
    def _():
        xe_ref[:, 0:1, :] = jnp.zeros((ns, 1, D_MODEL), F32)
        xe_ref[:, 1:pad, :] = buf_ref[...]

    @pl.when(t > 0)
    def _():
        xe_ref[:, 0:pad, :] = xe_ref[:, tt:tt + pad, :]

    x = x_ref[...]
    xn = _rms(x, g_ref[...])
    xe_ref[:, pad:pad + tt, :] = xn.reshape(ns, tt, D_MODEL)
    pos = t * tt + lax.broadcasted_iota(jnp.int32, (1, tt, 1), 1)
    for gi, w in enumerate(POOL_WINDOWS):
        sl = slice(gi * POOL_GROUP_DIM, (gi + 1) * POOL_GROUP_DIM)
        cur = xn[:, sl].reshape(ns, tt, POOL_GROUP_DIM)
        win = cur
        for d in range(1, w):
            win = win + xe_ref[:, pad - d:pad - d + tt, sl]
        cnt = jnp.minimum(w, pos + 1 + n_prev).astype(F32)
        y = (win / cnt - cur).reshape(ns * tt, POOL_GROUP_DIM).astype(BF16)
        h = jnp.dot(y, w_ref[gi], preferred_element_type=F32) * sc_ref[:, sl]
        y_ref[:, sl] = x[:, sl] + h

    @pl.when(t == pl.num_programs(1) - 1)
    def _():
        nb_ref[...] = xe_ref[:, tt + 1:tt + pad, :]


def _pool(x, gains, li, buf, w_group, scale, n_seq, seq_len, ns, tt, row0, n_prev):
    nt = seq_len // tt
    assert ns == 1 or nt == 1
    r0 = row0 // (ns * tt)
    row = lambda b, t: (r0 + b * nt + t, 0)
    return pl.pallas_call(
        functools.partial(_pool_body, ns, tt, n_prev),
        grid=(n_seq // ns, nt),
        in_specs=[
            pl.BlockSpec((ns * tt, D_MODEL), row),
            pl.BlockSpec((None, 1, D_MODEL), lambda b, t: (li, 0, 0)),
            pl.BlockSpec((ns, POOL_BUF, D_MODEL), lambda b, t: (b, 0, 0)),
            pl.BlockSpec((len(POOL_WINDOWS), POOL_GROUP_DIM, POOL_GROUP_DIM), lambda b, t: (0, 0, 0)),
            pl.BlockSpec((1, D_MODEL), lambda b, t: (0, 0)),
        ],
        out_specs=[
            pl.BlockSpec((ns * tt, D_MODEL), row),
            pl.BlockSpec((ns, POOL_BUF, D_MODEL), lambda b, t: (b, 0, 0)),
        ],
        out_shape=[
            jax.ShapeDtypeStruct(x.shape, F32),
            jax.ShapeDtypeStruct((n_seq, POOL_BUF, D_MODEL), F32),
        ],
        scratch_shapes=[pltpu.VMEM((ns, tt + POOL_BUF + 1, D_MODEL), F32)],
        input_output_aliases={0: 0},
        compiler_params=_params("parallel", "arbitrary"),
        name="pool_mixer",
    )(x, gains, buf, w_group, scale)


GROUP_TM = 1024


def _join_body(p_ref, s_ref, o_ref):
    is_prompt = pl.program_id(0) < N_PROMPT // GROUP_TM

    @pl.when(is_prompt)
    def _():
        o_ref[...] = p_ref[...]

    @pl.when(jnp.logical_not(is_prompt))
    def _():
        o_ref[...] = s_ref[...]


def _join_groups(x_prompt, x_sample):
    return pl.pallas_call(
        _join_body,
        grid=(N_TOKENS // GROUP_TM,),
        in_specs=[pl.BlockSpec((GROUP_TM, D_MODEL), lambda i: (_prompt_tile(i, GROUP_TM), 0)),
                  pl.BlockSpec((GROUP_TM, D_MODEL), lambda i: (_sample_tile(i, GROUP_TM), 0))],
        out_specs=pl.BlockSpec((GROUP_TM, D_MODEL), lambda i: (i, 0)),
        out_shape=jax.ShapeDtypeStruct((N_TOKENS, D_MODEL), F32),
        compiler_params=_params("arbitrary"),
        name="join_groups",
    )(x_prompt, x_sample)


def _final_norm_body(x_ref, g_ref, p_ref, s_ref):
    is_prompt = pl.program_id(0) < N_PROMPT // GROUP_TM

    @pl.when(is_prompt)
    def _():
        p_ref[...] = _rms(x_ref[...], g_ref[...])

    @pl.when(jnp.logical_not(is_prompt))
    def _():
        s_ref[...] = _rms(x_ref[...], g_ref[...])


def _final_norm(x, gain):
    return pl.pallas_call(
        _final_norm_body,
        grid=(N_TOKENS // GROUP_TM,),
        in_specs=[pl.BlockSpec((GROUP_TM, D_MODEL), lambda i: (i, 0)),
                  pl.BlockSpec((1, D_MODEL), lambda i: (0, 0))],
        out_specs=[pl.BlockSpec((GROUP_TM, D_MODEL), lambda i: (_prompt_tile(i, GROUP_TM), 0)),
                   pl.BlockSpec((GROUP_TM, D_MODEL), lambda i: (_sample_tile(i, GROUP_TM), 0))],
        out_shape=[jax.ShapeDtypeStruct((N_PROMPT, D_MODEL), F32),
                   jax.ShapeDtypeStruct((N_SAMPLE, D_MODEL), F32)],
        compiler_params=_params("arbitrary"),
        name="final_norm",
    )(x, gain)


def kernel(x_prompt, x_sample, state_hgrn, state_gla, state_pool, norm_ffn1, ffn1_w_gate, ffn1_w_up, ffn1_w_down, norm_mix, norm_ffn2, ffn2_w_gate, ffn2_w_up, ffn2_w_down, hgrn_lb_logits, hgrn_w_in, hgrn_o_norm, hgrn_w_out, gla_w_in, gla_w_gate_up, gla_b_gate, gla_o_norm, gla_w_out, pool_w_group, pool_scale, final_norm):
    x = _join_groups(x_prompt.reshape(N_PROMPT, D_MODEL), x_sample.reshape(N_SAMPLE, D_MODEL))
    gains = lambda a: a.reshape(a.shape[0], 1, D_MODEL)
    norm_ffn1, norm_mix, norm_ffn2 = gains(norm_ffn1), gains(norm_mix), gains(norm_ffn2)
    ffn1 = (ffn1_w_gate, ffn1_w_up, ffn1_w_down)
    ffn2 = (ffn2_w_gate, ffn2_w_up, ffn2_w_down)
    hgrn_w_out_b = hgrn_w_out.astype(BF16)
    gla_w_out_b = gla_w_out.astype(BF16)
    lb = _lower_bounds(hgrn_lb_logits)

    new_hgrn_p, new_gla_p, new_pool_p, new_pool_s = [], [], [], []
    new_hgrn_s = new_gla_s = None
    for li in range(DEPTH):
        j = li // N_MIXERS
        kind = li % N_MIXERS
        x = _ffn(x, norm_ffn1, *ffn1, li)
        if kind == 0:
            proj = _norm_proj(x, norm_mix, li, hgrn_w_in, j, 4 * D_MODEL, "hgrn_in_proj")
            lb_l = lb[li:li + 1]
            groups = HG_HEADS // HGRN_HB
            o_p, st_p = _rec_prompt("hgrn", proj, lb_l, HG_HEADS, HGRN_HB, HG_FORGET_DIM, HG_HEAD_V,
                                    HGRN_SEG[j], 0, groups, 2 * groups)
            o_s, new_hgrn_s = _rec_sample(
                "hgrn",
                _sample_heads(proj, 0, HG_HEADS, HG_FORGET_DIM),
                _sample_heads(proj, D_MODEL, HG_HEADS, HG_FORGET_DIM),
                _sample_heads(proj, 2 * D_MODEL, HG_HEADS, HG_HEAD_V),
                lb_l.reshape(HG_HEADS, 1, HG_FORGET_DIM), state_hgrn, j, new_hgrn_s,
                HG_HEADS, HG_FORGET_DIM, HG_HEAD_V)
            new_hgrn_p.append(st_p)
            x = _mix_out(o_p, _unsplit_heads(o_s), proj, 3, hgrn_o_norm[j].reshape(1, D_MODEL), x,
                         hgrn_w_out_b, j, D_MODEL, "hgrn_out")
        elif kind == 1:
            n_main = 2 * GLA_DK + 2 * GLA_DV
            proj = _norm_proj(x, norm_mix, li, gla_w_in, j, n_main, "gla_in_proj")
            w_low = jnp.pad(gla_w_in[j, :, n_main:], ((0, 0), (0, LANES - GLA_GATE_RANK))).astype(BF16)
            w_up = jnp.pad(gla_w_gate_up[j], ((0, LANES - GLA_GATE_RANK), (0, 0))).astype(BF16)
            lf = _gla_gate(x, norm_mix, li, w_low, w_up, gla_b_gate[j].reshape(1, GLA_DK))
            kb = GLA_DK // GLA_HEAD_K
            vb = 2 * GLA_DK // GLA_HEAD_V
            o_p, st_p = _rec_prompt("gla", proj, lf, GLA_HEADS, GLA_HB, GLA_HEAD_K, GLA_HEAD_V,
                                    GLA_SEG, 0, kb, vb)
            o_s, new_gla_s = _rec_sample(
                "gla",
                _sample_heads(proj, 0, GLA_HEADS, GLA_HEAD_K),
                _sample_heads(proj, GLA_DK, GLA_HEADS, GLA_HEAD_K),
                _sample_heads(proj, 2 * GLA_DK, GLA_HEADS, GLA_HEAD_V),
                _sample_heads(lf, 0, GLA_HEADS, GLA_HEAD_K), state_gla, j, new_gla_s,
                GLA_HEADS, GLA_HEAD_K, GLA_HEAD_V)
            new_gla_p.append(st_p)
            x = _mix_out(o_p, _unsplit_heads(o_s), proj, (2 * GLA_DK + GLA_DV) // D_MODEL,
                         gla_o_norm[j].reshape(1, GLA_DV), x, gla_w_out_b, j,
                         GLA_HEAD_V, "gla_out")
        else:
            w_group = pool_w_group[j].astype(BF16)
            scale = pool_scale[j].reshape(1, D_MODEL)
            x, nb_p = _pool(x, norm_mix, li, jnp.zeros((BATCH, POOL_BUF, D_MODEL), F32), w_group, scale,
                            BATCH, SEQ, 1, POOL_TT, 0, 0)
            x, nb_s = _pool(x, norm_mix, li, state_pool[j], w_group, scale,
                            DEC_BATCH, DEC_SEQ, POOL_SAMPLE_NS, DEC_SEQ, N_PROMPT, min(POOL_BUF, PAST_LEN))
            new_pool_p.append(nb_p)
            new_pool_s.append(nb_s)
        x = _ffn(x, norm_ffn2, *ffn2, li)

    y_p, y_s = _final_norm(x, final_norm.reshape(1, D_MODEL))
    return (y_p.reshape(BATCH, SEQ, D_MODEL), y_s.reshape(DEC_BATCH, DEC_SEQ, D_MODEL),
            jnp.stack(new_hgrn_p), jnp.stack(new_gla_p), jnp.stack(new_pool_p),
            new_hgrn_s, new_gla_s, jnp.stack(new_pool_s))
```

```python
import functools

import jax
import jax.numpy as jnp
from jax import lax
from jax.experimental import pallas as pl
from jax.experimental.pallas import tpu as pltpu

F32 = jnp.float32
BF16 = jnp.bfloat16

D_MODEL = 2048
BATCH = 4
SEQ = 2048
DEPTH = 4
DEC_BATCH = 128
DEC_SEQ = 8
PAST_LEN = 16384
N_MIXERS = 3

HG_FORGET_DIM = 128
HG_HEADS = D_MODEL // HG_FORGET_DIM
HG_HEAD_V = D_MODEL // HG_HEADS

GLA_HEADS = 4
GLA_DK = D_MODEL // 2
GLA_DV = D_MODEL
GLA_HEAD_K = GLA_DK // GLA_HEADS
GLA_HEAD_V = GLA_DV // GLA_HEADS
GLA_GATE_RANK = 16
GLA_GATE_NORM = 16.0

POOL_WINDOWS = (2, 4, 8, 16)
POOL_GROUP_DIM = D_MODEL // len(POOL_WINDOWS)
POOL_BUF = max(POOL_WINDOWS) - 1

D_FF = 5632
EPS = 1e-6

N_PROMPT = BATCH * SEQ
N_SAMPLE = DEC_BATCH * DEC_SEQ
N_TOKENS = N_PROMPT + N_SAMPLE

LANES = 128
VMEM_LIMIT = 56 * 1024 * 1024

FFN_TM, FFN_TF = 1024, 256
FFN_X_CHUNKS = 8
PROJ_TM, PROJ_TN = 1024, 1024
OUT_TM = 256
REC_TT = 256
REC_C = 16
HGRN_HB = 8
HGRN_SEG = (32, 128)
GLA_HB, GLA_SEG = 1, 128
MXU_EXP_RANGE = 60.0
MXU_KEY_RANGE = 1e8
SAMPLE_BS = 4
POOL_TT = 256
POOL_SAMPLE_NS = 16


def _params(*sem):
    return pltpu.CompilerParams(dimension_semantics=sem, vmem_limit_bytes=VMEM_LIMIT)


def _rms(xf, g):
    return xf * lax.rsqrt(jnp.mean(xf * xf, axis=-1, keepdims=True) + EPS) * g


def _silu(a):
    return a * jax.nn.sigmoid(a)


def _log1p_exp_neg_abs(a):
    return jnp.log(1.0 + jnp.exp(-jnp.abs(a)))


def _log_sigmoid(a):
    return jnp.minimum(a, 0.0) - _log1p_exp_neg_abs(a)


def _ffn_body(x_hbm, g_ref, wg_ref, wu_ref, wd_ref, o_ref, xs_ref, xn_ref, sems):
    i = pl.program_id(0)
    f = pl.program_id(1)
    rows = FFN_TM // FFN_X_CHUNKS

    def chunk_copy(tile, c):
        return pltpu.make_async_copy(x_hbm.at[pl.ds(tile * FFN_TM + c * rows, rows), :],
                                     xs_ref.at[pl.ds(c * rows, rows), :], sems.at[c])

    @pl.when((i == 0) & (f == 0))
    def _():
        for c in range(FFN_X_CHUNKS):
            chunk_copy(0, c).start()

    @pl.when(f == 0)
    def _():
        for c in range(FFN_X_CHUNKS):
            chunk_copy(i, c).wait()
        for c in range(FFN_X_CHUNKS):
            rs = slice(c * rows, (c + 1) * rows)
            o_ref[rs, :] = xs_ref[rs, :]
            xn_ref[rs, :] = _rms(xs_ref[rs, :], g_ref[...]).astype(BF16)

    for c in range(FFN_X_CHUNKS):
        @pl.when((f == c + 1) & (i + 1 < pl.num_programs(0)))
        def _():
            chunk_copy(i + 1, c).start()

    xn = xn_ref[...]
    a = jnp.dot(xn, wg_ref[...].astype(BF16), preferred_element_type=F32)
    u = jnp.dot(xn, wu_ref[...].astype(BF16), preferred_element_type=F32)
    h = (0.5 * _silu(a) * u).astype(BF16)
    o_ref[...] += jnp.dot(h, wd_ref[...].astype(BF16), preferred_element_type=F32)


def _ffn(x, gains, w_gate, w_up, w_down, li):
    m = x.shape[0]
    assert D_FF // FFN_TF > FFN_X_CHUNKS
    return pl.pallas_call(
        _ffn_body,
        grid=(m // FFN_TM, D_FF // FFN_TF),
        in_specs=[
            pl.BlockSpec(memory_space=pl.ANY),
            pl.BlockSpec((None, 1, D_MODEL), lambda i, f: (li, 0, 0)),
            pl.BlockSpec((None, D_MODEL, FFN_TF), lambda i, f: (li, 0, f)),
            pl.BlockSpec((None, D_MODEL, FFN_TF), lambda i, f: (li, 0, f)),
            pl.BlockSpec((None, FFN_TF, D_MODEL), lambda i, f: (li, f, 0)),
        ],
        out_specs=pl.BlockSpec((FFN_TM, D_MODEL), lambda i, f: (i, 0)),
        out_shape=jax.ShapeDtypeStruct((m, D_MODEL), F32),
        scratch_shapes=[pltpu.VMEM((FFN_TM, D_MODEL), F32),
                        pltpu.VMEM((FFN_TM, D_MODEL), BF16),
                        pltpu.SemaphoreType.DMA((FFN_X_CHUNKS,))],
        compiler_params=_params("arbitrary", "arbitrary"),
        name="ffn",
    )(x, gains, w_gate, w_up, w_down)


def _proj_body(x_ref, g_ref, w_ref, o_ref, xn_ref):
    @pl.when(pl.program_id(1) == 0)
    def _():
        xn_ref[...] = _rms(x_ref[...], g_ref[...]).astype(BF16)

    o_ref[...] = jnp.dot(xn_ref[...], w_ref[...].astype(BF16), preferred_element_type=F32)


def _norm_proj(x, gains, li, w, wi, n, name):
    m = x.shape[0]
    return pl.pallas_call(
        _proj_body,
        grid=(m // PROJ_TM, n // PROJ_TN),
        in_specs=[
            pl.BlockSpec((PROJ_TM, D_MODEL), lambda i, j: (i, 0)),
            pl.BlockSpec((None, 1, D_MODEL), lambda i, j: (li, 0, 0)),
            pl.BlockSpec((None, D_MODEL, PROJ_TN), lambda i, j: (wi, 0, j)),
        ],
        out_specs=pl.BlockSpec((PROJ_TM, PROJ_TN), lambda i, j: (i, j)),
        out_shape=jax.ShapeDtypeStruct((m, n), F32),
        scratch_shapes=[pltpu.VMEM((PROJ_TM, D_MODEL), BF16)],
        compiler_params=_params("parallel", "arbitrary"),
        name=name,
    )(x, gains, w)


def _lb_body(l_ref, o_ref):
    l = l_ref[...]
    e = jnp.exp(l - jnp.max(l, axis=0, keepdims=True))
    p = e / jnp.sum(e, axis=0, keepdims=True)
    gamma0 = p[0:1]
    gamma = gamma0
    o_ref[0:1, :] = gamma - gamma0
    for i in range(1, DEPTH):
        gamma = gamma + p[i:i + 1]
        o_ref[i:i + 1, :] = gamma - gamma0


def _lower_bounds(logits):
    return pl.pallas_call(
        _lb_body,
        out_shape=jax.ShapeDtypeStruct((DEPTH, D_MODEL), F32),
        name="hgrn_lower_bounds",
    )(logits)


def _chunk_step(q, k, v, lf, s_t, state_is_kv=False):
    c, kdim = q.shape
    rows = lax.broadcasted_iota(jnp.int32, (c, kdim), 0)
    b = lf
    d = 1
    while d < c:
        b = b + jnp.where(rows >= d, pltpu.roll(b, d, axis=0), 0.0)
        d *= 2
    b_last = b[c - 1:c, :]

    state_contract = 0 if state_is_kv else 1
    o = lax.dot_general((q * jnp.exp(b)).astype(BF16), s_t.astype(BF16),
                        (((1,), (state_contract,)), ((), ())), preferred_element_type=F32)

    rows_c = lax.broadcasted_iota(jnp.int32, (c, 1), 0)
    for s in range(c):
        w = jnp.exp(b - b[s:s + 1, :]) * (q * k[s:s + 1, :])
        col = jnp.sum(w, axis=-1, keepdims=True)
        col = jnp.where(rows_c >= s, col, 0.0)
        o = o + col * v[s:s + 1, :]

    kd = (k * jnp.exp(b_last - b)).astype(BF16)
    contract_rows = (((0,), (0,)), ((), ()))
    if state_is_kv:
        upd = lax.dot_general(kd, v.astype(BF16), contract_rows, preferred_element_type=F32)
        decay = jnp.broadcast_to(jnp.exp(b_last), (LANES, kdim)).T
        return o, s_t * jnp.tile(decay, (1, v.shape[1] // LANES)) + upd
    upd = lax.dot_general(v.astype(BF16), kd, contract_rows, preferred_element_type=F32)
    return o, s_t * jnp.exp(b_last) + upd


def _hgrn_gates(q_raw, f_raw, lb):
    q = _silu(q_raw) * (HG_FORGET_DIM ** -0.5)
    gate = jnp.log1p(-lb) + _log_sigmoid(f_raw)
    log_lb = jnp.log(lb)
    lf = jnp.maximum(log_lb, gate) + _log1p_exp_neg_abs(log_lb - gate)
    k = (1.0 - lb) * jax.nn.sigmoid(-f_raw)
    return q, k, lf


def _seg_cumsum(x, c):
    r = lax.broadcasted_iota(jnp.int32, x.shape, 0) & (c - 1)
    d = 1
    while d < c:
        x = x + jnp.where(r >= d, pltpu.roll(x, d, axis=0), 0.0)
        d *= 2
    return x


def _rec_prompt_body(kind, hb, kdim, vdim, c, q_ref, k_ref, v_ref, aux_ref, o_ref, st_ref,
                     s_ref, sn_ref, qs_ref, ks_ref, ls_ref):
    t = pl.program_id(2)
    tt = REC_TT
    nseg = tt // c

    @pl.when(t == 0)
    def _():
        s_ref[...] = jnp.zeros_like(s_ref)

    ri = lax.broadcasted_iota(jnp.int32, (tt, tt), 0)
    ci = lax.broadcasted_iota(jnp.int32, (tt, tt), 1)
    shift = c.bit_length() - 1
    mask = ((ri >> shift) == (ci >> shift)) & (ci <= ri)
    segs = [slice(j * c, (j + 1) * c) for j in range(nseg)]
    contract_last = (((1,), (1,)), ((), ()))
    contract_rows = (((0,), (0,)), ((), ()))

    worst = jnp.zeros((tt, kdim), F32)
    for h in range(hb):
        ksl = slice(h * kdim, (h + 1) * kdim)
        vsl = slice(h * vdim, (h + 1) * vdim)
        if kind == "hgrn":
            q, k, lf = _hgrn_gates(q_ref[:, ksl], k_ref[:, ksl], aux_ref[:, ksl])
        else:
            q = q_ref[:, ksl] * (GLA_HEAD_K ** -0.5)
            k = k_ref[:, ksl]
            lf = aux_ref[:, ksl]
        b = _seg_cumsum(lf, c)
        qs_ref[h] = q
        ks_ref[h] = k
        ls_ref[h] = lf
        worst = jnp.maximum(worst, jnp.maximum(-b, jnp.abs(k) * (MXU_EXP_RANGE / MXU_KEY_RANGE)))

        b3 = b.reshape(nseg, c, kdim)
        bl3 = b3[:, c - 1:c, :]
        qt = (q * jnp.exp(b)).astype(BF16)
        kt = (k * jnp.exp(-b)).astype(BF16)
        kd = (k * jnp.exp(bl3 - b3).reshape(tt, kdim)).astype(BF16)
        e = jnp.exp(bl3)
        vb = v_ref[:, vsl].astype(BF16)
        a = lax.dot_general(qt, kt, contract_last, preferred_element_type=F32)
        att = jnp.where(mask, a, 0.0).astype(BF16)
        o = jnp.dot(att, vb, preferred_element_type=F32)
        upd = [lax.dot_general(vb[rs], kd[rs], contract_rows, preferred_element_type=F32) for rs in segs]
        s_t = s_ref[h]
        states = []
        for j in range(nseg):
            states.append(s_t.astype(BF16))
            s_t = s_t * e[j] + upd[j]
        sn_ref[h] = s_t
        for j, rs in enumerate(segs):
            o_ref[rs, vsl] = o[rs] + lax.dot_general(qt[rs], states[j], contract_last,
                                                     preferred_element_type=F32)

    in_range = jnp.max(worst) <= MXU_EXP_RANGE

    @pl.when(jnp.logical_not(in_range))
    def _():
        for h in range(hb):
            vsl = slice(h * vdim, (h + 1) * vdim)
            sn_ref[h] = s_ref[h]

            def sub(j, carry):
                rs = pl.ds(pl.multiple_of(j * REC_C, REC_C), REC_C)
                o, s_new = _chunk_step(qs_ref[h, rs, :], ks_ref[h, rs, :], v_ref[rs, vsl],
                                       ls_ref[h, rs, :], sn_ref[h])
                sn_ref[h] = s_new
                o_ref[rs, vsl] = o
                return carry

            lax.fori_loop(0, tt // REC_C, sub, 0)

    s_ref[...] = sn_ref[...]

    @pl.when(t == pl.num_programs(2) - 1)
    def _():
        for h in range(hb):
            st_ref[h] = sn_ref[h].T


def _rec_prompt(kind, proj, aux, heads, hb, kdim, vdim, c, q_blk, k_blk, v_blk):
    nt = SEQ // REC_TT
    row = lambda b, g, t: b * nt + t
    if kind == "hgrn":
        aux_spec = pl.BlockSpec((1, hb * kdim), lambda b, g, t: (0, g))
    else:
        aux_spec = pl.BlockSpec((REC_TT, hb * kdim), lambda b, g, t: (row(b, g, t), g))
    return pl.pallas_call(
        functools.partial(_rec_prompt_body, kind, hb, kdim, vdim, c),
        grid=(BATCH, heads // hb, nt),
        in_specs=[
            pl.BlockSpec((REC_TT, hb * kdim), lambda b, g, t: (row(b, g, t), q_blk + g)),
            pl.BlockSpec((REC_TT, hb * kdim), lambda b, g, t: (row(b, g, t), k_blk + g)),
            pl.BlockSpec((REC_TT, hb * vdim), lambda b, g, t: (row(b, g, t), v_blk + g)),
            aux_spec,
        ],
        out_specs=[
            pl.BlockSpec((REC_TT, hb * vdim), lambda b, g, t: (row(b, g, t), g)),
            pl.BlockSpec((None, hb, kdim, vdim), lambda b, g, t: (b, g, 0, 0)),
        ],
        out_shape=[
            jax.ShapeDtypeStruct((N_PROMPT, heads * vdim), F32),
            jax.ShapeDtypeStruct((BATCH, heads, kdim, vdim), F32),
        ],
        scratch_shapes=[pltpu.VMEM((hb, vdim, kdim), F32)] * 2
        + [pltpu.VMEM((hb, REC_TT, kdim), F32)] * 3,
        compiler_params=_params("parallel", "parallel", "arbitrary"),
        name=kind + "_rec_prompt",
    )(proj, proj, proj, aux)


def _rec_sample_body(kind, heads, slot, q_ref, k_ref, v_ref, aux_ref, st_ref, *rest):
    o_ref, nst_ref = rest[-2:]

    for other in range(nst_ref.shape[0]):
        if other != slot:
            nst_ref[other] = jnp.zeros(nst_ref.shape[1:], F32)

    def item(i, carry):
        s = i // heads
        h = i % heads
        if kind == "hgrn":
            q, k, lf = _hgrn_gates(q_ref[s, h], k_ref[s, h], aux_ref[h])
        else:
            q = q_ref[s, h] * (GLA_HEAD_K ** -0.5)
            k = k_ref[s, h]
            lf = aux_ref[s, h]
        o, s_new = _chunk_step(q, k, v_ref[s, h], lf, st_ref[s, h], state_is_kv=True)
        o_ref[s, h] = o
        nst_ref[slot, s, h] = s_new
        return carry

    lax.fori_loop(0, SAMPLE_BS * heads, item, 0, unroll=16 if kind == "hgrn" else 4)


def _rec_sample(kind, q, k, v, aux, states, j, new_states, heads, kdim, vdim):
    blk = lambda last2: pl.BlockSpec((SAMPLE_BS, heads) + last2, lambda i: (i, 0, 0, 0))
    st_spec = pl.BlockSpec((None, SAMPLE_BS, heads, kdim, vdim), lambda i: (j, i, 0, 0, 0))
    if kind == "hgrn":
        aux_spec = pl.BlockSpec((heads, 1, kdim), lambda i: (0, 0, 0))
    else:
        aux_spec = blk((DEC_SEQ, kdim))
    in_specs = [blk((DEC_SEQ, kdim)), blk((DEC_SEQ, kdim)), blk((DEC_SEQ, vdim)), aux_spec, st_spec]
    args = [q, k, v, aux, states]
    if new_states is None:
        aliases, slot = {}, j
        new_spec = pl.BlockSpec((states.shape[0], SAMPLE_BS, heads, kdim, vdim), lambda i: (0, i, 0, 0, 0))
    else:
        in_specs.append(pl.BlockSpec(memory_space=pl.ANY))
        args.append(new_states)
        aliases, slot = {len(args) - 1: 1}, 0
        new_spec = pl.BlockSpec((1, SAMPLE_BS, heads, kdim, vdim), lambda i: (j, i, 0, 0, 0))
    return pl.pallas_call(
        functools.partial(_rec_sample_body, kind, heads, slot),
        grid=(DEC_BATCH // SAMPLE_BS,),
        in_specs=in_specs,
        out_specs=[blk((DEC_SEQ, vdim)), new_spec],
        out_shape=[
            jax.ShapeDtypeStruct((DEC_BATCH, heads, DEC_SEQ, vdim), F32),
            jax.ShapeDtypeStruct(states.shape, F32),
        ],
        input_output_aliases=aliases,
        compiler_params=_params("parallel"),
        name=kind + "_rec_sample",
    )(*args)


def _sample_heads(proj, col0, heads, hdim):
    a = proj[N_PROMPT:, col0:col0 + heads * hdim]
    return a.reshape(DEC_BATCH, DEC_SEQ, heads, hdim).transpose(0, 2, 1, 3)


def _unsplit_heads(o_sample):
    return o_sample.transpose(0, 2, 1, 3).reshape(N_SAMPLE, D_MODEL)


def _prompt_tile(i, tile):
    return jnp.minimum(i, N_PROMPT // tile - 1)


def _sample_tile(i, tile):
    return jnp.maximum(i - N_PROMPT // tile, 0)


def _mix_out_body(norm_dim, op_ref, os_ref, gate_ref, gain_ref, x_ref, w_ref, y_ref):
    def run(o_ref):
        half = OUT_TM // 2
        for r0 in (0, half):
            rs = slice(r0, r0 + half)
            parts = []
            for c0 in range(0, D_MODEL, norm_dim):
                sl = slice(c0, c0 + norm_dim)
                on = _rms(o_ref[rs, sl], gain_ref[:, sl])
                parts.append((on * _silu(gate_ref[rs, sl])).astype(BF16))
            lhs = parts[0] if len(parts) == 1 else jnp.concatenate(parts, axis=1)
            y_ref[rs, :] = x_ref[rs, :] + jnp.dot(lhs, w_ref[...], preferred_element_type=F32)

    is_prompt = pl.program_id(0) < N_PROMPT // OUT_TM
    pl.when(is_prompt)(lambda: run(op_ref))
    pl.when(jnp.logical_not(is_prompt))(lambda: run(os_ref))


def _mix_out(o_prompt, o_sample, proj, gate_blk, gain, x, w_out, wi, norm_dim, name):
    m = x.shape[0]
    row = lambda i: (i, 0)
    return pl.pallas_call(
        functools.partial(_mix_out_body, norm_dim),
        grid=(m // OUT_TM,),
        in_specs=[
            pl.BlockSpec((OUT_TM, D_MODEL), lambda i: (_prompt_tile(i, OUT_TM), 0)),
            pl.BlockSpec((OUT_TM, D_MODEL), lambda i: (_sample_tile(i, OUT_TM), 0)),
            pl.BlockSpec((OUT_TM, D_MODEL), lambda i: (i, gate_blk)),
            pl.BlockSpec((1, D_MODEL), lambda i: (0, 0)),
            pl.BlockSpec((OUT_TM, D_MODEL), row),
            pl.BlockSpec((None, D_MODEL, D_MODEL), lambda i: (wi, 0, 0)),
        ],
        out_specs=pl.BlockSpec((OUT_TM, D_MODEL), row),
        out_shape=jax.ShapeDtypeStruct((m, D_MODEL), F32),
        compiler_params=_params("arbitrary"),
        name=name,
    )(o_prompt, o_sample, proj, gain, x, w_out)


def _gla_gate_body(x_ref, g_ref, wl_ref, wu_ref, b_ref, o_ref):
    xn = _rms(x_ref[...], g_ref[...]).astype(BF16)
    low = jnp.dot(xn, wl_ref[...], preferred_element_type=F32)
    z = jnp.dot(low.astype(BF16), wu_ref[...], preferred_element_type=F32) + b_ref[...]
    o_ref[...] = _log_sigmoid(z) / GLA_GATE_NORM


def _gla_gate(x, gains, li, w_low, w_up, bias):
    m = x.shape[0]
    tm = 512
    return pl.pallas_call(
        _gla_gate_body,
        grid=(m // tm,),
        in_specs=[
            pl.BlockSpec((tm, D_MODEL), lambda i: (i, 0)),
            pl.BlockSpec((None, 1, D_MODEL), lambda i: (li, 0, 0)),
            pl.BlockSpec((D_MODEL, LANES), lambda i: (0, 0)),
            pl.BlockSpec((LANES, GLA_DK), lambda i: (0, 0)),
            pl.BlockSpec((1, GLA_DK), lambda i: (0, 0)),
        ],
        out_specs=pl.BlockSpec((tm, GLA_DK), lambda i: (i, 0)),
        out_shape=jax.ShapeDtypeStruct((m, GLA_DK), F32),
        compiler_params=_params("parallel"),
        name="gla_gate",
    )(x, gains, w_low, w_up, bias)


def _pool_body(ns, tt, n_prev, x_ref, g_ref, buf_ref, w_ref, sc_ref, y_ref, nb_ref, xe_ref):
    t = pl.program_id(1)
    pad = POOL_BUF + 1

    @pl.when(t == 0)
    def _():
        xe_ref[:, 0:1, :] = jnp.zeros((ns, 1, D_MODEL), F32)
        xe_ref[:, 1:pad, :] = buf_ref[...]

    @pl.when(t > 0)
    def _():
        xe_ref[:, 0:pad, :] = xe_ref[:, tt:tt + pad, :]

    x = x_ref[...]
    xn = _rms(x, g_ref[...])
    xe_ref[:, pad:pad + tt, :] = xn.reshape(ns, tt, D_MODEL)
    pos = t * tt + lax.broadcasted_iota(jnp.int32, (1, tt, 1), 1)
    for gi, w in enumerate(POOL_WINDOWS):
        sl = slice(gi * POOL_GROUP_DIM, (gi + 1) * POOL_GROUP_DIM)
        cur = xn[:, sl].reshape(ns, tt, POOL_GROUP_DIM)
        win = cur
        for d in range(1, w):
            win = win + xe_ref[:, pad - d:pad - d + tt, sl]
        cnt = jnp.minimum(w, pos + 1 + n_prev).astype(F32)
        y = (win / cnt - cur).reshape(ns * tt, POOL_GROUP_DIM).astype(BF16)
        h = jnp.dot(y, w_ref[gi], preferred_element_type=F32) * sc_ref[:, sl]
        y_ref[:, sl] = x[:, sl] + h

    @pl.when(t == pl.num_programs(1) - 1)
    def _():
        nb_ref[...] = xe_ref[:, tt + 1:tt + pad, :]


def _pool(x, gains, li, buf, w_group, scale, n_seq, seq_len, ns, tt, row0, n_prev):
    nt = seq_len // tt
    assert ns == 1 or nt == 1
    r0 = row0 // (ns * tt)
    row = lambda b, t: (r0 + b * nt + t, 0)
    return pl.pallas_call(
        functools.partial(_pool_body, ns, tt, n_prev),
        grid=(n_seq // ns, nt),
        in_specs=[
            pl.BlockSpec((ns * tt, D_MODEL), row),
            pl.BlockSpec((None, 1, D_MODEL), lambda b, t: (li, 0, 0)),
            pl.BlockSpec((ns, POOL_BUF, D_MODEL), lambda b, t: (b, 0, 0)),
            pl.BlockSpec((len(POOL_WINDOWS), POOL_GROUP_DIM, POOL_GROUP_DIM), lambda b, t: (0, 0, 0)),
            pl.BlockSpec((1, D_MODEL), lambda b, t: (0, 0)),
        ],
        out_specs=[
            pl.BlockSpec((ns * tt, D_MODEL), row),
            pl.BlockSpec((ns, POOL_BUF, D_MODEL), lambda b, t: (b, 0, 0)),
        ],
        out_shape=[
            jax.ShapeDtypeStruct(x.shape, F32),
            jax.ShapeDtypeStruct((n_seq, POOL_BUF, D_MODEL), F32),
        ],
        scratch_shapes=[pltpu.VMEM((ns, tt + POOL_BUF + 1, D_MODEL), F32)],
        input_output_aliases={0: 0},
        compiler_params=_params("parallel", "arbitrary"),
        name="pool_mixer",
    )(x, gains, buf, w_group, scale)


GROUP_TM = 1024


def _join_body(p_ref, s_ref, o_ref):
    is_prompt = pl.program_id(0) < N_PROMPT // GROUP_TM

    @pl.when(is_prompt)
    def _():
        o_ref[...] = p_ref[...]

    @pl.when(jnp.logical_not(is_prompt))
    def _():
        o_ref[...] = s_ref[...]


def _join_groups(x_prompt, x_sample):
    return pl.pallas_call(
        _join_body,
        grid=(N_TOKENS // GROUP_TM,),
        in_specs=[pl.BlockSpec((GROUP_TM, D_MODEL), lambda i: (_prompt_tile(i, GROUP_TM), 0)),
                  pl.BlockSpec((GROUP_TM, D_MODEL), lambda i: (_sample_tile(i, GROUP_TM), 0))],
        out_specs=pl.BlockSpec((GROUP_TM, D_MODEL), lambda i: (i, 0)),
        out_shape=jax.ShapeDtypeStruct((N_TOKENS, D_MODEL), F32),
        compiler_params=_params("arbitrary"),
        name="join_groups",
    )(x_prompt, x_sample)


def _final_norm_body(x_ref, g_ref, p_ref, s_ref):
    is_prompt = pl.program_id(0) < N_PROMPT // GROUP_TM

    @pl.when(is_prompt)
    def _():
        p_ref[...] = _rms(x_ref[...], g_ref[...])

    @pl.when(jnp.logical_not(is_prompt))
    def _():
        s_ref[...] = _rms(x_ref[...], g_ref[...])


def _final_norm(x, gain):
    return pl.pallas_call(
        _final_norm_body,
        grid=(N_TOKENS // GROUP_TM,),
        in_specs=[pl.BlockSpec((GROUP_TM, D_MODEL), lambda i: (i, 0)),
                  pl.BlockSpec((1, D_MODEL), lambda i: (0, 0))],
        out_specs=[pl.BlockSpec((GROUP_TM, D_MODEL), lambda i: (_prompt_tile(i, GROUP_TM), 0)),
                   pl.BlockSpec((GROUP_TM, D_MODEL), lambda i: (_sample_tile(i, GROUP_TM), 0))],
        out_shape=[jax.ShapeDtypeStruct((N_PROMPT, D_MODEL), F32),
                   jax.ShapeDtypeStruct((N_SAMPLE, D_MODEL), F32)],
        compiler_params=_params("arbitrary"),
        name="final_norm",
    )(x, gain)


def kernel(x_prompt, x_sample, state_hgrn, state_gla, state_pool, norm_ffn1, ffn1_w_gate, ffn1_w_up, ffn1_w_down, norm_mix, norm_ffn2, ffn2_w_gate, ffn2_w_up, ffn2_w_down, hgrn_lb_logits, hgrn_w_in, hgrn_o_norm, hgrn_w_out, gla_w_in, gla_w_gate_up, gla_b_gate, gla_o_norm, gla_w_out, pool_w_group, pool_scale, final_norm):
    x = _join_groups(x_prompt.reshape(N_PROMPT, D_MODEL), x_sample.reshape(N_SAMPLE, D_MODEL))
    gains = lambda a: a.reshape(a.shape[0], 1, D_MODEL)
    norm_ffn1, norm_mix, norm_ffn2 = gains(norm_ffn1), gains(norm_mix), gains(norm_ffn2)
    ffn1 = (ffn1_w_gate, ffn1_w_up, ffn1_w_down)
    ffn2 = (ffn2_w_gate, ffn2_w_up, ffn2_w_down)
    hgrn_w_out_b = hgrn_w_out.astype(BF16)
    gla_w_out_b = gla_w_out.astype(BF16)
    lb = _lower_bounds(hgrn_lb_logits)

    new_hgrn_p, new_gla_p, new_pool_p, new_pool_s = [], [], [], []
    new_hgrn_s = new_gla_s = None
    for li in range(DEPTH):
        j = li // N_MIXERS
        kind = li % N_MIXERS
        x = _ffn(x, norm_ffn1, *ffn1, li)
        if kind == 0:
            proj = _norm_proj(x, norm_mix, li, hgrn_w_in, j, 4 * D_MODEL, "hgrn_in_proj")
            lb_l = lb[li:li + 1]
            groups = HG_HEADS // HGRN_HB
            o_p, st_p = _rec_prompt("hgrn", proj, lb_l, HG_HEADS, HGRN_HB, HG_FORGET_DIM, HG_HEAD_V,
                                    HGRN_SEG[j], 0, groups, 2 * groups)
            o_s, new_hgrn_s = _rec_sample(
                "hgrn",
                _sample_heads(proj, 0, HG_HEADS, HG_FORGET_DIM),
                _sample_heads(proj, D_MODEL, HG_HEADS, HG_FORGET_DIM),
                _sample_heads(proj, 2 * D_MODEL, HG_HEADS, HG_HEAD_V),
                lb_l.reshape(HG_HEADS, 1, HG_FORGET_DIM), state_hgrn, j, new_hgrn_s,
                HG_HEADS, HG_FORGET_DIM, HG_HEAD_V)
            new_hgrn_p.append(st_p)
            x = _mix_out(o_p, _unsplit_heads(o_s), proj, 3, hgrn_o_norm[j].reshape(1, D_MODEL), x,
                         hgrn_w_out_b, j, D_MODEL, "hgrn_out")
        elif kind == 1:
            n_main = 2 * GLA_DK + 2 * GLA_DV
            proj = _norm_proj(x, norm_mix, li, gla_w_in, j, n_main, "gla_in_proj")
            w_low = jnp.pad(gla_w_in[j, :, n_main:], ((0, 0), (0, LANES - GLA_GATE_RANK))).astype(BF16)
            w_up = jnp.pad(gla_w_gate_up[j], ((0, LANES - GLA_GATE_RANK), (0, 0))).astype(BF16)
            lf = _gla_gate(x, norm_mix, li, w_low, w_up, gla_b_gate[j].reshape(1, GLA_DK))
            kb = GLA_DK // GLA_HEAD_K
            vb = 2 * GLA_DK // GLA_HEAD_V
            o_p, st_p = _rec_prompt("gla", proj, lf, GLA_HEADS, GLA_HB, GLA_HEAD_K, GLA_HEAD_V,
                                    GLA_SEG, 0, kb, vb)
            o_s, new_gla_s = _rec_sample(
                "gla",
                _sample_heads(proj, 0, GLA_HEADS, GLA_HEAD_K),
                _sample_heads(proj, GLA_DK, GLA_HEADS, GLA_HEAD_K),
                _sample_heads(proj, 2 * GLA_DK, GLA_HEADS, GLA_HEAD_V),
                _sample_heads(lf, 0, GLA_HEADS, GLA_HEAD_K), state_gla, j, new_gla_s,
                GLA_HEADS, GLA_HEAD_K, GLA_HEAD_V)
            new_gla_p.append(st_p)
            x = _mix_out(o_p, _unsplit_heads(o_s), proj, (2 * GLA_DK + GLA_DV) // D_MODEL,
                         gla_o_norm[j].reshape(1, GLA_DV), x, gla_w_out_b, j,
                         GLA_HEAD_V, "gla_out")
        else:
            w_group = pool_w_group[j].astype(BF16)
            scale = pool_scale[j].reshape(1, D_MODEL)
            x, nb_p = _pool(x, norm_mix, li, jnp.zeros((BATCH, POOL_BUF, D_MODEL), F32), w_group, scale,
                            BATCH, SEQ, 1, POOL_TT, 0, 0)
            x, nb_s = _pool(x, norm_mix, li, state_pool[j], w_group, scale,
                            DEC_BATCH, DEC_SEQ, POOL_SAMPLE_NS, DEC_SEQ, N_PROMPT, min(POOL_BUF, PAST_LEN))
            new_pool_p.append(nb_p)
            new_pool_s.append(nb_s)
        x = _ffn(x, norm_ffn2, *ffn2, li)

    y_p, y_s = _final_norm(x, final_norm.reshape(1, D_MODEL))
    return (y_p.reshape(BATCH, SEQ, D_MODEL), y_s.reshape(DEC_BATCH, DEC_SEQ, D_MODEL),
            jnp.stack(new_hgrn_p), jnp.stack(new_gla_p), jnp.stack(new_pool_p),
            new_hgrn_s, new_gla_s, jnp.stack(new_pool_s))
```

```python
import functools

import jax
import jax.numpy as jnp
from jax import lax
from jax.experimental import pallas as pl
from jax.experimental.pallas import tpu as pltpu

F32 = jnp.float32
BF16 = jnp.bfloat16

D_MODEL = 2048
BATCH = 4
SEQ = 2048
DEPTH = 4
DEC_BATCH = 128
DEC_SEQ = 8
PAST_LEN = 16384
N_MIXERS = 3

HG_FORGET_DIM = 128
HG_HEADS = D_MODEL // HG_FORGET_DIM
HG_HEAD_V = D_MODEL // HG_HEADS

GLA_HEADS = 4
GLA_DK = D_MODEL // 2
GLA_DV = D_MODEL
GLA_HEAD_K = GLA_DK // GLA_HEADS
GLA_HEAD_V = GLA_DV // GLA_HEADS
GLA_GATE_RANK = 16
GLA_GATE_NORM = 16.0

POOL_WINDOWS = (2, 4, 8, 16)
POOL_GROUP_DIM = D_MODEL // len(POOL_WINDOWS)
POOL_BUF = max(POOL_WINDOWS) - 1

D_FF = 5632
EPS = 1e-6

N_PROMPT = BATCH * SEQ
N_SAMPLE = DEC_BATCH * DEC_SEQ
N_TOKENS = N_PROMPT + N_SAMPLE

LANES = 128
VMEM_LIMIT = 56 * 1024 * 1024

FFN_TM, FFN_TF = 1024, 256
FFN_X_CHUNKS = 8
PROJ_TM, PROJ_TN = 1024, 1024
OUT_TM = 256
REC_TT = 256
REC_C = 16
HGRN_HB = 8
HGRN_SEG = (32, 128)
GLA_HB, GLA_SEG = 1, 128
MXU_EXP_RANGE = 60.0
MXU_KEY_RANGE = 1e8
SAMPLE_BS = 4
POOL_TT = 256
POOL_SAMPLE_NS = 16


def _params(*sem):
    return pltpu.CompilerParams(dimension_semantics=sem, vmem_limit_bytes=VMEM_LIMIT)


def _rms(xf, g):
    return xf * lax.rsqrt(jnp.mean(xf * xf, axis=-1, keepdims=True) + EPS) * g


def _silu(a):
    return a * jax.nn.sigmoid(a)


def _log1p_exp_neg_abs(a):
    return jnp.log(1.0 + jnp.exp(-jnp.abs(a)))


def _log_sigmoid(a):
    return jnp.minimum(a, 0.0) - _log1p_exp_neg_abs(a)


def _ffn_body(src_tiles, *refs):
    x_srcs = refs[:len(src_tiles)]
    g_ref, wg_ref, wu_ref, wd_ref, o_ref, xs_ref, xn_ref, sems = refs[len(src_tiles):]
    i = pl.program_id(0)
    f = pl.program_id(1)
    rows = FFN_TM // FFN_X_CHUNKS

    def chunk(tile, c, action):
        first = 0
        for src, n in zip(x_srcs, src_tiles):
            def go(src=src, first=first):
                copy = pltpu.make_async_copy(
                    src.at[pl.ds((tile - first) * FFN_TM + c * rows, rows), :],
                    xs_ref.at[pl.ds(c * rows, rows), :], sems.at[c])
                copy.start() if action == "start" else copy.wait()

            pl.when((tile >= first) & (tile < first + n))(go)
            first += n

    @pl.when((i == 0) & (f == 0))
    def _():
        for c in range(FFN_X_CHUNKS):
            chunk(i, c, "start")

    @pl.when(f == 0)
    def _():
        for c in range(FFN_X_CHUNKS):
            chunk(i, c, "wait")
        for c in range(FFN_X_CHUNKS):
            rs = slice(c * rows, (c + 1) * rows)
            x = xs_ref[rs, :]
            o_ref[rs, :] = x
            inv = lax.rsqrt(jnp.mean(x * x, axis=-1, keepdims=True) + EPS)
            xn_ref[rs, :] = (o_ref[rs, :] * inv * g_ref[...]).astype(BF16)

    for c in range(FFN_X_CHUNKS):
        @pl.when(f == c + 1)
        def _():
            chunk(i + 1, c, "start")

    xn = xn_ref[...]
    a = jnp.dot(xn, wg_ref[...].astype(BF16), preferred_element_type=F32)
    u = jnp.dot(xn, wu_ref[...].astype(BF16), preferred_element_type=F32)
    h = (0.5 * _silu(a) * u).astype(BF16)
    o_ref[...] += jnp.dot(h, wd_ref[...].astype(BF16), preferred_element_type=F32)


def _ffn(x_parts, gains, w_gate, w_up, w_down, li):
    src_tiles = tuple(p.shape[0] // FFN_TM for p in x_parts)
    assert all(p.shape[0] % FFN_TM == 0 for p in x_parts)
    assert D_FF // FFN_TF > FFN_X_CHUNKS
    m = sum(p.shape[0] for p in x_parts)
    return pl.pallas_call(
        functools.partial(_ffn_body, src_tiles),
        grid=(m // FFN_TM, D_FF // FFN_TF),
        in_specs=[pl.BlockSpec(memory_space=pl.ANY)] * len(x_parts) + [
            pl.BlockSpec((None, 1, D_MODEL), lambda i, f: (li, 0, 0)),
            pl.BlockSpec((None, D_MODEL, FFN_TF), lambda i, f: (li, 0, f)),
            pl.BlockSpec((None, D_MODEL, FFN_TF), lambda i, f: (li, 0, f)),
            pl.BlockSpec((None, FFN_TF, D_MODEL), lambda i, f: (li, f, 0)),
        ],
        out_specs=pl.BlockSpec((FFN_TM, D_MODEL), lambda i, f: (i, 0)),
        out_shape=jax.ShapeDtypeStruct((m, D_MODEL), F32),
        scratch_shapes=[pltpu.VMEM((FFN_TM, D_MODEL), F32),
                        pltpu.VMEM((FFN_TM, D_MODEL), BF16),
                        pltpu.SemaphoreType.DMA((FFN_X_CHUNKS,))],
        compiler_params=_params("arbitrary", "arbitrary"),
        name="ffn",
    )(*x_parts, gains, w_gate, w_up, w_down)


def _proj_body(x_ref, g_ref, w_ref, o_ref, xn_ref):
    @pl.when(pl.program_id(1) == 0)
    def _():
        xn_ref[...] = _rms(x_ref[...], g_ref[...]).astype(BF16)

    o_ref[...] = jnp.dot(xn_ref[...], w_ref[...].astype(BF16), preferred_element_type=F32)


def _norm_proj(x, gains, li, w, wi, n, name):
    m = x.shape[0]
    return pl.pallas_call(
        _proj_body,
        grid=(m // PROJ_TM, n // PROJ_TN),
        in_specs=[
            pl.BlockSpec((PROJ_TM, D_MODEL), lambda i, j: (i, 0)),
            pl.BlockSpec((None, 1, D_MODEL), lambda i, j: (li, 0, 0)),
            pl.BlockSpec((None, D_MODEL, PROJ_TN), lambda i, j: (wi, 0, j)),
        ],
        out_specs=pl.BlockSpec((PROJ_TM, PROJ_TN), lambda i, j: (i, j)),
        out_shape=jax.ShapeDtypeStruct((m, n), F32),
        scratch_shapes=[pltpu.VMEM((PROJ_TM, D_MODEL), BF16)],
        compiler_params=_params("parallel", "arbitrary"),
        name=name,
    )(x, gains, w)


def _lb_body(l_ref, o_ref):
    l = l_ref[...]
    e = jnp.exp(l - jnp.max(l, axis=0, keepdims=True))
    p = e / jnp.sum(e, axis=0, keepdims=True)
    gamma0 = p[0:1]
    gamma = gamma0
    o_ref[0:1, :] = gamma - gamma0
    for i in range(1, DEPTH):
        gamma = gamma + p[i:i + 1]
        o_ref[i:i + 1, :] = gamma - gamma0


def _lower_bounds(logits):
    return pl.pallas_call(
        _lb_body,
        out_shape=jax.ShapeDtypeStruct((DEPTH, D_MODEL), F32),
        name="hgrn_lower_bounds",
    )(logits)


def _chunk_step(q, k, v, lf, s_t, state_is_kv=False):
    c, kdim = q.shape
    rows = lax.broadcasted_iota(jnp.int32, (c, kdim), 0)
    b = lf
    d = 1
    while d < c:
        b = b + jnp.where(rows >= d, pltpu.roll(b, d, axis=0), 0.0)
        d *= 2
    b_last = b[c - 1:c, :]

    state_contract = 0 if state_is_kv else 1
    o = lax.dot_general((q * jnp.exp(b)).astype(BF16), s_t.astype(BF16),
                        (((1,), (state_contract,)), ((), ())), preferred_element_type=F32)

    rows_c = lax.broadcasted_iota(jnp.int32, (c, 1), 0)
    for s in range(c):
        w = jnp.exp(b - b[s:s + 1, :]) * (q * k[s:s + 1, :])
        col = jnp.sum(w, axis=-1, keepdims=True)
        col = jnp.where(rows_c >= s, col, 0.0)
        o = o + col * v[s:s + 1, :]

    kd = (k * jnp.exp(b_last - b)).astype(BF16)
    contract_rows = (((0,), (0,)), ((), ()))
    if state_is_kv:
        upd = lax.dot_general(kd, v.astype(BF16), contract_rows, preferred_element_type=F32)
        decay = jnp.broadcast_to(jnp.exp(b_last), (LANES, kdim)).T
        return o, s_t * jnp.tile(decay, (1, v.shape[1] // LANES)) + upd
    upd = lax.dot_general(v.astype(BF16), kd, contract_rows, preferred_element_type=F32)
    return o, s_t * jnp.exp(b_last) + upd


def _hgrn_gates(q_raw, f_raw, lb):
    q = _silu(q_raw) * (HG_FORGET_DIM ** -0.5)
    gate = jnp.log1p(-lb) + _log_sigmoid(f_raw)
    log_lb = jnp.log(lb)
    lf = jnp.maximum(log_lb, gate) + _log1p_exp_neg_abs(log_lb - gate)
    k = (1.0 - lb) * jax.nn.sigmoid(-f_raw)
    return q, k, lf


def _seg_cumsum(x, c):
    r = lax.broadcasted_iota(jnp.int32, x.shape, 0) & (c - 1)
    d = 1
    while d < c:
        x = x + jnp.where(r >= d, pltpu.roll(x, d, axis=0), 0.0)
        d *= 2
    return x


def _rec_prompt_body(kind, hb, kdim, vdim, c, q_ref, k_ref, v_ref, aux_ref, o_ref, st_ref,
                     s_ref, sn_ref, qs_ref, ks_ref, ls_ref):
    t = pl.program_id(2)
    tt = REC_TT
    nseg = tt // c

    @pl.when(t == 0)
    def _():
        s_ref[...] = jnp.zeros_like(s_ref)

    ri = lax.broadcasted_iota(jnp.int32, (tt, tt), 0)
    ci = lax.broadcasted_iota(jnp.int32, (tt, tt), 1)
    shift = c.bit_length() - 1
    mask = ((ri >> shift) == (ci >> shift)) & (ci <= ri)
    segs = [slice(j * c, (j + 1) * c) for j in range(nseg)]
    contract_last = (((1,), (1,)), ((), ()))
    contract_rows = (((0,), (0,)), ((), ()))

    worst = jnp.zeros((tt, kdim), F32)
    for h in range(hb):
        ksl = slice(h * kdim, (h + 1) * kdim)
        vsl = slice(h * vdim, (h + 1) * vdim)
        if kind == "hgrn":
            q, k, lf = _hgrn_gates(q_ref[:, ksl], k_ref[:, ksl], aux_ref[:, ksl])
        else:
            q = q_ref[:, ksl] * (GLA_HEAD_K ** -0.5)
            k = k_ref[:, ksl]
            lf = aux_ref[:, ksl]
        b = _seg_cumsum(lf, c)
        qs_ref[h] = q
        ks_ref[h] = k
        ls_ref[h] = lf
        worst = jnp.maximum(worst, -b)
        if kind != "hgrn":
            worst = jnp.maximum(worst, jnp.abs(k) * (MXU_EXP_RANGE / MXU_KEY_RANGE))

        bl3 = b.reshape(nseg, c, kdim)[:, c - 1:c, :]
        e = jnp.exp(bl3)
        qt = (q * jnp.exp(b)).astype(BF16)
        k_grown = k * jnp.exp(-b)
        kt = k_grown.astype(BF16)
        kd = (k_grown.reshape(nseg, c, kdim) * e).reshape(tt, kdim).astype(BF16)
        vb = v_ref[:, vsl].astype(BF16)
        a = lax.dot_general(qt, kt, contract_last, preferred_element_type=F32)
        att = jnp.where(mask, a, 0.0).astype(BF16)
        o = jnp.dot(att, vb, preferred_element_type=F32)
        upd = [lax.dot_general(vb[rs], kd[rs], contract_rows, preferred_element_type=F32) for rs in segs]
        s_t = s_ref[h]
        states = []
        for j in range(nseg):
            states.append(s_t.astype(BF16))
            s_t = s_t * e[j] + upd[j]
        sn_ref[h] = s_t
        for j, rs in enumerate(segs):
            o_ref[rs, vsl] = o[rs] + lax.dot_general(qt[rs], states[j], contract_last,
                                                     preferred_element_type=F32)

    in_range = jnp.max(worst) <= MXU_EXP_RANGE

    @pl.when(jnp.logical_not(in_range))
    def _():
        for h in range(hb):
            vsl = slice(h * vdim, (h + 1) * vdim)
            sn_ref[h] = s_ref[h]

            def sub(j, carry):
                rs = pl.ds(pl.multiple_of(j * REC_C, REC_C), REC_C)
                o, s_new = _chunk_step(qs_ref[h, rs, :], ks_ref[h, rs, :], v_ref[rs, vsl],
                                       ls_ref[h, rs, :], sn_ref[h])
                sn_ref[h] = s_new
                o_ref[rs, vsl] = o
                return carry

            lax.fori_loop(0, tt // REC_C, sub, 0)

    s_ref[...] = sn_ref[...]

    @pl.when(t == pl.num_programs(2) - 1)
    def _():
        for h in range(hb):
            st_ref[h] = sn_ref[h].T


def _rec_prompt(kind, proj, aux, heads, hb, kdim, vdim, c, q_blk, k_blk, v_blk):
    nt = SEQ // REC_TT
    row = lambda b, g, t: b * nt + t
    if kind == "hgrn":
        aux_spec = pl.BlockSpec((1, hb * kdim), lambda b, g, t: (0, g))
    else:
        aux_spec = pl.BlockSpec((REC_TT, hb * kdim), lambda b, g, t: (row(b, g, t), g))
    return pl.pallas_call(
        functools.partial(_rec_prompt_body, kind, hb, kdim, vdim, c),
        grid=(BATCH, heads // hb, nt),
        in_specs=[
            pl.BlockSpec((REC_TT, hb * kdim), lambda b, g, t: (row(b, g, t), q_blk + g)),
            pl.BlockSpec((REC_TT, hb * kdim), lambda b, g, t: (row(b, g, t), k_blk + g)),
            pl.BlockSpec((REC_TT, hb * vdim), lambda b, g, t: (row(b, g, t), v_blk + g)),
            aux_spec,
        ],
        out_specs=[
            pl.BlockSpec((REC_TT, hb * vdim), lambda b, g, t: (row(b, g, t), g)),
            pl.BlockSpec((None, hb, kdim, vdim), lambda b, g, t: (b, g, 0, 0)),
        ],
        out_shape=[
            jax.ShapeDtypeStruct((N_PROMPT, heads * vdim), F32),
            jax.ShapeDtypeStruct((BATCH, heads, kdim, vdim), F32),
        ],
        scratch_shapes=[pltpu.VMEM((hb, vdim, kdim), F32)] * 2
        + [pltpu.VMEM((hb, REC_TT, kdim), F32)] * 3,
        compiler_params=_params("parallel", "parallel", "arbitrary"),
        name=kind + "_rec_prompt",
    )(proj, proj, proj, aux)


def _rec_sample_body(kind, heads, slot, q_ref, k_ref, v_ref, aux_ref, st_ref, *rest):
    o_ref, nst_ref = rest[-2:]

    for other in range(nst_ref.shape[0]):
        if other != slot:
            nst_ref[other] = jnp.zeros(nst_ref.shape[1:], F32)

    def item(i, carry):
        s = i // heads
        h = i % heads
        if kind == "hgrn":
            q, k, lf = _hgrn_gates(q_ref[s, h], k_ref[s, h], aux_ref[h])
        else:
            q = q_ref[s, h] * (GLA_HEAD_K ** -0.5)
            k = k_ref[s, h]
            lf = aux_ref[s, h]
        o, s_new = _chunk_step(q, k, v_ref[s, h], lf, st_ref[s, h], state_is_kv=True)
        o_ref[s, h] = o
        nst_ref[slot, s, h] = s_new
        return carry

    lax.fori_loop(0, SAMPLE_BS * heads, item, 0, unroll=16 if kind == "hgrn" else 4)


def _rec_sample(kind, q, k, v, aux, states, j, new_states, heads, kdim, vdim):
    blk = lambda last2: pl.BlockSpec((SAMPLE_BS, heads) + last2, lambda i: (i, 0, 0, 0))
    st_spec = pl.BlockSpec((None, SAMPLE_BS, heads, kdim, vdim), lambda i: (j, i, 0, 0, 0))
    if kind == "hgrn":
        aux_spec = pl.BlockSpec((heads, 1, kdim), lambda i: (0, 0, 0))
    else:
        aux_spec = blk((DEC_SEQ, kdim))
    in_specs = [blk((DEC_SEQ, kdim)), blk((DEC_SEQ, kdim)), blk((DEC_SEQ, vdim)), aux_spec, st_spec]
    args = [q, k, v, aux, states]
    if new_states is None:
        aliases, slot = {}, j
        new_spec = pl.BlockSpec((states.shape[0], SAMPLE_BS, heads, kdim, vdim), lambda i: (0, i, 0, 0, 0))
    else:
        in_specs.append(pl.BlockSpec(memory_space=pl.ANY))
        args.append(new_states)
        aliases, slot = {len(args) - 1: 1}, 0
        new_spec = pl.BlockSpec((1, SAMPLE_BS, heads, kdim, vdim), lambda i: (j, i, 0, 0, 0))
    return pl.pallas_call(
        functools.partial(_rec_sample_body, kind, heads, slot),
        grid=(DEC_BATCH // SAMPLE_BS,),
        in_specs=in_specs,
        out_specs=[blk((DEC_SEQ, vdim)), new_spec],
        out_shape=[
            jax.ShapeDtypeStruct((DEC_BATCH, heads, DEC_SEQ, vdim), F32),
            jax.ShapeDtypeStruct(states.shape, F32),
        ],
        input_output_aliases=aliases,
        compiler_params=_params("parallel"),
        name=kind + "_rec_sample",
    )(*args)


def _sample_heads(proj, col0, heads, hdim):
    a = proj[N_PROMPT:, col0:col0 + heads * hdim]
    return a.reshape(DEC_BATCH, DEC_SEQ, heads, hdim).transpose(0, 2, 1, 3)


def _unsplit_heads(o_sample):
    return o_sample.transpose(0, 2, 1, 3).reshape(N_SAMPLE, D_MODEL)


def _prompt_tile(i, tile):
    return jnp.minimum(i, N_PROMPT // tile - 1)


def _sample_tile(i, tile):
    return jnp.maximum(i - N_PROMPT // tile, 0)


def _mix_out_body(norm_dim, op_ref, os_ref, gate_ref, gain_ref, x_ref, w_ref, y_ref):
    def run(o_ref):
        half = OUT_TM // 2
        for r0 in (0, half):
            rs = slice(r0, r0 + half)
            parts = []
            for c0 in range(0, D_MODEL, norm_dim):
                sl = slice(c0, c0 + norm_dim)
                on = _rms(o_ref[rs, sl], gain_ref[:, sl])
                parts.append((on * _silu(gate_ref[rs, sl])).astype(BF16))
            lhs = parts[0] if len(parts) == 1 else jnp.concatenate(parts, axis=1)
            y_ref[rs, :] = x_ref[rs, :] + jnp.dot(lhs, w_ref[...], preferred_element_type=F32)

    is_prompt = pl.program_id(0) < N_PROMPT // OUT_TM
    pl.when(is_prompt)(lambda: run(op_ref))
    pl.when(jnp.logical_not(is_prompt))(lambda: run(os_ref))


def _mix_out(o_prompt, o_sample, proj, gate_blk, gain, x, w_out, wi, norm_dim, name):
    m = x.shape[0]
    row = lambda i: (i, 0)
    return pl.pallas_call(
        functools.partial(_mix_out_body, norm_dim),
        grid=(m // OUT_TM,),
        in_specs=[
            pl.BlockSpec((OUT_TM, D_MODEL), lambda i: (_prompt_tile(i, OUT_TM), 0)),
            pl.BlockSpec((OUT_TM, D_MODEL), lambda i: (_sample_tile(i, OUT_TM), 0)),
            pl.BlockSpec((OUT_TM, D_MODEL), lambda i: (i, gate_blk)),
            pl.BlockSpec((1, D_MODEL), lambda i: (0, 0)),
            pl.BlockSpec((OUT_TM, D_MODEL), row),
            pl.BlockSpec((None, D_MODEL, D_MODEL), lambda i: (wi, 0, 0)),
        ],
        out_specs=pl.BlockSpec((OUT_TM, D_MODEL), row),
        out_shape=jax.ShapeDtypeStruct((m, D_MODEL), F32),
        compiler_params=_params("arbitrary"),
        name=name,
    )(o_prompt, o_sample, proj, gain, x, w_out)


def _gla_gate_body(x_ref, g_ref, wl_ref, wu_ref, b_ref, o_ref):
    xn = _rms(x_ref[...], g_ref[...]).astype(BF16)
    low = jnp.dot(xn, wl_ref[...], preferred_element_type=F32)
    z = jnp.dot(low.astype(BF16), wu_ref[...], preferred_element_type=F32) + b_ref[...]
    o_ref[...] = _log_sigmoid(z) / GLA_GATE_NORM


def _gla_gate(x, gains, li, w_low, w_up, bias):
    m = x.shape[0]
    tm = 512
    return pl.pallas_call(
        _gla_gate_body,
        grid=(m // tm,),
        in_specs=[
            pl.BlockSpec((tm, D_MODEL), lambda i: (i, 0)),
            pl.BlockSpec((None, 1, D_MODEL), lambda i: (li, 0, 0)),
            pl.BlockSpec((D_MODEL, LANES), lambda i: (0, 0)),
            pl.BlockSpec((LANES, GLA_DK), lambda i: (0, 0)),
            pl.BlockSpec((1, GLA_DK), lambda i: (0, 0)),
        ],
        out_specs=pl.BlockSpec((tm, GLA_DK), lambda i: (i, 0)),
        out_shape=jax.ShapeDtypeStruct((m, GLA_DK), F32),
        compiler_params=_params("parallel"),
        name="gla_gate",
    )(x, gains, w_low, w_up, bias)


def _pool_body(ns, tt, n_prev, x_ref, g_ref, buf_ref, w_ref, sc_ref, y_ref, nb_ref, xe_ref):
    t = pl.program_id(1)
    pad = POOL_BUF + 1

    @pl.when(t == 0)
    def _():
        xe_ref[:, 0:1, :] = jnp.zeros((ns, 1, D_MODEL), F32)
        xe_ref[:, 1:pad, :] = buf_ref[...]

    @pl.when(t > 0)
    def _():
        xe_ref[:, 0:pad, :] = xe_ref[:, tt:tt + pad, :]

    x = x_ref[...]
    xn = _rms(x, g_ref[...])
    xe_ref[:, pad:pad + tt, :] = xn.reshape(ns, tt, D_MODEL)
    pos = t * tt + lax.broadcasted_iota(jnp.int32, (1, tt, 1), 1)
    for gi, w in enumerate(POOL_WINDOWS):
        sl = slice(gi * POOL_GROUP_DIM, (gi + 1) * POOL_GROUP_DIM)
        cur = xn[:, sl].reshape(ns, tt, POOL_GROUP_DIM)
        win = cur
        for d in range(1, w):
            win = win + xe_ref[:, pad - d:pad - d + tt, sl]
        cnt = jnp.minimum(w, pos + 1 + n_prev).astype(F32)
        y = (win / cnt - cur).reshape(ns * tt, POOL_GROUP_DIM).astype(BF16)
        h = jnp.dot(y, w_ref[gi], preferred_element_type=F32) * sc_ref[:, sl]
        y_ref[:, sl] = x[:, sl] + h

    @pl.when(t == pl.num_programs(1) - 1)
    def _():
        nb_ref[...] = xe_ref[:, tt + 1:tt + pad, :]


def _pool(x, gains, li, buf, w_group, scale, n_seq, seq_len, ns, tt, row0, n_prev):
    nt = seq_len // tt
    assert ns == 1 or nt == 1
    r0 = row0 // (ns * tt)
    row = lambda b, t: (r0 + b * nt + t, 0)
    return pl.pallas_call(
        functools.partial(_pool_body, ns, tt, n_prev),
        grid=(n_seq // ns, nt),
        in_specs=[
            pl.BlockSpec((ns * tt, D_MODEL), row),
            pl.BlockSpec((None, 1, D_MODEL), lambda b, t: (li, 0, 0)),
            pl.BlockSpec((ns, POOL_BUF, D_MODEL), lambda b, t: (b, 0, 0)),
            pl.BlockSpec((len(POOL_WINDOWS), POOL_GROUP_DIM, POOL_GROUP_DIM), lambda b, t: (0, 0, 0)),
            pl.BlockSpec((1, D_MODEL), lambda b, t: (0, 0)),
        ],
        out_specs=[
            pl.BlockSpec((ns * tt, D_MODEL), row),
            pl.BlockSpec((ns, POOL_BUF, D_MODEL), lambda b, t: (b, 0, 0)),
        ],
        out_shape=[
            jax.ShapeDtypeStruct(x.shape, F32),
            jax.ShapeDtypeStruct((n_seq, POOL_BUF, D_MODEL), F32),
        ],
        scratch_shapes=[pltpu.VMEM((ns, tt + POOL_BUF + 1, D_MODEL), F32)],
        input_output_aliases={0: 0},
        compiler_params=_params("parallel", "arbitrary"),
        name="pool_mixer",
    )(x, gains, buf, w_group, scale)


GROUP_TM = 1024


def _final_norm_body(x_ref, g_ref, p_ref, s_ref):
    is_prompt = pl.program_id(0) < N_PROMPT // GROUP_TM

    @pl.when(is_prompt)
    def _():
        p_ref[...] = _rms(x_ref[...], g_ref[...])

    @pl.when(jnp.logical_not(is_prompt))
    def _():
        s_ref[...] = _rms(x_ref[...], g_ref[...])


def _final_norm(x, gain):
    return pl.pallas_call(
        _final_norm_body,
        grid=(N_TOKENS // GROUP_TM,),
        in_specs=[pl.BlockSpec((GROUP_TM, D_MODEL), lambda i: (i, 0)),
                  pl.BlockSpec((1, D_MODEL), lambda i: (0, 0))],
        out_specs=[pl.BlockSpec((GROUP_TM, D_MODEL), lambda i: (_prompt_tile(i, GROUP_TM), 0)),
                   pl.BlockSpec((GROUP_TM, D_MODEL), lambda i: (_sample_tile(i, GROUP_TM), 0))],
        out_shape=[jax.ShapeDtypeStruct((N_PROMPT, D_MODEL), F32),
                   jax.ShapeDtypeStruct((N_SAMPLE, D_MODEL), F32)],
        compiler_params=_params("arbitrary"),
        name="final_norm",
    )(x, gain)


def kernel(x_prompt, x_sample, state_hgrn, state_gla, state_pool, norm_ffn1, ffn1_w_gate, ffn1_w_up, ffn1_w_down, norm_mix, norm_ffn2, ffn2_w_gate, ffn2_w_up, ffn2_w_down, hgrn_lb_logits, hgrn_w_in, hgrn_o_norm, hgrn_w_out, gla_w_in, gla_w_gate_up, gla_b_gate, gla_o_norm, gla_w_out, pool_w_group, pool_scale, final_norm):
    x_parts = (x_prompt.reshape(N_PROMPT, D_MODEL), x_sample.reshape(N_SAMPLE, D_MODEL))
    gains = lambda a: a.reshape(a.shape[0], 1, D_MODEL)
    norm_ffn1, norm_mix, norm_ffn2 = gains(norm_ffn1), gains(norm_mix), gains(norm_ffn2)
    ffn1 = (ffn1_w_gate, ffn1_w_up, ffn1_w_down)
    ffn2 = (ffn2_w_gate, ffn2_w_up, ffn2_w_down)
    hgrn_w_out_b = hgrn_w_out.astype(BF16)
    gla_w_out_b = gla_w_out.astype(BF16)
    lb = _lower_bounds(hgrn_lb_logits)

    new_hgrn_p, new_gla_p, new_pool_p, new_pool_s = [], [], [], []
    new_hgrn_s = new_gla_s = None
    for li in range(DEPTH):
        j = li // N_MIXERS
        kind = li % N_MIXERS
        x = _ffn(x_parts if li == 0 else (x,), norm_ffn1, *ffn1, li)
        if kind == 0:
            proj = _norm_proj(x, norm_mix, li, hgrn_w_in, j, 4 * D_MODEL, "hgrn_in_proj")
            lb_l = lb[li:li + 1]
            groups = HG_HEADS // HGRN_HB
            o_p, st_p = _rec_prompt("hgrn", proj, lb_l, HG_HEADS, HGRN_HB, HG_FORGET_DIM, HG_HEAD_V,
                                    HGRN_SEG[j], 0, groups, 2 * groups)
            o_s, new_hgrn_s = _rec_sample(
                "hgrn",
                _sample_heads(proj, 0, HG_HEADS, HG_FORGET_DIM),
                _sample_heads(proj, D_MODEL, HG_HEADS, HG_FORGET_DIM),
                _sample_heads(proj, 2 * D_MODEL, HG_HEADS, HG_HEAD_V),
                lb_l.reshape(HG_HEADS, 1, HG_FORGET_DIM), state_hgrn, j, new_hgrn_s,
                HG_HEADS, HG_FORGET_DIM, HG_HEAD_V)
            new_hgrn_p.append(st_p)
            x = _mix_out(o_p, _unsplit_heads(o_s), proj, 3, hgrn_o_norm[j].reshape(1, D_MODEL), x,
                         hgrn_w_out_b, j, D_MODEL, "hgrn_out")
        elif kind == 1:
            n_main = 2 * GLA_DK + 2 * GLA_DV
            proj = _norm_proj(x, norm_mix, li, gla_w_in, j, n_main, "gla_in_proj")
            w_low = jnp.pad(gla_w_in[j, :, n_main:], ((0, 0), (0, LANES - GLA_GATE_RANK))).astype(BF16)
            w_up = jnp.pad(gla_w_gate_up[j], ((0, LANES - GLA_GATE_RANK), (0, 0))).astype(BF16)
            lf = _gla_gate(x, norm_mix, li, w_low, w_up, gla_b_gate[j].reshape(1, GLA_DK))
            kb = GLA_DK // GLA_HEAD_K
            vb = 2 * GLA_DK // GLA_HEAD_V
            o_p, st_p = _rec_prompt("gla", proj, lf, GLA_HEADS, GLA_HB, GLA_HEAD_K, GLA_HEAD_V,
                                    GLA_SEG, 0, kb, vb)
            o_s, new_gla_s = _rec_sample(
                "gla",
                _sample_heads(proj, 0, GLA_HEADS, GLA_HEAD_K),
                _sample_heads(proj, GLA_DK, GLA_HEADS, GLA_HEAD_K),
                _sample_heads(proj, 2 * GLA_DK, GLA_HEADS, GLA_HEAD_V),
                _sample_heads(lf, 0, GLA_HEADS, GLA_HEAD_K), state_gla, j, new_gla_s,
                GLA_HEADS, GLA_HEAD_K, GLA_HEAD_V)
            new_gla_p.append(st_p)
            x = _mix_out(o_p, _unsplit_heads(o_s), proj, (2 * GLA_DK + GLA_DV) // D_MODEL,
                         gla_o_norm[j].reshape(1, GLA_DV), x, gla_w_out_b, j,
                         GLA_HEAD_V, "gla_out")
        else:
            w_group = pool_w_group[j].astype(BF16)
            scale = pool_scale[j].reshape(1, D_MODEL)
            x, nb_p = _pool(x, norm_mix, li, jnp.zeros((BATCH, POOL_BUF, D_MODEL), F32), w_group, scale,
                            BATCH, SEQ, 1, POOL_TT, 0, 0)
            x, nb_s = _pool(x, norm_mix, li, state_pool[j], w_group, scale,
                            DEC_BATCH, DEC_SEQ, POOL_SAMPLE_NS, DEC_SEQ, N_PROMPT, min(POOL_BUF, PAST_LEN))
            new_pool_p.append(nb_p)
            new_pool_s.append(nb_s)
        x = _ffn((x,), norm_ffn2, *ffn2, li)

    y_p, y_s = _final_norm(x, final_norm.reshape(1, D_MODEL))
    return (y_p.reshape(BATCH, SEQ, D_MODEL), y_s.reshape(DEC_BATCH, DEC_SEQ, D_MODEL),
            jnp.stack(new_hgrn_p), jnp.stack(new_gla_p), jnp.stack(new_pool_p),
            new_hgrn_s, new_gla_s, jnp.stack(new_pool_s))
```

```python
import functools

import jax
import jax.numpy as jnp
from jax import lax
from jax.experimental import pallas as pl
from jax.experimental.pallas import tpu as pltpu

F32 = jnp.float32
BF16 = jnp.bfloat16

D_MODEL = 2048
BATCH = 4
SEQ = 2048
DEPTH = 4
DEC_BATCH = 128
DEC_SEQ = 8
PAST_LEN = 16384
N_MIXERS = 3

HG_FORGET_DIM = 128
HG_HEADS = D_MODEL // HG_FORGET_DIM
HG_HEAD_V = D_MODEL // HG_HEADS

GLA_HEADS = 4
GLA_DK = D_MODEL // 2
GLA_DV = D_MODEL
GLA_HEAD_K = GLA_DK // GLA_HEADS
GLA_HEAD_V = GLA_DV // GLA_HEADS
GLA_GATE_RANK = 16
GLA_GATE_NORM = 16.0

POOL_WINDOWS = (2, 4, 8, 16)
POOL_GROUP_DIM = D_MODEL // len(POOL_WINDOWS)
POOL_BUF = max(POOL_WINDOWS) - 1

D_FF = 5632
EPS = 1e-6

N_PROMPT = BATCH * SEQ
N_SAMPLE = DEC_BATCH * DEC_SEQ
N_TOKENS = N_PROMPT + N_SAMPLE

LANES = 128
VMEM_LIMIT = 56 * 1024 * 1024
FFN_VMEM_LIMIT = 61 * 1024 * 1024

FFN_TM, FFN_TF = 1024, 512
FFN_SUB = 256
FFN_X_CHUNKS = 8
PROJ_TM, PROJ_TN = 1024, 1024
OUT_TM = 256
REC_TT = 256
REC_C = 16
HGRN_HB = 8
HGRN_SEG = (32, 128)
GLA_HB, GLA_SEG = 1, 128
MXU_EXP_RANGE = 60.0
MXU_KEY_RANGE = 1e8
SAMPLE_BS = 4
POOL_TT = 256
POOL_SAMPLE_NS = 16


def _params(*sem, vmem=VMEM_LIMIT):
    return pltpu.CompilerParams(dimension_semantics=sem, vmem_limit_bytes=vmem)


def _rms(xf, g):
    return xf * lax.rsqrt(jnp.mean(xf * xf, axis=-1, keepdims=True) + EPS) * g


def _silu(a):
    return a * jax.nn.sigmoid(a)


def _log1p_exp_neg_abs(a):
    return jnp.log(1.0 + jnp.exp(-jnp.abs(a)))


def _log_sigmoid(a):
    return jnp.minimum(a, 0.0) - _log1p_exp_neg_abs(a)


def _ffn_body(src_tiles, *refs):
    x_srcs = refs[:len(src_tiles)]
    g_ref, wg_ref, wu_ref, wd_ref, o_ref, xs_ref, xn_ref, sems = refs[len(src_tiles):]
    i = pl.program_id(0)
    f = pl.program_id(1)
    rows = FFN_TM // FFN_X_CHUNKS

    def chunk(tile, c, action):
        first = 0
        for src, n in zip(x_srcs, src_tiles):
            def go(src=src, first=first):
                copy = pltpu.make_async_copy(
                    src.at[pl.ds((tile - first) * FFN_TM + c * rows, rows), :],
                    xs_ref.at[pl.ds(c * rows, rows), :], sems.at[c])
                copy.start() if action == "start" else copy.wait()

            pl.when((tile >= first) & (tile < first + n))(go)
            first += n

    @pl.when((i == 0) & (f == 0))
    def _():
        for c in range(FFN_X_CHUNKS):
            chunk(i, c, "start")

    @pl.when(f == 0)
    def _():
        for c in range(FFN_X_CHUNKS):
            chunk(i, c, "wait")
        for c in range(FFN_X_CHUNKS):
            rs = slice(c * rows, (c + 1) * rows)
            x = xs_ref[rs, :]
            o_ref[rs, :] = x
            inv = lax.rsqrt(jnp.mean(x * x, axis=-1, keepdims=True) + EPS)
            xn_ref[rs, :] = (o_ref[rs, :] * inv * g_ref[...]).astype(BF16)

    for c in range(FFN_X_CHUNKS):
        @pl.when(f == c + 1)
        def _():
            chunk(i + 1, c, "start")

    xn = xn_ref[...]
    for c0 in range(0, FFN_TF, FFN_SUB):
        cs = slice(c0, c0 + FFN_SUB)
        a = jnp.dot(xn, wg_ref[:, cs].astype(BF16), preferred_element_type=F32)
        u = jnp.dot(xn, wu_ref[:, cs].astype(BF16), preferred_element_type=F32)
        h = (0.5 * _silu(a) * u).astype(BF16)
        o_ref[...] += jnp.dot(h, wd_ref[cs, :].astype(BF16), preferred_element_type=F32)


def _ffn(x_parts, gains, w_gate, w_up, w_down, li):
    src_tiles = tuple(p.shape[0] // FFN_TM for p in x_parts)
    assert all(p.shape[0] % FFN_TM == 0 for p in x_parts)
    assert D_FF // FFN_TF > FFN_X_CHUNKS
    m = sum(p.shape[0] for p in x_parts)
    return pl.pallas_call(
        functools.partial(_ffn_body, src_tiles),
        grid=(m // FFN_TM, D_FF // FFN_TF),
        in_specs=[pl.BlockSpec(memory_space=pl.ANY)] * len(x_parts) + [
            pl.BlockSpec((None, 1, D_MODEL), lambda i, f: (li, 0, 0)),
            pl.BlockSpec((None, D_MODEL, FFN_TF), lambda i, f: (li, 0, f)),
            pl.BlockSpec((None, D_MODEL, FFN_TF), lambda i, f: (li, 0, f)),
            pl.BlockSpec((None, FFN_TF, D_MODEL), lambda i, f: (li, f, 0)),
        ],
        out_specs=pl.BlockSpec((FFN_TM, D_MODEL), lambda i, f: (i, 0)),
        out_shape=jax.ShapeDtypeStruct((m, D_MODEL), F32),
        scratch_shapes=[pltpu.VMEM((FFN_TM, D_MODEL), F32),
                        pltpu.VMEM((FFN_TM, D_MODEL), BF16),
                        pltpu.SemaphoreType.DMA((FFN_X_CHUNKS,))],
        compiler_params=_params("arbitrary", "arbitrary", vmem=FFN_VMEM_LIMIT),
        name="ffn",
    )(*x_parts, gains, w_gate, w_up, w_down)


def _proj_body(x_ref, g_ref, w_ref, o_ref, xn_ref):
    @pl.when(pl.program_id(1) == 0)
    def _():
        xn_ref[...] = _rms(x_ref[...], g_ref[...]).astype(BF16)

    o_ref[...] = jnp.dot(xn_ref[...], w_ref[...].astype(BF16), preferred_element_type=F32)


def _norm_proj(x, gains, li, w, wi, n, name):
    m = x.shape[0]
    return pl.pallas_call(
        _proj_body,
        grid=(m // PROJ_TM, n // PROJ_TN),
        in_specs=[
            pl.BlockSpec((PROJ_TM, D_MODEL), lambda i, j: (i, 0)),
            pl.BlockSpec((None, 1, D_MODEL), lambda i, j: (li, 0, 0)),
            pl.BlockSpec((None, D_MODEL, PROJ_TN), lambda i, j: (wi, 0, j)),
        ],
        out_specs=pl.BlockSpec((PROJ_TM, PROJ_TN), lambda i, j: (i, j)),
        out_shape=jax.ShapeDtypeStruct((m, n), F32),
        scratch_shapes=[pltpu.VMEM((PROJ_TM, D_MODEL), BF16)],
        compiler_params=_params("parallel", "arbitrary"),
        name=name,
    )(x, gains, w)


def _lb_body(l_ref, o_ref):
    l = l_ref[...]
    e = jnp.exp(l - jnp.max(l, axis=0, keepdims=True))
    p = e / jnp.sum(e, axis=0, keepdims=True)
    gamma0 = p[0:1]
    gamma = gamma0
    o_ref[0:1, :] = gamma - gamma0
    for i in range(1, DEPTH):
        gamma = gamma + p[i:i + 1]
        o_ref[i:i + 1, :] = gamma - gamma0


def _lower_bounds(logits):
    return pl.pallas_call(
        _lb_body,
        out_shape=jax.ShapeDtypeStruct((DEPTH, D_MODEL), F32),
        name="hgrn_lower_bounds",
    )(logits)


def _chunk_step(q, k, v, lf, s_t, state_is_kv=False):
    c, kdim = q.shape
    rows = lax.broadcasted_iota(jnp.int32, (c, kdim), 0)
    b = lf
    d = 1
    while d < c:
        b = b + jnp.where(rows >= d, pltpu.roll(b, d, axis=0), 0.0)
        d *= 2
    b_last = b[c - 1:c, :]

    state_contract = 0 if state_is_kv else 1
    o = lax.dot_general((q * jnp.exp(b)).astype(BF16), s_t.astype(BF16),
                        (((1,), (state_contract,)), ((), ())), preferred_element_type=F32)

    rows_c = lax.broadcasted_iota(jnp.int32, (c, 1), 0)
    for s in range(c):
        w = jnp.exp(b - b[s:s + 1, :]) * (q * k[s:s + 1, :])
        col = jnp.sum(w, axis=-1, keepdims=True)
        col = jnp.where(rows_c >= s, col, 0.0)
        o = o + col * v[s:s + 1, :]

    kd = (k * jnp.exp(b_last - b)).astype(BF16)
    contract_rows = (((0,), (0,)), ((), ()))
    if state_is_kv:
        upd = lax.dot_general(kd, v.astype(BF16), contract_rows, preferred_element_type=F32)
        decay = jnp.broadcast_to(jnp.exp(b_last), (LANES, kdim)).T
        return o, s_t * jnp.tile(decay, (1, v.shape[1] // LANES)) + upd
    upd = lax.dot_general(v.astype(BF16), kd, contract_rows, preferred_element_type=F32)
    return o, s_t * jnp.exp(b_last) + upd


def _hgrn_gates(q_raw, f_raw, lb):
    q = _silu(q_raw) * (HG_FORGET_DIM ** -0.5)
    gate = jnp.log1p(-lb) + _log_sigmoid(f_raw)
    log_lb = jnp.log(lb)
    lf = jnp.maximum(log_lb, gate) + _log1p_exp_neg_abs(log_lb - gate)
    k = (1.0 - lb) * jax.nn.sigmoid(-f_raw)
    return q, k, lf


def _seg_cumsum(x, c):
    r = lax.broadcasted_iota(jnp.int32, x.shape, 0) & (c - 1)
    d = 1
    while d < c:
        x = x + jnp.where(r >= d, pltpu.roll(x, d, axis=0), 0.0)
        d *= 2
    return x


def _rec_prompt_body(kind, hb, kdim, vdim, c, q_ref, k_ref, v_ref, aux_ref, o_ref, st_ref,
                     s_ref, sn_ref, qs_ref, ks_ref, ls_ref):
    t = pl.program_id(2)
    tt = REC_TT
    nseg = tt // c

    @pl.when(t == 0)
    def _():
        s_ref[...] = jnp.zeros_like(s_ref)

    ri = lax.broadcasted_iota(jnp.int32, (tt, tt), 0)
    ci = lax.broadcasted_iota(jnp.int32, (tt, tt), 1)
    shift = c.bit_length() - 1
    mask = ((ri >> shift) == (ci >> shift)) & (ci <= ri)
    segs = [slice(j * c, (j + 1) * c) for j in range(nseg)]
    contract_last = (((1,), (1,)), ((), ()))
    contract_rows = (((0,), (0,)), ((), ()))

    worst = jnp.zeros((tt, kdim), F32)
    for h in range(hb):
        ksl = slice(h * kdim, (h + 1) * kdim)
        vsl = slice(h * vdim, (h + 1) * vdim)
        if kind == "hgrn":
            q, k, lf = _hgrn_gates(q_ref[:, ksl], k_ref[:, ksl], aux_ref[:, ksl])
        else:
            q = q_ref[:, ksl] * (GLA_HEAD_K ** -0.5)
            k = k_ref[:, ksl]
            lf = aux_ref[:, ksl]
        b = _seg_cumsum(lf, c)
        qs_ref[h] = q
        ks_ref[h] = k
        ls_ref[h] = lf
        worst = jnp.maximum(worst, -b)
        if kind != "hgrn":
            worst = jnp.maximum(worst, jnp.abs(k) * (MXU_EXP_RANGE / MXU_KEY_RANGE))

        bl3 = b.reshape(nseg, c, kdim)[:, c - 1:c, :]
        e = jnp.exp(bl3)
        qt = (q * jnp.exp(b)).astype(BF16)
        k_grown = k * jnp.exp(-b)
        kt = k_grown.astype(BF16)
        kd = (k_grown.reshape(nseg, c, kdim) * e).reshape(tt, kdim).astype(BF16)
        vb = v_ref[:, vsl].astype(BF16)
        a = lax.dot_general(qt, kt, contract_last, preferred_element_type=F32)
        att = jnp.where(mask, a, 0.0).astype(BF16)
        o = jnp.dot(att, vb, preferred_element_type=F32)
        upd = [lax.dot_general(vb[rs], kd[rs], contract_rows, preferred_element_type=F32) for rs in segs]
        s_t = s_ref[h]
        states = []
        for j in range(nseg):
            states.append(s_t.astype(BF16))
            s_t = s_t * e[j] + upd[j]
        sn_ref[h] = s_t
        for j, rs in enumerate(segs):
            o_ref[rs, vsl] = o[rs] + lax.dot_general(qt[rs], states[j], contract_last,
                                                     preferred_element_type=F32)

    in_range = jnp.max(worst) <= MXU_EXP_RANGE

    @pl.when(jnp.logical_not(in_range))
    def _():
        for h in range(hb):
            vsl = slice(h * vdim, (h + 1) * vdim)
            sn_ref[h] = s_ref[h]

            def sub(j, carry):
                rs = pl.ds(pl.multiple_of(j * REC_C, REC_C), REC_C)
                o, s_new = _chunk_step(qs_ref[h, rs, :], ks_ref[h, rs, :], v_ref[rs, vsl],
                                       ls_ref[h, rs, :], sn_ref[h])
                sn_ref[h] = s_new
                o_ref[rs, vsl] = o
                return carry

            lax.fori_loop(0, tt // REC_C, sub, 0)

    s_ref[...] = sn_ref[...]

    @pl.when(t == pl.num_programs(2) - 1)
    def _():
        for h in range(hb):
            st_ref[h] = sn_ref[h].T


def _rec_prompt(kind, proj, aux, heads, hb, kdim, vdim, c, q_blk, k_blk, v_blk):
    nt = SEQ // REC_TT
    row = lambda b, g, t: b * nt + t
    if kind == "hgrn":
        aux_spec = pl.BlockSpec((1, hb * kdim), lambda b, g, t: (0, g))
    else:
        aux_spec = pl.BlockSpec((REC_TT, hb * kdim), lambda b, g, t: (row(b, g, t), g))
    return pl.pallas_call(
        functools.partial(_rec_prompt_body, kind, hb, kdim, vdim, c),
        grid=(BATCH, heads // hb, nt),
        in_specs=[
            pl.BlockSpec((REC_TT, hb * kdim), lambda b, g, t: (row(b, g, t), q_blk + g)),
            pl.BlockSpec((REC_TT, hb * kdim), lambda b, g, t: (row(b, g, t), k_blk + g)),
            pl.BlockSpec((REC_TT, hb * vdim), lambda b, g, t: (row(b, g, t), v_blk + g)),
            aux_spec,
        ],
        out_specs=[
            pl.BlockSpec((REC_TT, hb * vdim), lambda b, g, t: (row(b, g, t), g)),
            pl.BlockSpec((None, hb, kdim, vdim), lambda b, g, t: (b, g, 0, 0)),
        ],
        out_shape=[
            jax.ShapeDtypeStruct((N_PROMPT, heads * vdim), F32),
            jax.ShapeDtypeStruct((BATCH, heads, kdim, vdim), F32),
        ],
        scratch_shapes=[pltpu.VMEM((hb, vdim, kdim), F32)] * 2
        + [pltpu.VMEM((hb, REC_TT, kdim), F32)] * 3,
        compiler_params=_params("parallel", "parallel", "arbitrary"),
        name=kind + "_rec_prompt",
    )(proj, proj, proj, aux)


def _rec_sample_body(kind, heads, slot, q_ref, k_ref, v_ref, aux_ref, st_ref, *rest):
    o_ref, nst_ref = rest[-2:]

    for other in range(nst_ref.shape[0]):
        if other != slot:
            nst_ref[other] = jnp.zeros(nst_ref.shape[1:], F32)

    def item(i, carry):
        s = i // heads
        h = i % heads
        if kind == "hgrn":
            q, k, lf = _hgrn_gates(q_ref[s, h], k_ref[s, h], aux_ref[h])
        else:
            q = q_ref[s, h] * (GLA_HEAD_K ** -0.5)
            k = k_ref[s, h]
            lf = aux_ref[s, h]
        o, s_new = _chunk_step(q, k, v_ref[s, h], lf, st_ref[s, h], state_is_kv=True)
        o_ref[s, h] = o
        nst_ref[slot, s, h] = s_new
        return carry

    lax.fori_loop(0, SAMPLE_BS * heads, item, 0, unroll=16 if kind == "hgrn" else 4)


def _rec_sample(kind, q, k, v, aux, states, j, new_states, heads, kdim, vdim):
    blk = lambda last2: pl.BlockSpec((SAMPLE_BS, heads) + last2, lambda i: (i, 0, 0, 0))
    st_spec = pl.BlockSpec((None, SAMPLE_BS, heads, kdim, vdim), lambda i: (j, i, 0, 0, 0))
    if kind == "hgrn":
        aux_spec = pl.BlockSpec((heads, 1, kdim), lambda i: (0, 0, 0))
    else:
        aux_spec = blk((DEC_SEQ, kdim))
    in_specs = [blk((DEC_SEQ, kdim)), blk((DEC_SEQ, kdim)), blk((DEC_SEQ, vdim)), aux_spec, st_spec]
    args = [q, k, v, aux, states]
    if new_states is None:
        aliases, slot = {}, j
        new_spec = pl.BlockSpec((states.shape[0], SAMPLE_BS, heads, kdim, vdim), lambda i: (0, i, 0, 0, 0))
    else:
        in_specs.append(pl.BlockSpec(memory_space=pl.ANY))
        args.append(new_states)
        aliases, slot = {len(args) - 1: 1}, 0
        new_spec = pl.BlockSpec((1, SAMPLE_BS, heads, kdim, vdim), lambda i: (j, i, 0, 0, 0))
    return pl.pallas_call(
        functools.partial(_rec_sample_body, kind, heads, slot),
        grid=(DEC_BATCH // SAMPLE_BS,),
        in_specs=in_specs,
        out_specs=[blk((DEC_SEQ, vdim)), new_spec],
        out_shape=[
            jax.ShapeDtypeStruct((DEC_BATCH, heads, DEC_SEQ, vdim), F32),
            jax.ShapeDtypeStruct(states.shape, F32),
        ],
        input_output_aliases=aliases,
        compiler_params=_params("parallel"),
        name=kind + "_rec_sample",
    )(*args)


def _sample_heads(proj, col0, heads, hdim):
    a = proj[N_PROMPT:, col0:col0 + heads * hdim]
    return a.reshape(DEC_BATCH, DEC_SEQ, heads, hdim).transpose(0, 2, 1, 3)


def _unsplit_heads(o_sample):
    return o_sample.transpose(0, 2, 1, 3).reshape(N_SAMPLE, D_MODEL)


def _prompt_tile(i, tile):
    return jnp.minimum(i, N_PROMPT // tile - 1)


def _sample_tile(i, tile):
    return jnp.maximum(i - N_PROMPT // tile, 0)


def _mix_out_body(norm_dim, op_ref, os_ref, gate_ref, gain_ref, x_ref, w_ref, y_ref):
    def run(o_ref):
        half = OUT_TM // 2
        for r0 in (0, half):
            rs = slice(r0, r0 + half)
            parts = []
            for c0 in range(0, D_MODEL, norm_dim):
                sl = slice(c0, c0 + norm_dim)
                on = _rms(o_ref[rs, sl], gain_ref[:, sl])
                parts.append((on * _silu(gate_ref[rs, sl])).astype(BF16))
            lhs = parts[0] if len(parts) == 1 else jnp.concatenate(parts, axis=1)
            y_ref[rs, :] = x_ref[rs, :] + jnp.dot(lhs, w_ref[...], preferred_element_type=F32)

    is_prompt = pl.program_id(0) < N_PROMPT // OUT_TM
    pl.when(is_prompt)(lambda: run(op_ref))
    pl.when(jnp.logical_not(is_prompt))(lambda: run(os_ref))


def _mix_out(o_prompt, o_sample, proj, gate_blk, gain, x, w_out, wi, norm_dim, name):
    m = x.shape[0]
    row = lambda i: (i, 0)
    return pl.pallas_call(
        functools.partial(_mix_out_body, norm_dim),
        grid=(m // OUT_TM,),
        in_specs=[
            pl.BlockSpec((OUT_TM, D_MODEL), lambda i: (_prompt_tile(i, OUT_TM), 0)),
            pl.BlockSpec((OUT_TM, D_MODEL), lambda i: (_sample_tile(i, OUT_TM), 0)),
            pl.BlockSpec((OUT_TM, D_MODEL), lambda i: (i, gate_blk)),
            pl.BlockSpec((1, D_MODEL), lambda i: (0, 0)),
            pl.BlockSpec((OUT_TM, D_MODEL), row),
            pl.BlockSpec((None, D_MODEL, D_MODEL), lambda i: (wi, 0, 0)),
        ],
        out_specs=pl.BlockSpec((OUT_TM, D_MODEL), row),
        out_shape=jax.ShapeDtypeStruct((m, D_MODEL), F32),
        compiler_params=_params("arbitrary"),
        name=name,
    )(o_prompt, o_sample, proj, gain, x, w_out)


def _gla_gate_body(x_ref, g_ref, wl_ref, wu_ref, b_ref, o_ref):
    xn = _rms(x_ref[...], g_ref[...]).astype(BF16)
    low = jnp.dot(xn, wl_ref[...], preferred_element_type=F32)
    z = jnp.dot(low.astype(BF16), wu_ref[...], preferred_element_type=F32) + b_ref[...]
    o_ref[...] = _log_sigmoid(z) / GLA_GATE_NORM


def _gla_gate(x, gains, li, w_low, w_up, bias):
    m = x.shape[0]
    tm = 512
    return pl.pallas_call(
        _gla_gate_body,
        grid=(m // tm,),
        in_specs=[
            pl.BlockSpec((tm, D_MODEL), lambda i: (i, 0)),
            pl.BlockSpec((None, 1, D_MODEL), lambda i: (li, 0, 0)),
            pl.BlockSpec((D_MODEL, LANES), lambda i: (0, 0)),
            pl.BlockSpec((LANES, GLA_DK), lambda i: (0, 0)),
            pl.BlockSpec((1, GLA_DK), lambda i: (0, 0)),
        ],
        out_specs=pl.BlockSpec((tm, GLA_DK), lambda i: (i, 0)),
        out_shape=jax.ShapeDtypeStruct((m, GLA_DK), F32),
        compiler_params=_params("parallel"),
        name="gla_gate",
    )(x, gains, w_low, w_up, bias)


def _pool_body(ns, tt, n_prev, x_ref, g_ref, buf_ref, w_ref, sc_ref, y_ref, nb_ref, xe_ref):
    t = pl.program_id(1)
    pad = POOL_BUF + 1

    @pl.when(t == 0)
    def _():
        xe_ref[:, 0:1, :] = jnp.zeros((ns, 1, D_MODEL), F32)
        xe_ref[:, 1:pad, :] = buf_ref[...]

    @pl.when(t > 0)
    def _():
        xe_ref[:, 0:pad, :] = xe_ref[:, tt:tt + pad, :]

    x = x_ref[...]
    xn = _rms(x, g_ref[...])
    xe_ref[:, pad:pad + tt, :] = xn.reshape(ns, tt, D_MODEL)
    pos = t * tt + lax.broadcasted_iota(jnp.int32, (1, tt, 1), 1)
    for gi, w in enumerate(POOL_WINDOWS):
        sl = slice(gi * POOL_GROUP_DIM, (gi + 1) * POOL_GROUP_DIM)
        cur = xn[:, sl].reshape(ns, tt, POOL_GROUP_DIM)
        win = cur
        for d in range(1, w):
            win = win + xe_ref[:, pad - d:pad - d + tt, sl]
        cnt = jnp.minimum(w, pos + 1 + n_prev).astype(F32)
        y = (win / cnt - cur).reshape(ns * tt, POOL_GROUP_DIM).astype(BF16)
        h = jnp.dot(y, w_ref[gi], preferred_element_type=F32) * sc_ref[:, sl]
        y_ref[:, sl] = x[:, sl] + h

    @pl.when(t == pl.num_programs(1) - 1)
    def _():
        nb_ref[...] = xe_ref[:, tt + 1:tt + pad, :]


def _pool(x, gains, li, buf, w_group, scale, n_seq, seq_len, ns, tt, row0, n_prev):
    nt = seq_len // tt
    assert ns == 1 or nt == 1
    r0 = row0 // (ns * tt)
    row = lambda b, t: (r0 + b * nt + t, 0)
    return pl.pallas_call(
        functools.partial(_pool_body, ns, tt, n_prev),
        grid=(n_seq // ns, nt),
        in_specs=[
            pl.BlockSpec((ns * tt, D_MODEL), row),
            pl.BlockSpec((None, 1, D_MODEL), lambda b, t: (li, 0, 0)),
            pl.BlockSpec((ns, POOL_BUF, D_MODEL), lambda b, t: (b, 0, 0)),
            pl.BlockSpec((len(POOL_WINDOWS), POOL_GROUP_DIM, POOL_GROUP_DIM), lambda b, t: (0, 0, 0)),
            pl.BlockSpec((1, D_MODEL), lambda b, t: (0, 0)),
        ],
        out_specs=[
            pl.BlockSpec((ns * tt, D_MODEL), row),
            pl.BlockSpec((ns, POOL_BUF, D_MODEL), lambda b, t: (b, 0, 0)),
        ],
        out_shape=[
            jax.ShapeDtypeStruct(x.shape, F32),
            jax.ShapeDtypeStruct((n_seq, POOL_BUF, D_MODEL), F32),
        ],
        scratch_shapes=[pltpu.VMEM((ns, tt + POOL_BUF + 1, D_MODEL), F32)],
        input_output_aliases={0: 0},
        compiler_params=_params("parallel", "arbitrary"),
        name="pool_mixer",
    )(x, gains, buf, w_group, scale)


GROUP_TM = 1024


def _final_norm_body(x_ref, g_ref, p_ref, s_ref):
    is_prompt = pl.program_id(0) < N_PROMPT // GROUP_TM

    @pl.when(is_prompt)
    def _():
        p_ref[...] = _rms(x_ref[...], g_ref[...])

    @pl.when(jnp.logical_not(is_prompt))
    def _():
        s_ref[...] = _rms(x_ref[...], g_ref[...])


def _final_norm(x, gain):
    return pl.pallas_call(
        _final_norm_body,
        grid=(N_TOKENS // GROUP_TM,),
        in_specs=[pl.BlockSpec((GROUP_TM, D_MODEL), lambda i: (i, 0)),
                  pl.BlockSpec((1, D_MODEL), lambda i: (0, 0))],
        out_specs=[pl.BlockSpec((GROUP_TM, D_MODEL), lambda i: (_prompt_tile(i, GROUP_TM), 0)),
                   pl.BlockSpec((GROUP_TM, D_MODEL), lambda i: (_sample_tile(i, GROUP_TM), 0))],
        out_shape=[jax.ShapeDtypeStruct((N_PROMPT, D_MODEL), F32),
                   jax.ShapeDtypeStruct((N_SAMPLE, D_MODEL), F32)],
        compiler_params=_params("arbitrary"),
        name="final_norm",
    )(x, gain)


def kernel(x_prompt, x_sample, state_hgrn, state_gla, state_pool, norm_ffn1, ffn1_w_gate, ffn1_w_up, ffn1_w_down, norm_mix, norm_ffn2, ffn2_w_gate, ffn2_w_up, ffn2_w_down, hgrn_lb_logits, hgrn_w_in, hgrn_o_norm, hgrn_w_out, gla_w_in, gla_w_gate_up, gla_b_gate, gla_o_norm, gla_w_out, pool_w_group, pool_scale, final_norm):
    x_parts = (x_prompt.reshape(N_PROMPT, D_MODEL), x_sample.reshape(N_SAMPLE, D_MODEL))
    gains = lambda a: a.reshape(a.shape[0], 1, D_MODEL)
    norm_ffn1, norm_mix, norm_ffn2 = gains(norm_ffn1), gains(norm_mix), gains(norm_ffn2)
    ffn1 = (ffn1_w_gate, ffn1_w_up, ffn1_w_down)
    ffn2 = (ffn2_w_gate, ffn2_w_up, ffn2_w_down)
    hgrn_w_out_b = hgrn_w_out.astype(BF16)
    gla_w_out_b = gla_w_out.astype(BF16)
    lb = _lower_bounds(hgrn_lb_logits)

    new_hgrn_p, new_gla_p, new_pool_p, new_pool_s = [], [], [], []
    new_hgrn_s = new_gla_s = None
    for li in range(DEPTH):
        j = li // N_MIXERS
        kind = li % N_MIXERS
        x = _ffn(x_parts if li == 0 else (x,), norm_ffn1, *ffn1, li)
        if kind == 0:
            proj = _norm_proj(x, norm_mix, li, hgrn_w_in, j, 4 * D_MODEL, "hgrn_in_proj")
            lb_l = lb[li:li + 1]
            groups = HG_HEADS // HGRN_HB
            o_p, st_p = _rec_prompt("hgrn", proj, lb_l, HG_HEADS, HGRN_HB, HG_FORGET_DIM, HG_HEAD_V,
                                    HGRN_SEG[j], 0, groups, 2 * groups)
            o_s, new_hgrn_s = _rec_sample(
                "hgrn",
                _sample_heads(proj, 0, HG_HEADS, HG_FORGET_DIM),
                _sample_heads(proj, D_MODEL, HG_HEADS, HG_FORGET_DIM),
                _sample_heads(proj, 2 * D_MODEL, HG_HEADS, HG_HEAD_V),
                lb_l.reshape(HG_HEADS, 1, HG_FORGET_DIM), state_hgrn, j, new_hgrn_s,
                HG_HEADS, HG_FORGET_DIM, HG_HEAD_V)
            new_hgrn_p.append(st_p)
            x = _mix_out(o_p, _unsplit_heads(o_s), proj, 3, hgrn_o_norm[j].reshape(1, D_MODEL), x,
                         hgrn_w_out_b, j, D_MODEL, "hgrn_out")
        elif kind == 1:
            n_main = 2 * GLA_DK + 2 * GLA_DV
            proj = _norm_proj(x, norm_mix, li, gla_w_in, j, n_main, "gla_in_proj")
            w_low = jnp.pad(gla_w_in[j, :, n_main:], ((0, 0), (0, LANES - GLA_GATE_RANK))).astype(BF16)
            w_up = jnp.pad(gla_w_gate_up[j], ((0, LANES - GLA_GATE_RANK), (0, 0))).astype(BF16)
            lf = _gla_gate(x, norm_mix, li, w_low, w_up, gla_b_gate[j].reshape(1, GLA_DK))
            kb = GLA_DK // GLA_HEAD_K
            vb = 2 * GLA_DK // GLA_HEAD_V
            o_p, st_p = _rec_prompt("gla", proj, lf, GLA_HEADS, GLA_HB, GLA_HEAD_K, GLA_HEAD_V,
                                    GLA_SEG, 0, kb, vb)
            o_s, new_gla_s = _rec_sample(
                "gla",
                _sample_heads(proj, 0, GLA_HEADS, GLA_HEAD_K),
                _sample_heads(proj, GLA_DK, GLA_HEADS, GLA_HEAD_K),
                _sample_heads(proj, 2 * GLA_DK, GLA_HEADS, GLA_HEAD_V),
                _sample_heads(lf, 0, GLA_HEADS, GLA_HEAD_K), state_gla, j, new_gla_s,
                GLA_HEADS, GLA_HEAD_K, GLA_HEAD_V)
            new_gla_p.append(st_p)
            x = _mix_out(o_p, _unsplit_heads(o_s), proj, (2 * GLA_DK + GLA_DV) // D_MODEL,
                         gla_o_norm[j].reshape(1, GLA_DV), x, gla_w_out_b, j,
                         GLA_HEAD_V, "gla_out")
        else:
            w_group = pool_w_group[j].astype(BF16)
            scale = pool_scale[j].reshape(1, D_MODEL)
            x, nb_p = _pool(x, norm_mix, li, jnp.zeros((BATCH, POOL_BUF, D_MODEL), F32), w_group, scale,
                            BATCH, SEQ, 1, POOL_TT, 0, 0)
            x, nb_s = _pool(x, norm_mix, li, state_pool[j], w_group, scale,
                            DEC_BATCH, DEC_SEQ, POOL_SAMPLE_NS, DEC_SEQ, N_PROMPT, min(POOL_BUF, PAST_LEN))
            new_pool_p.append(nb_p)
            new_pool_s.append(nb_s)
        x = _ffn((x,), norm_ffn2, *ffn2, li)

    y_p, y_s = _final_norm(x, final_norm.reshape(1, D_MODEL))
    return (y_p.reshape(BATCH, SEQ, D_MODEL), y_s.reshape(DEC_BATCH, DEC_SEQ, D_MODEL),
            jnp.stack(new_hgrn_p), jnp.stack(new_gla_p), jnp.stack(new_pool_p),
            new_hgrn_s, new_gla_s, jnp.stack(new_pool_s))
```

```python
import functools

import jax
import jax.numpy as jnp
from jax import lax
from jax.experimental import pallas as pl
from jax.experimental.pallas import tpu as pltpu

F32 = jnp.float32
BF16 = jnp.bfloat16

D_MODEL = 2048
BATCH = 4
SEQ = 2048
DEPTH = 4
DEC_BATCH = 128
DEC_SEQ = 8
PAST_LEN = 16384
N_MIXERS = 3

HG_FORGET_DIM = 128
HG_HEADS = D_MODEL // HG_FORGET_DIM
HG_HEAD_V = D_MODEL // HG_HEADS

GLA_HEADS = 4
GLA_DK = D_MODEL // 2
GLA_DV = D_MODEL
GLA_HEAD_K = GLA_DK // GLA_HEADS
GLA_HEAD_V = GLA_DV // GLA_HEADS
GLA_GATE_RANK = 16
GLA_GATE_NORM = 16.0

POOL_WINDOWS = (2, 4, 8, 16)
POOL_GROUP_DIM = D_MODEL // len(POOL_WINDOWS)
POOL_BUF = max(POOL_WINDOWS) - 1

D_FF = 5632
EPS = 1e-6

N_PROMPT = BATCH * SEQ
N_SAMPLE = DEC_BATCH * DEC_SEQ
N_TOKENS = N_PROMPT + N_SAMPLE

LANES = 128
VMEM_LIMIT = 56 * 1024 * 1024
FFN_VMEM_LIMIT = 61 * 1024 * 1024

FFN_TM, FFN_TF = 1024, 512
FFN_SUB = 256
FFN_X_CHUNKS = 8
PROJ_TM, PROJ_TN = 1024, 1024
OUT_TM = 256
REC_TT = 256
REC_C = 16
HGRN_HB = 16
HGRN_SEG = (32, 128)
GLA_HB, GLA_SEG = 4, 128
MXU_EXP_RANGE = 60.0
MXU_KEY_RANGE = 1e8
SAMPLE_BS = 4
POOL_TT = 256
POOL_SAMPLE_NS = 16


def _params(*sem, vmem=VMEM_LIMIT):
    return pltpu.CompilerParams(dimension_semantics=sem, vmem_limit_bytes=vmem)


def _rms(xf, g):
    return xf * lax.rsqrt(jnp.mean(xf * xf, axis=-1, keepdims=True) + EPS) * g


def _silu(a):
    return a * jax.nn.sigmoid(a)


def _log1p_exp_neg_abs(a):
    return jnp.log(1.0 + jnp.exp(-jnp.abs(a)))


def _log_sigmoid(a):
    return jnp.minimum(a, 0.0) - _log1p_exp_neg_abs(a)


def _ffn_body(src_tiles, *refs):
    x_srcs = refs[:len(src_tiles)]
    g_ref, wg_ref, wu_ref, wd_ref, o_ref, xs_ref, xn_ref, sems = refs[len(src_tiles):]
    i = pl.program_id(0)
    f = pl.program_id(1)
    rows = FFN_TM // FFN_X_CHUNKS

    def chunk(tile, c, action):
        first = 0
        for src, n in zip(x_srcs, src_tiles):
            def go(src=src, first=first):
                copy = pltpu.make_async_copy(
                    src.at[pl.ds((tile - first) * FFN_TM + c * rows, rows), :],
                    xs_ref.at[pl.ds(c * rows, rows), :], sems.at[c])
                copy.start() if action == "start" else copy.wait()

            pl.when((tile >= first) & (tile < first + n))(go)
            first += n

    @pl.when((i == 0) & (f == 0))
    def _():
        for c in range(FFN_X_CHUNKS):
            chunk(i, c, "start")

    @pl.when(f == 0)
    def _():
        for c in range(FFN_X_CHUNKS):
            chunk(i, c, "wait")
        for c in range(FFN_X_CHUNKS):
            rs = slice(c * rows, (c + 1) * rows)
            x = xs_ref[rs, :]
            o_ref[rs, :] = x
            inv = lax.rsqrt(jnp.mean(x * x, axis=-1, keepdims=True) + EPS)
            xn_ref[rs, :] = (o_ref[rs, :] * inv * g_ref[...]).astype(BF16)

    for c in range(FFN_X_CHUNKS):
        @pl.when(f == c + 1)
        def _():
            chunk(i + 1, c, "start")

    xn = xn_ref[...]
    for c0 in range(0, FFN_TF, FFN_SUB):
        cs = slice(c0, c0 + FFN_SUB)
        a = jnp.dot(xn, wg_ref[:, cs].astype(BF16), preferred_element_type=F32)
        u = jnp.dot(xn, wu_ref[:, cs].astype(BF16), preferred_element_type=F32)
        h = (0.5 * _silu(a) * u).astype(BF16)
        o_ref[...] += jnp.dot(h, wd_ref[cs, :].astype(BF16), preferred_element_type=F32)


def _ffn(x_parts, gains, w_gate, w_up, w_down, li):
    src_tiles = tuple(p.shape[0] // FFN_TM for p in x_parts)
    assert all(p.shape[0] % FFN_TM == 0 for p in x_parts)
    assert D_FF // FFN_TF > FFN_X_CHUNKS
    m = sum(p.shape[0] for p in x_parts)
    return pl.pallas_call(
        functools.partial(_ffn_body, src_tiles),
        grid=(m // FFN_TM, D_FF // FFN_TF),
        in_specs=[pl.BlockSpec(memory_space=pl.ANY)] * len(x_parts) + [
            pl.BlockSpec((None, 1, D_MODEL), lambda i, f: (li, 0, 0)),
            pl.BlockSpec((None, D_MODEL, FFN_TF), lambda i, f: (li, 0, f)),
            pl.BlockSpec((None, D_MODEL, FFN_TF), lambda i, f: (li, 0, f)),
            pl.BlockSpec((None, FFN_TF, D_MODEL), lambda i, f: (li, f, 0)),
        ],
        out_specs=pl.BlockSpec((FFN_TM, D_MODEL), lambda i, f: (i, 0)),
        out_shape=jax.ShapeDtypeStruct((m, D_MODEL), F32),
        scratch_shapes=[pltpu.VMEM((FFN_TM, D_MODEL), F32),
                        pltpu.VMEM((FFN_TM, D_MODEL), BF16),
                        pltpu.SemaphoreType.DMA((FFN_X_CHUNKS,))],
        compiler_params=_params("arbitrary", "arbitrary", vmem=FFN_VMEM_LIMIT),
        name="ffn",
    )(*x_parts, gains, w_gate, w_up, w_down)


def _proj_body(x_ref, g_ref, w_ref, o_ref, xn_ref):
    @pl.when(pl.program_id(1) == 0)
    def _():
        xn_ref[...] = _rms(x_ref[...], g_ref[...]).astype(BF16)

    o_ref[...] = jnp.dot(xn_ref[...], w_ref[...].astype(BF16), preferred_element_type=F32)


def _norm_proj(x, gains, li, w, wi, n, name):
    m = x.shape[0]
    return pl.pallas_call(
        _proj_body,
        grid=(m // PROJ_TM, n // PROJ_TN),
        in_specs=[
            pl.BlockSpec((PROJ_TM, D_MODEL), lambda i, j: (i, 0)),
            pl.BlockSpec((None, 1, D_MODEL), lambda i, j: (li, 0, 0)),
            pl.BlockSpec((None, D_MODEL, PROJ_TN), lambda i, j: (wi, 0, j)),
        ],
        out_specs=pl.BlockSpec((PROJ_TM, PROJ_TN), lambda i, j: (i, j)),
        out_shape=jax.ShapeDtypeStruct((m, n), F32),
        scratch_shapes=[pltpu.VMEM((PROJ_TM, D_MODEL), BF16)],
        compiler_params=_params("parallel", "arbitrary"),
        name=name,
    )(x, gains, w)


def _lb_body(l_ref, o_ref):
    l = l_ref[...]
    e = jnp.exp(l - jnp.max(l, axis=0, keepdims=True))
    p = e / jnp.sum(e, axis=0, keepdims=True)
    gamma0 = p[0:1]
    gamma = gamma0
    o_ref[0:1, :] = gamma - gamma0
    for i in range(1, DEPTH):
        gamma = gamma + p[i:i + 1]
        o_ref[i:i + 1, :] = gamma - gamma0


def _lower_bounds(logits):
    return pl.pallas_call(
        _lb_body,
        out_shape=jax.ShapeDtypeStruct((DEPTH, D_MODEL), F32),
        name="hgrn_lower_bounds",
    )(logits)


def _chunk_step(q, k, v, lf, s_t, state_is_kv=False):
    c, kdim = q.shape
    rows = lax.broadcasted_iota(jnp.int32, (c, kdim), 0)
    b = lf
    d = 1
    while d < c:
        b = b + jnp.where(rows >= d, pltpu.roll(b, d, axis=0), 0.0)
        d *= 2
    b_last = b[c - 1:c, :]

    state_contract = 0 if state_is_kv else 1
    o = lax.dot_general((q * jnp.exp(b)).astype(BF16), s_t.astype(BF16),
                        (((1,), (state_contract,)), ((), ())), preferred_element_type=F32)

    rows_c = lax.broadcasted_iota(jnp.int32, (c, 1), 0)
    for s in range(c):
        w = jnp.exp(b - b[s:s + 1, :]) * (q * k[s:s + 1, :])
        col = jnp.sum(w, axis=-1, keepdims=True)
        col = jnp.where(rows_c >= s, col, 0.0)
        o = o + col * v[s:s + 1, :]

    kd = (k * jnp.exp(b_last - b)).astype(BF16)
    contract_rows = (((0,), (0,)), ((), ()))
    if state_is_kv:
        upd = lax.dot_general(kd, v.astype(BF16), contract_rows, preferred_element_type=F32)
        decay = jnp.broadcast_to(jnp.exp(b_last), (LANES, kdim)).T
        return o, s_t * jnp.tile(decay, (1, v.shape[1] // LANES)) + upd
    upd = lax.dot_general(v.astype(BF16), kd, contract_rows, preferred_element_type=F32)
    return o, s_t * jnp.exp(b_last) + upd


def _hgrn_gates(q_raw, f_raw, lb):
    q = _silu(q_raw) * (HG_FORGET_DIM ** -0.5)
    gate = jnp.log1p(-lb) + _log_sigmoid(f_raw)
    log_lb = jnp.log(lb)
    lf = jnp.maximum(log_lb, gate) + _log1p_exp_neg_abs(log_lb - gate)
    k = (1.0 - lb) * jax.nn.sigmoid(-f_raw)
    return q, k, lf


def _seg_cumsum(x, c):
    r = lax.broadcasted_iota(jnp.int32, x.shape, 0) & (c - 1)
    d = 1
    while d < c:
        x = x + jnp.where(r >= d, pltpu.roll(x, d, axis=0), 0.0)
        d *= 2
    return x


def _rec_prompt_body(kind, hb, kdim, vdim, c, q_ref, k_ref, v_ref, aux_ref, o_ref, st_ref,
                     s_ref, sn_ref, qs_ref, ks_ref, ls_ref):
    t = pl.program_id(2)
    tt = REC_TT
    nseg = tt // c

    @pl.when(t == 0)
    def _():
        s_ref[...] = jnp.zeros_like(s_ref)

    ri = lax.broadcasted_iota(jnp.int32, (tt, tt), 0)
    ci = lax.broadcasted_iota(jnp.int32, (tt, tt), 1)
    shift = c.bit_length() - 1
    mask = ((ri >> shift) == (ci >> shift)) & (ci <= ri)
    segs = [slice(j * c, (j + 1) * c) for j in range(nseg)]
    contract_last = (((1,), (1,)), ((), ()))
    contract_rows = (((0,), (0,)), ((), ()))

    worst = jnp.zeros((tt, kdim), F32)
    for h in range(hb):
        ksl = slice(h * kdim, (h + 1) * kdim)
        vsl = slice(h * vdim, (h + 1) * vdim)
        if kind == "hgrn":
            q, k, lf = _hgrn_gates(q_ref[:, ksl], k_ref[:, ksl], aux_ref[:, ksl])
        else:
            q = q_ref[:, ksl] * (GLA_HEAD_K ** -0.5)
            k = k_ref[:, ksl]
            lf = aux_ref[:, ksl]
        b = _seg_cumsum(lf, c)
        qs_ref[h] = q
        ks_ref[h] = k
        ls_ref[h] = lf
        worst = jnp.maximum(worst, -b)
        if kind != "hgrn":
            worst = jnp.maximum(worst, jnp.abs(k) * (MXU_EXP_RANGE / MXU_KEY_RANGE))

        bl3 = b.reshape(nseg, c, kdim)[:, c - 1:c, :]
        e = jnp.exp(bl3)
        qt = (q * jnp.exp(b)).astype(BF16)
        k_grown = k * jnp.exp(-b)
        kt = k_grown.astype(BF16)
        kd = (k_grown.reshape(nseg, c, kdim) * e).reshape(tt, kdim).astype(BF16)
        vb = v_ref[:, vsl].astype(BF16)
        a = lax.dot_general(qt, kt, contract_last, preferred_element_type=F32)
        att = jnp.where(mask, a, 0.0).astype(BF16)
        o = jnp.dot(att, vb, preferred_element_type=F32)
        upd = [lax.dot_general(vb[rs], kd[rs], contract_rows, preferred_element_type=F32) for rs in segs]
        s_t = s_ref[h]
        states = []
        for j in range(nseg):
            states.append(s_t.astype(BF16))
            s_t = s_t * e[j] + upd[j]
        sn_ref[h] = s_t
        for j, rs in enumerate(segs):
            o_ref[rs, vsl] = o[rs] + lax.dot_general(qt[rs], states[j], contract_last,
                                                     preferred_element_type=F32)

    in_range = jnp.max(worst) <= MXU_EXP_RANGE

    @pl.when(jnp.logical_not(in_range))
    def _():
        for h in range(hb):
            vsl = slice(h * vdim, (h + 1) * vdim)
            sn_ref[h] = s_ref[h]

            def sub(j, carry):
                rs = pl.ds(pl.multiple_of(j * REC_C, REC_C), REC_C)
                o, s_new = _chunk_step(qs_ref[h, rs, :], ks_ref[h, rs, :], v_ref[rs, vsl],
                                       ls_ref[h, rs, :], sn_ref[h])
                sn_ref[h] = s_new
                o_ref[rs, vsl] = o
                return carry

            lax.fori_loop(0, tt // REC_C, sub, 0)

    s_ref[...] = sn_ref[...]

    @pl.when(t == pl.num_programs(2) - 1)
    def _():
        for h in range(hb):
            st_ref[h] = sn_ref[h].T


def _rec_prompt(kind, proj, aux, heads, hb, kdim, vdim, c, q_blk, k_blk, v_blk):
    nt = SEQ // REC_TT
    row = lambda b, g, t: b * nt + t
    if kind == "hgrn":
        aux_spec = pl.BlockSpec((1, hb * kdim), lambda b, g, t: (0, g))
    else:
        aux_spec = pl.BlockSpec((REC_TT, hb * kdim), lambda b, g, t: (row(b, g, t), g))
    return pl.pallas_call(
        functools.partial(_rec_prompt_body, kind, hb, kdim, vdim, c),
        grid=(BATCH, heads // hb, nt),
        in_specs=[
            pl.BlockSpec((REC_TT, hb * kdim), lambda b, g, t: (row(b, g, t), q_blk + g)),
            pl.BlockSpec((REC_TT, hb * kdim), lambda b, g, t: (row(b, g, t), k_blk + g)),
            pl.BlockSpec((REC_TT, hb * vdim), lambda b, g, t: (row(b, g, t), v_blk + g)),
            aux_spec,
        ],
        out_specs=[
            pl.BlockSpec((REC_TT, hb * vdim), lambda b, g, t: (row(b, g, t), g)),
            pl.BlockSpec((None, hb, kdim, vdim), lambda b, g, t: (b, g, 0, 0)),
        ],
        out_shape=[
            jax.ShapeDtypeStruct((N_PROMPT, heads * vdim), F32),
            jax.ShapeDtypeStruct((BATCH, heads, kdim, vdim), F32),
        ],
        scratch_shapes=[pltpu.VMEM((hb, vdim, kdim), F32)] * 2
        + [pltpu.VMEM((hb, REC_TT, kdim), F32)] * 3,
        compiler_params=_params("parallel", "parallel", "arbitrary"),
        name=kind + "_rec_prompt",
    )(proj, proj, proj, aux)


def _rec_sample_body(kind, heads, slot, q_ref, k_ref, v_ref, aux_ref, st_ref, *rest):
    o_ref, nst_ref = rest[-2:]

    for other in range(nst_ref.shape[0]):
        if other != slot:
            nst_ref[other] = jnp.zeros(nst_ref.shape[1:], F32)

    def item(i, carry):
        s = i // heads
        h = i % heads
        if kind == "hgrn":
            q, k, lf = _hgrn_gates(q_ref[s, h], k_ref[s, h], aux_ref[h])
        else:
            q = q_ref[s, h] * (GLA_HEAD_K ** -0.5)
            k = k_ref[s, h]
            lf = aux_ref[s, h]
        o, s_new = _chunk_step(q, k, v_ref[s, h], lf, st_ref[s, h], state_is_kv=True)
        o_ref[s, h] = o
        nst_ref[slot, s, h] = s_new
        return carry

    lax.fori_loop(0, SAMPLE_BS * heads, item, 0, unroll=16 if kind == "hgrn" else 4)


def _rec_sample(kind, q, k, v, aux, states, j, new_states, heads, kdim, vdim):
    blk = lambda last2: pl.BlockSpec((SAMPLE_BS, heads) + last2, lambda i: (i, 0, 0, 0))
    st_spec = pl.BlockSpec((None, SAMPLE_BS, heads, kdim, vdim), lambda i: (j, i, 0, 0, 0))
    if kind == "hgrn":
        aux_spec = pl.BlockSpec((heads, 1, kdim), lambda i: (0, 0, 0))
    else:
        aux_spec = blk((DEC_SEQ, kdim))
    in_specs = [blk((DEC_SEQ, kdim)), blk((DEC_SEQ, kdim)), blk((DEC_SEQ, vdim)), aux_spec, st_spec]
    args = [q, k, v, aux, states]
    if new_states is None:
        aliases, slot = {}, j
        new_spec = pl.BlockSpec((states.shape[0], SAMPLE_BS, heads, kdim, vdim), lambda i: (0, i, 0, 0, 0))
    else:
        in_specs.append(pl.BlockSpec(memory_space=pl.ANY))
        args.append(new_states)
        aliases, slot = {len(args) - 1: 1}, 0
        new_spec = pl.BlockSpec((1, SAMPLE_BS, heads, kdim, vdim), lambda i: (j, i, 0, 0, 0))
    return pl.pallas_call(
        functools.partial(_rec_sample_body, kind, heads, slot),
        grid=(DEC_BATCH // SAMPLE_BS,),
        in_specs=in_specs,
        out_specs=[blk((DEC_SEQ, vdim)), new_spec],
        out_shape=[
            jax.ShapeDtypeStruct((DEC_BATCH, heads, DEC_SEQ, vdim), F32),
            jax.ShapeDtypeStruct(states.shape, F32),
        ],
        input_output_aliases=aliases,
        compiler_params=_params("parallel"),
        name=kind + "_rec_sample",
    )(*args)


def _sample_heads(proj, col0, heads, hdim):
    a = proj[N_PROMPT:, col0:col0 + heads * hdim]
    return a.reshape(DEC_BATCH, DEC_SEQ, heads, hdim).transpose(0, 2, 1, 3)


def _unsplit_heads(o_sample):
    return o_sample.transpose(0, 2, 1, 3).reshape(N_SAMPLE, D_MODEL)


def _prompt_tile(i, tile):
    return jnp.minimum(i, N_PROMPT // tile - 1)


def _sample_tile(i, tile):
    return jnp.maximum(i - N_PROMPT // tile, 0)


def _mix_out_body(norm_dim, op_ref, os_ref, gate_ref, gain_ref, x_ref, w_ref, y_ref):
    def run(o_ref):
        half = OUT_TM // 2
        for r0 in (0, half):
            rs = slice(r0, r0 + half)
            parts = []
            for c0 in range(0, D_MODEL, norm_dim):
                sl = slice(c0, c0 + norm_dim)
                on = _rms(o_ref[rs, sl], gain_ref[:, sl])
                parts.append((on * _silu(gate_ref[rs, sl])).astype(BF16))
            lhs = parts[0] if len(parts) == 1 else jnp.concatenate(parts, axis=1)
            y_ref[rs, :] = x_ref[rs, :] + jnp.dot(lhs, w_ref[...], preferred_element_type=F32)

    is_prompt = pl.program_id(0) < N_PROMPT // OUT_TM
    pl.when(is_prompt)(lambda: run(op_ref))
    pl.when(jnp.logical_not(is_prompt))(lambda: run(os_ref))


def _mix_out(o_prompt, o_sample, proj, gate_blk, gain, x, w_out, wi, norm_dim, name):
    m = x.shape[0]
    row = lambda i: (i, 0)
    return pl.pallas_call(
        functools.partial(_mix_out_body, norm_dim),
        grid=(m // OUT_TM,),
        in_specs=[
            pl.BlockSpec((OUT_TM, D_MODEL), lambda i: (_prompt_tile(i, OUT_TM), 0)),
            pl.BlockSpec((OUT_TM, D_MODEL), lambda i: (_sample_tile(i, OUT_TM), 0)),
            pl.BlockSpec((OUT_TM, D_MODEL), lambda i: (i, gate_blk)),
            pl.BlockSpec((1, D_MODEL), lambda i: (0, 0)),
            pl.BlockSpec((OUT_TM, D_MODEL), row),
            pl.BlockSpec((None, D_MODEL, D_MODEL), lambda i: (wi, 0, 0)),
        ],
        out_specs=pl.BlockSpec((OUT_TM, D_MODEL), row),
        out_shape=jax.ShapeDtypeStruct((m, D_MODEL), F32),
        compiler_params=_params("arbitrary"),
        name=name,
    )(o_prompt, o_sample, proj, gain, x, w_out)


def _gla_gate_body(x_ref, g_ref, wl_ref, wu_ref, b_ref, o_ref):
    xn = _rms(x_ref[...], g_ref[...]).astype(BF16)
    low = jnp.dot(xn, wl_ref[...], preferred_element_type=F32)
    z = jnp.dot(low.astype(BF16), wu_ref[...], preferred_element_type=F32) + b_ref[...]
    o_ref[...] = _log_sigmoid(z) / GLA_GATE_NORM


def _gla_gate(x, gains, li, w_low, w_up, bias):
    m = x.shape[0]
    tm = 512
    return pl.pallas_call(
        _gla_gate_body,
        grid=(m // tm,),
        in_specs=[
            pl.BlockSpec((tm, D_MODEL), lambda i: (i, 0)),
            pl.BlockSpec((None, 1, D_MODEL), lambda i: (li, 0, 0)),
            pl.BlockSpec((D_MODEL, LANES), lambda i: (0, 0)),
            pl.BlockSpec((LANES, GLA_DK), lambda i: (0, 0)),
            pl.BlockSpec((1, GLA_DK), lambda i: (0, 0)),
        ],
        out_specs=pl.BlockSpec((tm, GLA_DK), lambda i: (i, 0)),
        out_shape=jax.ShapeDtypeStruct((m, GLA_DK), F32),
        compiler_params=_params("parallel"),
        name="gla_gate",
    )(x, gains, w_low, w_up, bias)


def _pool_body(ns, tt, n_prev, x_ref, g_ref, buf_ref, w_ref, sc_ref, y_ref, nb_ref, xe_ref):
    t = pl.program_id(1)
    pad = POOL_BUF + 1

    @pl.when(t == 0)
    def _():
        xe_ref[:, 0:1, :] = jnp.zeros((ns, 1, D_MODEL), F32)
        xe_ref[:, 1:pad, :] = buf_ref[...]

    @pl.when(t > 0)
    def _():
        xe_ref[:, 0:pad, :] = xe_ref[:, tt:tt + pad, :]

    x = x_ref[...]
    xn = _rms(x, g_ref[...])
    xe_ref[:, pad:pad + tt, :] = xn.reshape(ns, tt, D_MODEL)
    pos = t * tt + lax.broadcasted_iota(jnp.int32, (1, tt, 1), 1)
    for gi, w in enumerate(POOL_WINDOWS):
        sl = slice(gi * POOL_GROUP_DIM, (gi + 1) * POOL_GROUP_DIM)
        cur = xn[:, sl].reshape(ns, tt, POOL_GROUP_DIM)
        win = cur
        for d in range(1, w):
            win = win + xe_ref[:, pad - d:pad - d + tt, sl]
        cnt = jnp.minimum(w, pos + 1 + n_prev).astype(F32)
        y = (win / cnt - cur).reshape(ns * tt, POOL_GROUP_DIM).astype(BF16)
        h = jnp.dot(y, w_ref[gi], preferred_element_type=F32) * sc_ref[:, sl]
        y_ref[:, sl] = x[:, sl] + h

    @pl.when(t == pl.num_programs(1) - 1)
    def _():
        nb_ref[...] = xe_ref[:, tt + 1:tt + pad, :]


def _pool(x, gains, li, buf, w_group, scale, n_seq, seq_len, ns, tt, row0, n_prev):
    nt = seq_len // tt
    assert ns == 1 or nt == 1
    r0 = row0 // (ns * tt)
    row = lambda b, t: (r0 + b * nt + t, 0)
    return pl.pallas_call(
        functools.partial(_pool_body, ns, tt, n_prev),
        grid=(n_seq // ns, nt),
        in_specs=[
            pl.BlockSpec((ns * tt, D_MODEL), row),
            pl.BlockSpec((None, 1, D_MODEL), lambda b, t: (li, 0, 0)),
            pl.BlockSpec((ns, POOL_BUF, D_MODEL), lambda b, t: (b, 0, 0)),
            pl.BlockSpec((len(POOL_WINDOWS), POOL_GROUP_DIM, POOL_GROUP_DIM), lambda b, t: (0, 0, 0)),
            pl.BlockSpec((1, D_MODEL), lambda b, t: (0, 0)),
        ],
        out_specs=[
            pl.BlockSpec((ns * tt, D_MODEL), row),
            pl.BlockSpec((ns, POOL_BUF, D_MODEL), lambda b, t: (b, 0, 0)),
        ],
        out_shape=[
            jax.ShapeDtypeStruct(x.shape, F32),
            jax.ShapeDtypeStruct((n_seq, POOL_BUF, D_MODEL), F32),
        ],
        scratch_shapes=[pltpu.VMEM((ns, tt + POOL_BUF + 1, D_MODEL), F32)],
        input_output_aliases={0: 0},
        compiler_params=_params("parallel", "arbitrary"),
        name="pool_mixer",
    )(x, gains, buf, w_group, scale)


GROUP_TM = 1024


def _final_norm_body(x_ref, g_ref, p_ref, s_ref):
    is_prompt = pl.program_id(0) < N_PROMPT // GROUP_TM

    @pl.when(is_prompt)
    def _():
        p_ref[...] = _rms(x_ref[...], g_ref[...])

    @pl.when(jnp.logical_not(is_prompt))
    def _():
        s_ref[...] = _rms(x_ref[...], g_ref[...])


def _final_norm(x, gain):
    return pl.pallas_call(
        _final_norm_body,
        grid=(N_TOKENS // GROUP_TM,),
        in_specs=[pl.BlockSpec((GROUP_TM, D_MODEL), lambda i: (i, 0)),
                  pl.BlockSpec((1, D_MODEL), lambda i: (0, 0))],
        out_specs=[pl.BlockSpec((GROUP_TM, D_MODEL), lambda i: (_prompt_tile(i, GROUP_TM), 0)),
                   pl.BlockSpec((GROUP_TM, D_MODEL), lambda i: (_sample_tile(i, GROUP_TM), 0))],
        out_shape=[jax.ShapeDtypeStruct((N_PROMPT, D_MODEL), F32),
                   jax.ShapeDtypeStruct((N_SAMPLE, D_MODEL), F32)],
        compiler_params=_params("arbitrary"),
        name="final_norm",
    )(x, gain)


def kernel(x_prompt, x_sample, state_hgrn, state_gla, state_pool, norm_ffn1, ffn1_w_gate, ffn1_w_up, ffn1_w_down, norm_mix, norm_ffn2, ffn2_w_gate, ffn2_w_up, ffn2_w_down, hgrn_lb_logits, hgrn_w_in, hgrn_o_norm, hgrn_w_out, gla_w_in, gla_w_gate_up, gla_b_gate, gla_o_norm, gla_w_out, pool_w_group, pool_scale, final_norm):
    x_parts = (x_prompt.reshape(N_PROMPT, D_MODEL), x_sample.reshape(N_SAMPLE, D_MODEL))
    gains = lambda a: a.reshape(a.shape[0], 1, D_MODEL)
    norm_ffn1, norm_mix, norm_ffn2 = gains(norm_ffn1), gains(norm_mix), gains(norm_ffn2)
    ffn1 = (ffn1_w_gate, ffn1_w_up, ffn1_w_down)
    ffn2 = (ffn2_w_gate, ffn2_w_up, ffn2_w_down)
    hgrn_w_out_b = hgrn_w_out.astype(BF16)
    gla_w_out_b = gla_w_out.astype(BF16)
    lb = _lower_bounds(hgrn_lb_logits)

    new_hgrn_p, new_gla_p, new_pool_p, new_pool_s = [], [], [], []
    new_hgrn_s = new_gla_s = None
    for li in range(DEPTH):
        j = li // N_MIXERS
        kind = li % N_MIXERS
        x = _ffn(x_parts if li == 0 else (x,), norm_ffn1, *ffn1, li)
        if kind == 0:
            proj = _norm_proj(x, norm_mix, li, hgrn_w_in, j, 4 * D_MODEL, "hgrn_in_proj")
            lb_l = lb[li:li + 1]
            groups = HG_HEADS // HGRN_HB
            o_p, st_p = _rec_prompt("hgrn", proj, lb_l, HG_HEADS, HGRN_HB, HG_FORGET_DIM, HG_HEAD_V,
                                    HGRN_SEG[j], 0, groups, 2 * groups)
            o_s, new_hgrn_s = _rec_sample(
                "hgrn",
                _sample_heads(proj, 0, HG_HEADS, HG_FORGET_DIM),
                _sample_heads(proj, D_MODEL, HG_HEADS, HG_FORGET_DIM),
                _sample_heads(proj, 2 * D_MODEL, HG_HEADS, HG_HEAD_V),
                lb_l.reshape(HG_HEADS, 1, HG_FORGET_DIM), state_hgrn, j, new_hgrn_s,
                HG_HEADS, HG_FORGET_DIM, HG_HEAD_V)
            new_hgrn_p.append(st_p)
            x = _mix_out(o_p, _unsplit_heads(o_s), proj, 3, hgrn_o_norm[j].reshape(1, D_MODEL), x,
                         hgrn_w_out_b, j, D_MODEL, "hgrn_out")
        elif kind == 1:
            n_main = 2 * GLA_DK + 2 * GLA_DV
            proj = _norm_proj(x, norm_mix, li, gla_w_in, j, n_main, "gla_in_proj")
            w_low = jnp.pad(gla_w_in[j, :, n_main:], ((0, 0), (0, LANES - GLA_GATE_RANK))).astype(BF16)
            w_up = jnp.pad(gla_w_gate_up[j], ((0, LANES - GLA_GATE_RANK), (0, 0))).astype(BF16)
            lf = _gla_gate(x, norm_mix, li, w_low, w_up, gla_b_gate[j].reshape(1, GLA_DK))
            kb = GLA_DK // (GLA_HB * GLA_HEAD_K)
            vb = 2 * GLA_DK // (GLA_HB * GLA_HEAD_V)
            o_p, st_p = _rec_prompt("gla", proj, lf, GLA_HEADS, GLA_HB, GLA_HEAD_K, GLA_HEAD_V,
                                    GLA_SEG, 0, kb, vb)
            o_s, new_gla_s = _rec_sample(
                "gla",
                _sample_heads(proj, 0, GLA_HEADS, GLA_HEAD_K),
                _sample_heads(proj, GLA_DK, GLA_HEADS, GLA_HEAD_K),
                _sample_heads(proj, 2 * GLA_DK, GLA_HEADS, GLA_HEAD_V),
                _sample_heads(lf, 0, GLA_HEADS, GLA_HEAD_K), state_gla, j, new_gla_s,
                GLA_HEADS, GLA_HEAD_K, GLA_HEAD_V)
            new_gla_p.append(st_p)
            x = _mix_out(o_p, _unsplit_heads(o_s), proj, (2 * GLA_DK + GLA_DV) // D_MODEL,
                         gla_o_norm[j].reshape(1, GLA_DV), x, gla_w_out_b, j,
                         GLA_HEAD_V, "gla_out")
        else:
            w_group = pool_w_group[j].astype(BF16)
            scale = pool_scale[j].reshape(1, D_MODEL)
            x, nb_p = _pool(x, norm_mix, li, jnp.zeros((BATCH, POOL_BUF, D_MODEL), F32), w_group, scale,
                            BATCH, SEQ, 1, POOL_TT, 0, 0)
            x, nb_s = _pool(x, norm_mix, li, state_pool[j], w_group, scale,
                            DEC_BATCH, DEC_SEQ, POOL_SAMPLE_NS, DEC_SEQ, N_PROMPT, min(POOL_BUF, PAST_LEN))
            new_pool_p.append(nb_p)
            new_pool_s.append(nb_s)
        x = _ffn((x,), norm_ffn2, *ffn2, li)

    y_p, y_s = _final_norm(x, final_norm.reshape(1, D_MODEL))
    return (y_p.reshape(BATCH, SEQ, D_MODEL), y_s.reshape(DEC_BATCH, DEC_SEQ, D_MODEL),
            jnp.stack(new_hgrn_p), jnp.stack(new_gla_p), jnp.stack(new_pool_p),
            new_hgrn_s, new_gla_s, jnp.stack(new_pool_s))
```

```python
import functools

import jax
import jax.numpy as jnp
from jax import lax
from jax.experimental import pallas as pl
from jax.experimental.pallas import tpu as pltpu

F32 = jnp.float32
BF16 = jnp.bfloat16

D_MODEL = 2048
BATCH = 4
SEQ = 2048
DEPTH = 4
DEC_BATCH = 128
DEC_SEQ = 8
PAST_LEN = 16384
N_MIXERS = 3

HG_FORGET_DIM = 128
HG_HEADS = D_MODEL // HG_FORGET_DIM
HG_HEAD_V = D_MODEL // HG_HEADS

GLA_HEADS = 4
GLA_DK = D_MODEL // 2
GLA_DV = D_MODEL
GLA_HEAD_K = GLA_DK // GLA_HEADS
GLA_HEAD_V = GLA_DV // GLA_HEADS
GLA_GATE_RANK = 16
GLA_GATE_NORM = 16.0

POOL_WINDOWS = (2, 4, 8, 16)
POOL_GROUP_DIM = D_MODEL // len(POOL_WINDOWS)
POOL_BUF = max(POOL_WINDOWS) - 1

D_FF = 5632
EPS = 1e-6

N_PROMPT = BATCH * SEQ
N_SAMPLE = DEC_BATCH * DEC_SEQ
N_TOKENS = N_PROMPT + N_SAMPLE

LANES = 128
VMEM_LIMIT = 56 * 1024 * 1024

FFN_TM = 1024
FFN_SUB = 256
FFN_X_CHUNKS = 8
PROJ_TM, PROJ_TN = 1024, 1024
OUT_TM = 256
REC_TT = 256
REC_C = 16
HGRN_HB = 16
HGRN_SEG = (32, 128)
GLA_HB, GLA_SEG = 4, 128
MXU_EXP_RANGE = 60.0
MXU_KEY_RANGE = 1e8
SAMPLE_BS = 4
POOL_TT = 256
POOL_SAMPLE_NS = 16


def _params(*sem):
    return pltpu.CompilerParams(dimension_semantics=sem, vmem_limit_bytes=VMEM_LIMIT)


def _rms(xf, g):
    return xf * lax.rsqrt(jnp.mean(xf * xf, axis=-1, keepdims=True) + EPS) * g


def _silu(a):
    return a * jax.nn.sigmoid(a)


def _log1p_exp_neg_abs(a):
    return jnp.log(1.0 + jnp.exp(-jnp.abs(a)))


def _log_sigmoid(a):
    return jnp.minimum(a, 0.0) - _log1p_exp_neg_abs(a)


def _ffn_body(src_tiles, li, *refs):
    x_srcs = refs[:len(src_tiles)]
    (g_ref, wg_hbm, wu_hbm, wd_hbm, o_ref,
     xs_ref, xn_ref, wg_buf, wu_buf, wd_buf, xsems, wsems) = refs[len(src_tiles):]
    i = pl.program_id(0)
    last_tile = pl.num_programs(0) - 1
    n_groups = D_FF // FFN_SUB
    rows = FFN_TM // FFN_X_CHUNKS

    def x_chunk(tile, c, action):
        first = 0
        for src, n in zip(x_srcs, src_tiles):
            def go(src=src, first=first):
                copy = pltpu.make_async_copy(
                    src.at[pl.ds((tile - first) * FFN_TM + c * rows, rows), :],
                    xs_ref.at[pl.ds(c * rows, rows), :], xsems.at[c])
                copy.start() if action == "start" else copy.wait()

            pl.when((tile >= first) & (tile < first + n))(go)
            first += n

    def w_group(group, slot, action):
        cols = pl.ds(pl.multiple_of(group * FFN_SUB, FFN_SUB), FFN_SUB)
        for k, (src, dst) in enumerate(((wg_hbm.at[li, :, cols], wg_buf.at[slot]),
                                        (wu_hbm.at[li, :, cols], wu_buf.at[slot]),
                                        (wd_hbm.at[li, cols, :], wd_buf.at[slot]))):
            copy = pltpu.make_async_copy(src, dst, wsems.at[slot, k])
            copy.start() if action == "start" else copy.wait()

    @pl.when(i == 0)
    def _():
        for slot in range(2):
            w_group(slot, slot, "start")
        for c in range(FFN_X_CHUNKS):
            x_chunk(i, c, "start")

    for c in range(FFN_X_CHUNKS):
        x_chunk(i, c, "wait")
    for c in range(FFN_X_CHUNKS):
        rs = slice(c * rows, (c + 1) * rows)
        x = xs_ref[rs, :]
        o_ref[rs, :] = x
        inv = lax.rsqrt(jnp.mean(x * x, axis=-1, keepdims=True) + EPS)
        xn_ref[rs, :] = (o_ref[rs, :] * inv * g_ref[...]).astype(BF16)

    def pair(p, carry):
        @pl.when(p < FFN_X_CHUNKS)
        def _():
            x_chunk(i + 1, p, "start")

        for slot in range(2):
            group = 2 * p + slot
            w_group(group, slot, "wait")
            xn = xn_ref[...]
            a = jnp.dot(xn, wg_buf[slot].astype(BF16), preferred_element_type=F32)
            u = jnp.dot(xn, wu_buf[slot].astype(BF16), preferred_element_type=F32)
            h = (0.5 * _silu(a) * u).astype(BF16)
            o_ref[...] += jnp.dot(h, wd_buf[slot].astype(BF16), preferred_element_type=F32)

            nxt = group + 2
            pl.when(nxt < n_groups)(lambda: w_group(nxt, slot, "start"))
            pl.when((nxt >= n_groups) & (i < last_tile))(lambda: w_group(nxt - n_groups, slot, "start"))
        return carry

    lax.fori_loop(0, n_groups // 2, pair, 0)


def _ffn(x_parts, gains, w_gate, w_up, w_down, li):
    src_tiles = tuple(p.shape[0] // FFN_TM for p in x_parts)
    assert all(p.shape[0] % FFN_TM == 0 for p in x_parts)
    n_pairs = D_FF // FFN_SUB // 2
    assert n_pairs * 2 * FFN_SUB == D_FF and n_pairs >= FFN_X_CHUNKS
    m = sum(p.shape[0] for p in x_parts)
    hbm = pl.BlockSpec(memory_space=pl.ANY)
    return pl.pallas_call(
        functools.partial(_ffn_body, src_tiles, li),
        grid=(m // FFN_TM,),
        in_specs=[hbm] * len(x_parts) + [pl.BlockSpec((None, 1, D_MODEL), lambda i: (li, 0, 0)), hbm, hbm, hbm],
        out_specs=pl.BlockSpec((FFN_TM, D_MODEL), lambda i: (i, 0)),
        out_shape=jax.ShapeDtypeStruct((m, D_MODEL), F32),
        scratch_shapes=[pltpu.VMEM((FFN_TM, D_MODEL), F32),
                        pltpu.VMEM((FFN_TM, D_MODEL), BF16),
                        pltpu.VMEM((2, D_MODEL, FFN_SUB), F32),
                        pltpu.VMEM((2, D_MODEL, FFN_SUB), F32),
                        pltpu.VMEM((2, FFN_SUB, D_MODEL), F32),
                        pltpu.SemaphoreType.DMA((FFN_X_CHUNKS,)),
                        pltpu.SemaphoreType.DMA((2, 3))],
        compiler_params=_params("arbitrary"),
        name="ffn",
    )(*x_parts, gains, w_gate, w_up, w_down)


def _proj_body(x_ref, g_ref, w_ref, o_ref, xn_ref):
    @pl.when(pl.program_id(1) == 0)
    def _():
        xn_ref[...] = _rms(x_ref[...], g_ref[...]).astype(BF16)

    o_ref[...] = jnp.dot(xn_ref[...], w_ref[...].astype(BF16), preferred_element_type=F32)


def _norm_proj(x, gains, li, w, wi, n, name):
    m = x.shape[0]
    return pl.pallas_call(
        _proj_body,
        grid=(m // PROJ_TM, n // PROJ_TN),
        in_specs=[
            pl.BlockSpec((PROJ_TM, D_MODEL), lambda i, j: (i, 0)),
            pl.BlockSpec((None, 1, D_MODEL), lambda i, j: (li, 0, 0)),
            pl.BlockSpec((None, D_MODEL, PROJ_TN), lambda i, j: (wi, 0, j)),
        ],
        out_specs=pl.BlockSpec((PROJ_TM, PROJ_TN), lambda i, j: (i, j)),
        out_shape=jax.ShapeDtypeStruct((m, n), F32),
        scratch_shapes=[pltpu.VMEM((PROJ_TM, D_MODEL), BF16)],
        compiler_params=_params("parallel", "arbitrary"),
        name=name,
    )(x, gains, w)


def _lb_body(l_ref, o_ref):
    l = l_ref[...]
    e = jnp.exp(l - jnp.max(l, axis=0, keepdims=True))
    p = e / jnp.sum(e, axis=0, keepdims=True)
    gamma0 = p[0:1]
    gamma = gamma0
    o_ref[0:1, :] = gamma - gamma0
    for i in range(1, DEPTH):
        gamma = gamma + p[i:i + 1]
        o_ref[i:i + 1, :] = gamma - gamma0


def _lower_bounds(logits):
    return pl.pallas_call(
        _lb_body,
        out_shape=jax.ShapeDtypeStruct((DEPTH, D_MODEL), F32),
        name="hgrn_lower_bounds",
    )(logits)


def _chunk_step(q, k, v, lf, s_t, state_is_kv=False):
    c, kdim = q.shape
    rows = lax.broadcasted_iota(jnp.int32, (c, kdim), 0)
    b = lf
    d = 1
    while d < c:
        b = b + jnp.where(rows >= d, pltpu.roll(b, d, axis=0), 0.0)
        d *= 2
    b_last = b[c - 1:c, :]

    state_contract = 0 if state_is_kv else 1
    o = lax.dot_general((q * jnp.exp(b)).astype(BF16), s_t.astype(BF16),
                        (((1,), (state_contract,)), ((), ())), preferred_element_type=F32)

    rows_c = lax.broadcasted_iota(jnp.int32, (c, 1), 0)
    for s in range(c):
        w = jnp.exp(b - b[s:s + 1, :]) * (q * k[s:s + 1, :])
        col = jnp.sum(w, axis=-1, keepdims=True)
        col = jnp.where(rows_c >= s, col, 0.0)
        o = o + col * v[s:s + 1, :]

    kd = (k * jnp.exp(b_last - b)).astype(BF16)
    contract_rows = (((0,), (0,)), ((), ()))
    if state_is_kv:
        upd = lax.dot_general(kd, v.astype(BF16), contract_rows, preferred_element_type=F32)
        decay = jnp.broadcast_to(jnp.exp(b_last), (LANES, kdim)).T
        return o, s_t * jnp.tile(decay, (1, v.shape[1] // LANES)) + upd
    upd = lax.dot_general(v.astype(BF16), kd, contract_rows, preferred_element_type=F32)
    return o, s_t * jnp.exp(b_last) + upd


def _hgrn_gates(q_raw, f_raw, lb):
    q = _silu(q_raw) * (HG_FORGET_DIM ** -0.5)
    gate = jnp.log1p(-lb) + _log_sigmoid(f_raw)
    log_lb = jnp.log(lb)
    lf = jnp.maximum(log_lb, gate) + _log1p_exp_neg_abs(log_lb - gate)
    k = (1.0 - lb) * jax.nn.sigmoid(-f_raw)
    return q, k, lf


def _seg_cumsum(x, c):
    r = lax.broadcasted_iota(jnp.int32, x.shape, 0) & (c - 1)
    d = 1
    while d < c:
        x = x + jnp.where(r >= d, pltpu.roll(x, d, axis=0), 0.0)
        d *= 2
    return x


def _rec_prompt_body(kind, hb, kdim, vdim, c, q_ref, k_ref, v_ref, aux_ref, o_ref, st_ref,
                     s_ref, sn_ref, qs_ref, ks_ref, ls_ref):
    t = pl.program_id(2)
    tt = REC_TT
    nseg = tt // c

    @pl.when(t == 0)
    def _():
        s_ref[...] = jnp.zeros_like(s_ref)

    ri = lax.broadcasted_iota(jnp.int32, (tt, tt), 0)
    ci = lax.broadcasted_iota(jnp.int32, (tt, tt), 1)
    shift = c.bit_length() - 1
    mask = ((ri >> shift) == (ci >> shift)) & (ci <= ri)
    segs = [slice(j * c, (j + 1) * c) for j in range(nseg)]
    contract_last = (((1,), (1,)), ((), ()))
    contract_rows = (((0,), (0,)), ((), ()))

    worst = jnp.zeros((tt, kdim), F32)
    for h in range(hb):
        ksl = slice(h * kdim, (h + 1) * kdim)
        vsl = slice(h * vdim, (h + 1) * vdim)
        if kind == "hgrn":
            q, k, lf = _hgrn_gates(q_ref[:, ksl], k_ref[:, ksl], aux_ref[:, ksl])
        else:
            q = q_ref[:, ksl] * (GLA_HEAD_K ** -0.5)
            k = k_ref[:, ksl]
            lf = aux_ref[:, ksl]
        b = _seg_cumsum(lf, c)
        qs_ref[h] = q
        ks_ref[h] = k
        ls_ref[h] = lf
        worst = jnp.maximum(worst, -b)
        if kind != "hgrn":
            worst = jnp.maximum(worst, jnp.abs(k) * (MXU_EXP_RANGE / MXU_KEY_RANGE))

        bl3 = b.reshape(nseg, c, kdim)[:, c - 1:c, :]
        e = jnp.exp(bl3)
        qt = (q * jnp.exp(b)).astype(BF16)
        k_grown = k * jnp.exp(-b)
        kt = k_grown.astype(BF16)
        kd = (k_grown.reshape(nseg, c, kdim) * e).reshape(tt, kdim).astype(BF16)
        vb = v_ref[:, vsl].astype(BF16)
        a = lax.dot_general(qt, kt, contract_last, preferred_element_type=F32)
        att = jnp.where(mask, a, 0.0).astype(BF16)
        o = jnp.dot(att, vb, preferred_element_type=F32)
        upd = [lax.dot_general(vb[rs], kd[rs], contract_rows, preferred_element_type=F32) for rs in segs]
        s_t = s_ref[h]
        states = []
        for j in range(nseg):
            states.append(s_t.astype(BF16))
            s_t = s_t * e[j] + upd[j]
        sn_ref[h] = s_t
        for j, rs in enumerate(segs):
            o_ref[rs, vsl] = o[rs] + lax.dot_general(qt[rs], states[j], contract_last,
                                                     preferred_element_type=F32)

    in_range = jnp.max(worst) <= MXU_EXP_RANGE

    @pl.when(jnp.logical_not(in_range))
    def _():
        for h in range(hb):
            vsl = slice(h * vdim, (h + 1) * vdim)
            sn_ref[h] = s_ref[h]

            def sub(j, carry):
                rs = pl.ds(pl.multiple_of(j * REC_C, REC_C), REC_C)
                o, s_new = _chunk_step(qs_ref[h, rs, :], ks_ref[h, rs, :], v_ref[rs, vsl],
                                       ls_ref[h, rs, :], sn_ref[h])
                sn_ref[h] = s_new
                o_ref[rs, vsl] = o
                return carry

            lax.fori_loop(0, tt // REC_C, sub, 0)

    s_ref[...] = sn_ref[...]

    @pl.when(t == pl.num_programs(2) - 1)
    def _():
        for h in range(hb):
            st_ref[h] = sn_ref[h].T


def _rec_prompt(kind, proj, aux, heads, hb, kdim, vdim, c, q_blk, k_blk, v_blk):
    nt = SEQ // REC_TT
    row = lambda b, g, t: b * nt + t
    if kind == "hgrn":
        aux_spec = pl.BlockSpec((1, hb * kdim), lambda b, g, t: (0, g))
    else:
        aux_spec = pl.BlockSpec((REC_TT, hb * kdim), lambda b, g, t: (row(b, g, t), g))
    return pl.pallas_call(
        functools.partial(_rec_prompt_body, kind, hb, kdim, vdim, c),
        grid=(BATCH, heads // hb, nt),
        in_specs=[
            pl.BlockSpec((REC_TT, hb * kdim), lambda b, g, t: (row(b, g, t), q_blk + g)),
            pl.BlockSpec((REC_TT, hb * kdim), lambda b, g, t: (row(b, g, t), k_blk + g)),
            pl.BlockSpec((REC_TT, hb * vdim), lambda b, g, t: (row(b, g, t), v_blk + g)),
            aux_spec,
        ],
        out_specs=[
            pl.BlockSpec((REC_TT, hb * vdim), lambda b, g, t: (row(b, g, t), g)),
            pl.BlockSpec((None, hb, kdim, vdim), lambda b, g, t: (b, g, 0, 0)),
        ],
        out_shape=[
            jax.ShapeDtypeStruct((N_PROMPT, heads * vdim), F32),
            jax.ShapeDtypeStruct((BATCH, heads, kdim, vdim), F32),
        ],
        scratch_shapes=[pltpu.VMEM((hb, vdim, kdim), F32)] * 2
        + [pltpu.VMEM((hb, REC_TT, kdim), F32)] * 3,
        compiler_params=_params("parallel", "parallel", "arbitrary"),
        name=kind + "_rec_prompt",
    )(proj, proj, proj, aux)


def _rec_sample_body(kind, heads, slot, q_ref, k_ref, v_ref, aux_ref, st_ref, *rest):
    o_ref, nst_ref = rest[-2:]

    for other in range(nst_ref.shape[0]):
        if other != slot:
            nst_ref[other] = jnp.zeros(nst_ref.shape[1:], F32)

    def item(i, carry):
        s = i // heads
        h = i % heads
        if kind == "hgrn":
            q, k, lf = _hgrn_gates(q_ref[s, h], k_ref[s, h], aux_ref[h])
        else:
            q = q_ref[s, h] * (GLA_HEAD_K ** -0.5)
            k = k_ref[s, h]
            lf = aux_ref[s, h]
        o, s_new = _chunk_step(q, k, v_ref[s, h], lf, st_ref[s, h], state_is_kv=True)
        o_ref[s, h] = o
        nst_ref[slot, s, h] = s_new
        return carry

    lax.fori_loop(0, SAMPLE_BS * heads, item, 0, unroll=16 if kind == "hgrn" else 4)


def _rec_sample(kind, q, k, v, aux, states, j, new_states, heads, kdim, vdim):
    blk = lambda last2: pl.BlockSpec((SAMPLE_BS, heads) + last2, lambda i: (i, 0, 0, 0))
    st_spec = pl.BlockSpec((None, SAMPLE_BS, heads, kdim, vdim), lambda i: (j, i, 0, 0, 0))
    if kind == "hgrn":
        aux_spec = pl.BlockSpec((heads, 1, kdim), lambda i: (0, 0, 0))
    else:
        aux_spec = blk((DEC_SEQ, kdim))
    in_specs = [blk((DEC_SEQ, kdim)), blk((DEC_SEQ, kdim)), blk((DEC_SEQ, vdim)), aux_spec, st_spec]
    args = [q, k, v, aux, states]
    if new_states is None:
        aliases, slot = {}, j
        new_spec = pl.BlockSpec((states.shape[0], SAMPLE_BS, heads, kdim, vdim), lambda i: (0, i, 0, 0, 0))
    else:
        in_specs.append(pl.BlockSpec(memory_space=pl.ANY))
        args.append(new_states)
        aliases, slot = {len(args) - 1: 1}, 0
        new_spec = pl.BlockSpec((1, SAMPLE_BS, heads, kdim, vdim), lambda i: (j, i, 0, 0, 0))
    return pl.pallas_call(
        functools.partial(_rec_sample_body, kind, heads, slot),
        grid=(DEC_BATCH // SAMPLE_BS,),
        in_specs=in_specs,
        out_specs=[blk((DEC_SEQ, vdim)), new_spec],
        out_shape=[
            jax.ShapeDtypeStruct((DEC_BATCH, heads, DEC_SEQ, vdim), F32),
            jax.ShapeDtypeStruct(states.shape, F32),
        ],
        input_output_aliases=aliases,
        compiler_params=_params("parallel"),
        name=kind + "_rec_sample",
    )(*args)


def _sample_heads(proj, col0, heads, hdim):
    a = proj[N_PROMPT:, col0:col0 + heads * hdim]
    return a.reshape(DEC_BATCH, DEC_SEQ, heads, hdim).transpose(0, 2, 1, 3)


def _unsplit_heads(o_sample):
    return o_sample.transpose(0, 2, 1, 3).reshape(N_SAMPLE, D_MODEL)


def _prompt_tile(i, tile):
    return jnp.minimum(i, N_PROMPT // tile - 1)


def _sample_tile(i, tile):
    return jnp.maximum(i - N_PROMPT // tile, 0)


def _mix_out_body(norm_dim, op_ref, os_ref, gate_ref, gain_ref, x_ref, w_ref, y_ref):
    def run(o_ref):
        half = OUT_TM // 2
        for r0 in (0, half):
            rs = slice(r0, r0 + half)
            parts = []
            for c0 in range(0, D_MODEL, norm_dim):
                sl = slice(c0, c0 + norm_dim)
                on = _rms(o_ref[rs, sl], gain_ref[:, sl])
                parts.append((on * _silu(gate_ref[rs, sl])).astype(BF16))
            lhs = parts[0] if len(parts) == 1 else jnp.concatenate(parts, axis=1)
            y_ref[rs, :] = x_ref[rs, :] + jnp.dot(lhs, w_ref[...], preferred_element_type=F32)

    is_prompt = pl.program_id(0) < N_PROMPT // OUT_TM
    pl.when(is_prompt)(lambda: run(op_ref))
    pl.when(jnp.logical_not(is_prompt))(lambda: run(os_ref))


def _mix_out(o_prompt, o_sample, proj, gate_blk, gain, x, w_out, wi, norm_dim, name):
    m = x.shape[0]
    row = lambda i: (i, 0)
    return pl.pallas_call(
        functools.partial(_mix_out_body, norm_dim),
        grid=(m // OUT_TM,),
        in_specs=[
            pl.BlockSpec((OUT_TM, D_MODEL), lambda i: (_prompt_tile(i, OUT_TM), 0)),
            pl.BlockSpec((OUT_TM, D_MODEL), lambda i: (_sample_tile(i, OUT_TM), 0)),
            pl.BlockSpec((OUT_TM, D_MODEL), lambda i: (i, gate_blk)),
            pl.BlockSpec((1, D_MODEL), lambda i: (0, 0)),
            pl.BlockSpec((OUT_TM, D_MODEL), row),
            pl.BlockSpec((None, D_MODEL, D_MODEL), lambda i: (wi, 0, 0)),
        ],
        out_specs=pl.BlockSpec((OUT_TM, D_MODEL), row),
        out_shape=jax.ShapeDtypeStruct((m, D_MODEL), F32),
        compiler_params=_params("arbitrary"),
        name=name,
    )(o_prompt, o_sample, proj, gain, x, w_out)


def _gla_gate_body(x_ref, g_ref, wl_ref, wu_ref, b_ref, o_ref):
    xn = _rms(x_ref[...], g_ref[...]).astype(BF16)
    low = jnp.dot(xn, wl_ref[...], preferred_element_type=F32)
    z = jnp.dot(low.astype(BF16), wu_ref[...], preferred_element_type=F32) + b_ref[...]
    o_ref[...] = _log_sigmoid(z) / GLA_GATE_NORM


def _gla_gate(x, gains, li, w_low, w_up, bias):
    m = x.shape[0]
    tm = 512
    return pl.pallas_call(
        _gla_gate_body,
        grid=(m // tm,),
        in_specs=[
            pl.BlockSpec((tm, D_MODEL), lambda i: (i, 0)),
            pl.BlockSpec((None, 1, D_MODEL), lambda i: (li, 0, 0)),
            pl.BlockSpec((D_MODEL, LANES), lambda i: (0, 0)),
            pl.BlockSpec((LANES, GLA_DK), lambda i: (0, 0)),
            pl.BlockSpec((1, GLA_DK), lambda i: (0, 0)),
        ],
        out_specs=pl.BlockSpec((tm, GLA_DK), lambda i: (i, 0)),
        out_shape=jax.ShapeDtypeStruct((m, GLA_DK), F32),
        compiler_params=_params("parallel"),
        name="gla_gate",
    )(x, gains, w_low, w_up, bias)


def _pool_body(ns, tt, n_prev, x_ref, g_ref, buf_ref, w_ref, sc_ref, y_ref, nb_ref, xe_ref):
    t = pl.program_id(1)
    pad = POOL_BUF + 1

    @pl.when(t == 0)
    def _():
        xe_ref[:, 0:1, :] = jnp.zeros((ns, 1, D_MODEL), F32)
        xe_ref[:, 1:pad, :] = buf_ref[...]

    @pl.when(t > 0)
    def _():
        xe_ref[:, 0:pad, :] = xe_ref[:, tt:tt + pad, :]

    x = x_ref[...]
    xn = _rms(x, g_ref[...])
    xe_ref[:, pad:pad + tt, :] = xn.reshape(ns, tt, D_MODEL)
    pos = t * tt + lax.broadcasted_iota(jnp.int32, (1, tt, 1), 1)
    for gi, w in enumerate(POOL_WINDOWS):
        sl = slice(gi * POOL_GROUP_DIM, (gi + 1) * POOL_GROUP_DIM)
        cur = xn[:, sl].reshape(ns, tt, POOL_GROUP_DIM)
        win = cur
        for d in range(1, w):
            win = win + xe_ref[:, pad - d:pad - d + tt, sl]
        cnt = jnp.minimum(w, pos + 1 + n_prev).astype(F32)
        y = (win / cnt - cur).reshape(ns * tt, POOL_GROUP_DIM).astype(BF16)
        h = jnp.dot(y, w_ref[gi], preferred_element_type=F32) * sc_ref[:, sl]
        y_ref[:, sl] = x[:, sl] + h

    @pl.when(t == pl.num_programs(1) - 1)
    def _():
        nb_ref[...] = xe_ref[:, tt + 1:tt + pad, :]


def _pool(x, gains, li, buf, w_group, scale, n_seq, seq_len, ns, tt, row0, n_prev):
    nt = seq_len // tt
    assert ns == 1 or nt == 1
    r0 = row0 // (ns * tt)
    row = lambda b, t: (r0 + b * nt + t, 0)
    return pl.pallas_call(
        functools.partial(_pool_body, ns, tt, n_prev),
        grid=(n_seq // ns, nt),
        in_specs=[
            pl.BlockSpec((ns * tt, D_MODEL), row),
            pl.BlockSpec((None, 1, D_MODEL), lambda b, t: (li, 0, 0)),
            pl.BlockSpec((ns, POOL_BUF, D_MODEL), lambda b, t: (b, 0, 0)),
            pl.BlockSpec((len(POOL_WINDOWS), POOL_GROUP_DIM, POOL_GROUP_DIM), lambda b, t: (0, 0, 0)),
            pl.BlockSpec((1, D_MODEL), lambda b, t: (0, 0)),
        ],
        out_specs=[
            pl.BlockSpec((ns * tt, D_MODEL), row),
            pl.BlockSpec((ns, POOL_BUF, D_MODEL), lambda b, t: (b, 0, 0)),
        ],
        out_shape=[
            jax.ShapeDtypeStruct(x.shape, F32),
            jax.ShapeDtypeStruct((n_seq, POOL_BUF, D_MODEL), F32),
        ],
        scratch_shapes=[pltpu.VMEM((ns, tt + POOL_BUF + 1, D_MODEL), F32)],
        input_output_aliases={0: 0},
        compiler_params=_params("parallel", "arbitrary"),
        name="pool_mixer",
    )(x, gains, buf, w_group, scale)


GROUP_TM = 1024


def _final_norm_body(x_ref, g_ref, p_ref, s_ref):
    is_prompt = pl.program_id(0) < N_PROMPT // GROUP_TM

    @pl.when(is_prompt)
    def _():
        p_ref[...] = _rms(x_ref[...], g_ref[...])

    @pl.when(jnp.logical_not(is_prompt))
    def _():
        s_ref[...] = _rms(x_ref[...], g_ref[...])


def _final_norm(x, gain):
    return pl.pallas_call(
        _final_norm_body,
        grid=(N_TOKENS // GROUP_TM,),
        in_specs=[pl.BlockSpec((GROUP_TM, D_MODEL), lambda i: (i, 0)),
                  pl.BlockSpec((1, D_MODEL), lambda i: (0, 0))],
        out_specs=[pl.BlockSpec((GROUP_TM, D_MODEL), lambda i: (_prompt_tile(i, GROUP_TM), 0)),
                   pl.BlockSpec((GROUP_TM, D_MODEL), lambda i: (_sample_tile(i, GROUP_TM), 0))],
        out_shape=[jax.ShapeDtypeStruct((N_PROMPT, D_MODEL), F32),
                   jax.ShapeDtypeStruct((N_SAMPLE, D_MODEL), F32)],
        compiler_params=_params("arbitrary"),
        name="final_norm",
    )(x, gain)


def kernel(x_prompt, x_sample, state_hgrn, state_gla, state_pool, norm_ffn1, ffn1_w_gate, ffn1_w_up, ffn1_w_down, norm_mix, norm_ffn2, ffn2_w_gate, ffn2_w_up, ffn2_w_down, hgrn_lb_logits, hgrn_w_in, hgrn_o_norm, hgrn_w_out, gla_w_in, gla_w_gate_up, gla_b_gate, gla_o_norm, gla_w_out, pool_w_group, pool_scale, final_norm):
    x_parts = (x_prompt.reshape(N_PROMPT, D_MODEL), x_sample.reshape(N_SAMPLE, D_MODEL))
    gains = lambda a: a.reshape(a.shape[0], 1, D_MODEL)
    norm_ffn1, norm_mix, norm_ffn2 = gains(norm_ffn1), gains(norm_mix), gains(norm_ffn2)
    ffn1 = (ffn1_w_gate, ffn1_w_up, ffn1_w_down)
    ffn2 = (ffn2_w_gate, ffn2_w_up, ffn2_w_down)
    hgrn_w_out_b = hgrn_w_out.astype(BF16)
    gla_w_out_b = gla_w_out.astype(BF16)
    lb = _lower_bounds(hgrn_lb_logits)

    new_hgrn_p, new_gla_p, new_pool_p, new_pool_s = [], [], [], []
    new_hgrn_s = new_gla_s = None
    for li in range(DEPTH):
        j = li // N_MIXERS
        kind = li % N_MIXERS
        x = _ffn(x_parts if li == 0 else (x,), norm_ffn1, *ffn1, li)
        if kind == 0:
            proj = _norm_proj(x, norm_mix, li, hgrn_w_in, j, 4 * D_MODEL, "hgrn_in_proj")
            lb_l = lb[li:li + 1]
            groups = HG_HEADS // HGRN_HB
            o_p, st_p = _rec_prompt("hgrn", proj, lb_l, HG_HEADS, HGRN_HB, HG_FORGET_DIM, HG_HEAD_V,
                                    HGRN_SEG[j], 0, groups, 2 * groups)
            o_s, new_hgrn_s = _rec_sample(
                "hgrn",
                _sample_heads(proj, 0, HG_HEADS, HG_FORGET_DIM),
                _sample_heads(proj, D_MODEL, HG_HEADS, HG_FORGET_DIM),
                _sample_heads(proj, 2 * D_MODEL, HG_HEADS, HG_HEAD_V),
                lb_l.reshape(HG_HEADS, 1, HG_FORGET_DIM), state_hgrn, j, new_hgrn_s,
                HG_HEADS, HG_FORGET_DIM, HG_HEAD_V)
            new_hgrn_p.append(st_p)
            x = _mix_out(o_p, _unsplit_heads(o_s), proj, 3, hgrn_o_norm[j].reshape(1, D_MODEL), x,
                         hgrn_w_out_b, j, D_MODEL, "hgrn_out")
        elif kind == 1:
            n_main = 2 * GLA_DK + 2 * GLA_DV
            proj = _norm_proj(x, norm_mix, li, gla_w_in, j, n_main, "gla_in_proj")
            w_low = jnp.pad(gla_w_in[j, :, n_main:], ((0, 0), (0, LANES - GLA_GATE_RANK))).astype(BF16)
            w_up = jnp.pad(gla_w_gate_up[j], ((0, LANES - GLA_GATE_RANK), (0, 0))).astype(BF16)
            lf = _gla_gate(x, norm_mix, li, w_low, w_up, gla_b_gate[j].reshape(1, GLA_DK))
            kb = GLA_DK // (GLA_HB * GLA_HEAD_K)
            vb = 2 * GLA_DK // (GLA_HB * GLA_HEAD_V)
            o_p, st_p = _rec_prompt("gla", proj, lf, GLA_HEADS, GLA_HB, GLA_HEAD_K, GLA_HEAD_V,
                                    GLA_SEG, 0, kb, vb)
            o_s, new_gla_s = _rec_sample(
                "gla",
                _sample_heads(proj, 0, GLA_HEADS, GLA_HEAD_K),
                _sample_heads(proj, GLA_DK, GLA_HEADS, GLA_HEAD_K),
                _sample_heads(proj, 2 * GLA_DK, GLA_HEADS, GLA_HEAD_V),
                _sample_heads(lf, 0, GLA_HEADS, GLA_HEAD_K), state_gla, j, new_gla_s,
                GLA_HEADS, GLA_HEAD_K, GLA_HEAD_V)
            new_gla_p.append(st_p)
            x = _mix_out(o_p, _unsplit_heads(o_s), proj, (2 * GLA_DK + GLA_DV) // D_MODEL,
                         gla_o_norm[j].reshape(1, GLA_DV), x, gla_w_out_b, j,
                         GLA_HEAD_V, "gla_out")
        else:
            w_group = pool_w_group[j].astype(BF16)
            scale = pool_scale[j].reshape(1, D_MODEL)
            x, nb_p = _pool(x, norm_mix, li, jnp.zeros((BATCH, POOL_BUF, D_MODEL), F32), w_group, scale,
                            BATCH, SEQ, 1, POOL_TT, 0, 0)
            x, nb_s = _pool(x, norm_mix, li, state_pool[j], w_group, scale,
                            DEC_BATCH, DEC_SEQ, POOL_SAMPLE_NS, DEC_SEQ, N_PROMPT, min(POOL_BUF, PAST_LEN))
            new_pool_p.append(nb_p)
            new_pool_s.append(nb_s)
        x = _ffn((x,), norm_ffn2, *ffn2, li)

    y_p, y_s = _final_norm(x, final_norm.reshape(1, D_MODEL))
    return (y_p.reshape(BATCH, SEQ, D_MODEL), y_s.reshape(DEC_BATCH, DEC_SEQ, D_MODEL),
            jnp.stack(new_hgrn_p), jnp.stack(new_gla_p), jnp.stack(new_pool_p),
            new_hgrn_s, new_gla_s, jnp.stack(new_pool_s))
```

```python
import functools

import jax
import jax.numpy as jnp
from jax import lax
from jax.experimental import pallas as pl
from jax.experimental.pallas import tpu as pltpu

F32 = jnp.float32
BF16 = jnp.bfloat16

D_MODEL = 2048
BATCH = 4
SEQ = 2048
DEPTH = 4
DEC_BATCH = 128
DEC_SEQ = 8
PAST_LEN = 16384
N_MIXERS = 3

HG_FORGET_DIM = 128
HG_HEADS = D_MODEL // HG_FORGET_DIM
HG_HEAD_V = D_MODEL // HG_HEADS

GLA_HEADS = 4
GLA_DK = D_MODEL // 2
GLA_DV = D_MODEL
GLA_HEAD_K = GLA_DK // GLA_HEADS
GLA_HEAD_V = GLA_DV // GLA_HEADS
GLA_GATE_RANK = 16
GLA_GATE_NORM = 16.0

POOL_WINDOWS = (2, 4, 8, 16)
POOL_GROUP_DIM = D_MODEL // len(POOL_WINDOWS)
POOL_BUF = max(POOL_WINDOWS) - 1

D_FF = 5632
EPS = 1e-6

N_PROMPT = BATCH * SEQ
N_SAMPLE = DEC_BATCH * DEC_SEQ
N_TOKENS = N_PROMPT + N_SAMPLE

LANES = 128
VMEM_LIMIT = 56 * 1024 * 1024
FFN_VMEM_LIMIT = 61 * 1024 * 1024

FFN_TM, FFN_TF = 1024, 512
FFN_SUB = 256
FFN_X_CHUNKS = 8
PROJ_TM, PROJ_TN = 1024, 1024
OUT_TM = 512
REC_TT = 256
REC_C = 16
HGRN_HB = 16
HGRN_SEG = (32, 128)
GLA_HB, GLA_SEG = 4, 128
MXU_EXP_RANGE = 60.0
MXU_KEY_RANGE = 1e8
SAMPLE_BS = 4
POOL_TT = 256
POOL_SAMPLE_NS = 16


def _params(*sem, vmem=VMEM_LIMIT):
    return pltpu.CompilerParams(dimension_semantics=sem, vmem_limit_bytes=vmem)


def _rms(xf, g):
    return xf * lax.rsqrt(jnp.mean(xf * xf, axis=-1, keepdims=True) + EPS) * g


def _silu(a):
    return a * jax.nn.sigmoid(a)


def _log1p_exp_neg_abs(a):
    return jnp.log(1.0 + jnp.exp(-jnp.abs(a)))


def _log_sigmoid(a):
    return jnp.minimum(a, 0.0) - _log1p_exp_neg_abs(a)


def _ffn_body(src_tiles, *refs):
    x_srcs = refs[:len(src_tiles)]
    g_ref, wg_ref, wu_ref, wd_ref, o_ref, xs_ref, xn_ref, sems = refs[len(src_tiles):]
    i = pl.program_id(0)
    f = pl.program_id(1)
    rows = FFN_TM // FFN_X_CHUNKS

    def chunk(tile, c, action):
        first = 0
        for src, n in zip(x_srcs, src_tiles):
            def go(src=src, first=first):
                copy = pltpu.make_async_copy(
                    src.at[pl.ds((tile - first) * FFN_TM + c * rows, rows), :],
                    xs_ref.at[pl.ds(c * rows, rows), :], sems.at[c])
                copy.start() if action == "start" else copy.wait()

            pl.when((tile >= first) & (tile < first + n))(go)
            first += n

    @pl.when((i == 0) & (f == 0))
    def _():
        for c in range(FFN_X_CHUNKS):
            chunk(i, c, "start")

    @pl.when(f == 0)
    def _():
        for c in range(FFN_X_CHUNKS):
            chunk(i, c, "wait")
        for c in range(FFN_X_CHUNKS):
            rs = slice(c * rows, (c + 1) * rows)
            x = xs_ref[rs, :]
            o_ref[rs, :] = x
            inv = lax.rsqrt(jnp.mean(x * x, axis=-1, keepdims=True) + EPS)
            xn_ref[rs, :] = (o_ref[rs, :] * inv * g_ref[...]).astype(BF16)

    for c in range(FFN_X_CHUNKS):
        @pl.when(f == c + 1)
        def _():
            chunk(i + 1, c, "start")

    xn = xn_ref[...]
    for c0 in range(0, FFN_TF, FFN_SUB):
        cs = slice(c0, c0 + FFN_SUB)
        a = jnp.dot(xn, wg_ref[:, cs].astype(BF16), preferred_element_type=F32)
        u = jnp.dot(xn, wu_ref[:, cs].astype(BF16), preferred_element_type=F32)
        h = (0.5 * _silu(a) * u).astype(BF16)
        o_ref[...] += jnp.dot(h, wd_ref[cs, :].astype(BF16), preferred_element_type=F32)


def _ffn(x_parts, gains, w_gate, w_up, w_down, li):
    src_tiles = tuple(p.shape[0] // FFN_TM for p in x_parts)
    assert all(p.shape[0] % FFN_TM == 0 for p in x_parts)
    assert D_FF // FFN_TF > FFN_X_CHUNKS
    m = sum(p.shape[0] for p in x_parts)
    return pl.pallas_call(
        functools.partial(_ffn_body, src_tiles),
        grid=(m // FFN_TM, D_FF // FFN_TF),
        in_specs=[pl.BlockSpec(memory_space=pl.ANY)] * len(x_parts) + [
            pl.BlockSpec((None, 1, D_MODEL), lambda i, f: (li, 0, 0)),
            pl.BlockSpec((None, D_MODEL, FFN_TF), lambda i, f: (li, 0, f)),
            pl.BlockSpec((None, D_MODEL, FFN_TF), lambda i, f: (li, 0, f)),
            pl.BlockSpec((None, FFN_TF, D_MODEL), lambda i, f: (li, f, 0)),
        ],
        out_specs=pl.BlockSpec((FFN_TM, D_MODEL), lambda i, f: (i, 0)),
        out_shape=jax.ShapeDtypeStruct((m, D_MODEL), F32),
        scratch_shapes=[pltpu.VMEM((FFN_TM, D_MODEL), F32),
                        pltpu.VMEM((FFN_TM, D_MODEL), BF16),
                        pltpu.SemaphoreType.DMA((FFN_X_CHUNKS,))],
        compiler_params=_params("arbitrary", "arbitrary", vmem=FFN_VMEM_LIMIT),
        name="ffn",
    )(*x_parts, gains, w_gate, w_up, w_down)


def _proj_body(x_ref, g_ref, w_ref, o_ref, xn_ref):
    @pl.when(pl.program_id(1) == 0)
    def _():
        xn_ref[...] = _rms(x_ref[...], g_ref[...]).astype(BF16)

    o_ref[...] = jnp.dot(xn_ref[...], w_ref[...].astype(BF16), preferred_element_type=F32)


def _norm_proj(x, gains, li, w, wi, n, name):
    m = x.shape[0]
    return pl.pallas_call(
        _proj_body,
        grid=(m // PROJ_TM, n // PROJ_TN),
        in_specs=[
            pl.BlockSpec((PROJ_TM, D_MODEL), lambda i, j: (i, 0)),
            pl.BlockSpec((None, 1, D_MODEL), lambda i, j: (li, 0, 0)),
            pl.BlockSpec((None, D_MODEL, PROJ_TN), lambda i, j: (wi, 0, j)),
        ],
        out_specs=pl.BlockSpec((PROJ_TM, PROJ_TN), lambda i, j: (i, j)),
        out_shape=jax.ShapeDtypeStruct((m, n), F32),
        scratch_shapes=[pltpu.VMEM((PROJ_TM, D_MODEL), BF16)],
        compiler_params=_params("parallel", "arbitrary"),
        name=name,
    )(x, gains, w)


def _proj_gate_body(x_ref, g_ref, w_ref, wl_ref, wu_ref, b_ref, o_ref, lf_ref, xn_ref):
    j = pl.program_id(1)
    gate_step = pl.num_programs(1) - 1

    @pl.when(j == 0)
    def _():
        xn_ref[...] = _rms(x_ref[...], g_ref[...]).astype(BF16)

    @pl.when(j < gate_step)
    def _():
        o_ref[...] = jnp.dot(xn_ref[...], w_ref[...].astype(BF16), preferred_element_type=F32)

    @pl.when(j == gate_step)
    def _():
        low = jnp.dot(xn_ref[...], wl_ref[...], preferred_element_type=F32)
        z = jnp.dot(low.astype(BF16), wu_ref[...], preferred_element_type=F32) + b_ref[...]
        lf_ref[...] = _log_sigmoid(z) / GLA_GATE_NORM


def _norm_proj_gate(x, gains, li, w, wi, n, w_low, w_up, bias, name):
    m = x.shape[0]
    nb = n // PROJ_TN
    col = lambda j: jnp.minimum(j, nb - 1)
    return pl.pallas_call(
        _proj_gate_body,
        grid=(m // PROJ_TM, nb + 1),
        in_specs=[
            pl.BlockSpec((PROJ_TM, D_MODEL), lambda i, j: (i, 0)),
            pl.BlockSpec((None, 1, D_MODEL), lambda i, j: (li, 0, 0)),
            pl.BlockSpec((None, D_MODEL, PROJ_TN), lambda i, j: (wi, 0, col(j))),
            pl.BlockSpec((D_MODEL, LANES), lambda i, j: (0, 0)),
            pl.BlockSpec((LANES, GLA_DK), lambda i, j: (0, 0)),
            pl.BlockSpec((1, GLA_DK), lambda i, j: (0, 0)),
        ],
        out_specs=[pl.BlockSpec((PROJ_TM, PROJ_TN), lambda i, j: (i, col(j))),
                   pl.BlockSpec((PROJ_TM, GLA_DK), lambda i, j: (i, 0))],
        out_shape=[jax.ShapeDtypeStruct((m, n), F32), jax.ShapeDtypeStruct((m, GLA_DK), F32)],
        scratch_shapes=[pltpu.VMEM((PROJ_TM, D_MODEL), BF16)],
        compiler_params=_params("parallel", "arbitrary", vmem=FFN_VMEM_LIMIT),
        name=name,
    )(x, gains, w, w_low, w_up, bias)


def _lb_body(l_ref, o_ref):
    l = l_ref[...]
    e = jnp.exp(l - jnp.max(l, axis=0, keepdims=True))
    p = e / jnp.sum(e, axis=0, keepdims=True)
    gamma0 = p[0:1]
    gamma = gamma0
    o_ref[0:1, :] = gamma - gamma0
    for i in range(1, DEPTH):
        gamma = gamma + p[i:i + 1]
        o_ref[i:i + 1, :] = gamma - gamma0


def _lower_bounds(logits):
    return pl.pallas_call(
        _lb_body,
        out_shape=jax.ShapeDtypeStruct((DEPTH, D_MODEL), F32),
        name="hgrn_lower_bounds",
    )(logits)


def _chunk_step(q, k, v, lf, s_t, state_is_kv=False):
    c, kdim = q.shape
    rows = lax.broadcasted_iota(jnp.int32, (c, kdim), 0)
    b = lf
    d = 1
    while d < c:
        b = b + jnp.where(rows >= d, pltpu.roll(b, d, axis=0), 0.0)
        d *= 2
    b_last = b[c - 1:c, :]

    state_contract = 0 if state_is_kv else 1
    o = lax.dot_general((q * jnp.exp(b)).astype(BF16), s_t.astype(BF16),
                        (((1,), (state_contract,)), ((), ())), preferred_element_type=F32)

    rows_c = lax.broadcasted_iota(jnp.int32, (c, 1), 0)
    for s in range(c):
        w = jnp.exp(b - b[s:s + 1, :]) * (q * k[s:s + 1, :])
        col = jnp.sum(w, axis=-1, keepdims=True)
        col = jnp.where(rows_c >= s, col, 0.0)
        o = o + col * v[s:s + 1, :]

    kd = (k * jnp.exp(b_last - b)).astype(BF16)
    contract_rows = (((0,), (0,)), ((), ()))
    if state_is_kv:
        upd = lax.dot_general(kd, v.astype(BF16), contract_rows, preferred_element_type=F32)
        decay = jnp.broadcast_to(jnp.exp(b_last), (LANES, kdim)).T
        return o, s_t * jnp.tile(decay, (1, v.shape[1] // LANES)) + upd
    upd = lax.dot_general(v.astype(BF16), kd, contract_rows, preferred_element_type=F32)
    return o, s_t * jnp.exp(b_last) + upd


def _hgrn_gates(q_raw, f_raw, lb):
    q = _silu(q_raw) * (HG_FORGET_DIM ** -0.5)
    gate = jnp.log1p(-lb) + _log_sigmoid(f_raw)
    log_lb = jnp.log(lb)
    lf = jnp.maximum(log_lb, gate) + _log1p_exp_neg_abs(log_lb - gate)
    k = (1.0 - lb) * jax.nn.sigmoid(-f_raw)
    return q, k, lf


def _seg_cumsum(x, c):
    r = lax.broadcasted_iota(jnp.int32, x.shape, 0) & (c - 1)
    d = 1
    while d < c:
        x = x + jnp.where(r >= d, pltpu.roll(x, d, axis=0), 0.0)
        d *= 2
    return x


def _rec_prompt_body(kind, hb, kdim, vdim, c, q_ref, k_ref, v_ref, aux_ref, o_ref, st_ref,
                     s_ref, sn_ref, qs_ref, ks_ref, ls_ref):
    t = pl.program_id(2)
    tt = REC_TT
    nseg = tt // c

    @pl.when(t == 0)
    def _():
        s_ref[...] = jnp.zeros_like(s_ref)

    ri = lax.broadcasted_iota(jnp.int32, (tt, tt), 0)
    ci = lax.broadcasted_iota(jnp.int32, (tt, tt), 1)
    shift = c.bit_length() - 1
    mask = ((ri >> shift) == (ci >> shift)) & (ci <= ri)
    segs = [slice(j * c, (j + 1) * c) for j in range(nseg)]
    contract_last = (((1,), (1,)), ((), ()))
    contract_rows = (((0,), (0,)), ((), ()))

    worst = jnp.zeros((tt, kdim), F32)
    for h in range(hb):
        ksl = slice(h * kdim, (h + 1) * kdim)
        vsl = slice(h * vdim, (h + 1) * vdim)
        if kind == "hgrn":
            q, k, lf = _hgrn_gates(q_ref[:, ksl], k_ref[:, ksl], aux_ref[:, ksl])
        else:
            q = q_ref[:, ksl] * (GLA_HEAD_K ** -0.5)
            k = k_ref[:, ksl]
            lf = aux_ref[:, ksl]
        b = _seg_cumsum(lf, c)
        qs_ref[h] = q
        ks_ref[h] = k
        ls_ref[h] = lf
        worst = jnp.maximum(worst, -b)
        if kind != "hgrn":
            worst = jnp.maximum(worst, jnp.abs(k) * (MXU_EXP_RANGE / MXU_KEY_RANGE))

        bl3 = b.reshape(nseg, c, kdim)[:, c - 1:c, :]
        e = jnp.exp(bl3)
        qt = (q * jnp.exp(b)).astype(BF16)
        k_grown = k * jnp.exp(-b)
        kt = k_grown.astype(BF16)
        kd = (k_grown.reshape(nseg, c, kdim) * e).reshape(tt, kdim).astype(BF16)
        vb = v_ref[:, vsl].astype(BF16)
        a = lax.dot_general(qt, kt, contract_last, preferred_element_type=F32)
        att = jnp.where(mask, a, 0.0).astype(BF16)
        o = jnp.dot(att, vb, preferred_element_type=F32)
        upd = [lax.dot_general(vb[rs], kd[rs], contract_rows, preferred_element_type=F32) for rs in segs]
        s_t = s_ref[h]
        states = []
        for j in range(nseg):
            states.append(s_t.astype(BF16))
            s_t = s_t * e[j] + upd[j]
        sn_ref[h] = s_t
        for j, rs in enumerate(segs):
            o_ref[rs, vsl] = o[rs] + lax.dot_general(qt[rs], states[j], contract_last,
                                                     preferred_element_type=F32)

    in_range = jnp.max(worst) <= MXU_EXP_RANGE

    @pl.when(jnp.logical_not(in_range))
    def _():
        for h in range(hb):
            vsl = slice(h * vdim, (h + 1) * vdim)
            sn_ref[h] = s_ref[h]

            def sub(j, carry):
                rs = pl.ds(pl.multiple_of(j * REC_C, REC_C), REC_C)
                o, s_new = _chunk_step(qs_ref[h, rs, :], ks_ref[h, rs, :], v_ref[rs, vsl],
                                       ls_ref[h, rs, :], sn_ref[h])
                sn_ref[h] = s_new
                o_ref[rs, vsl] = o
                return carry

            lax.fori_loop(0, tt // REC_C, sub, 0)

    s_ref[...] = sn_ref[...]

    @pl.when(t == pl.num_programs(2) - 1)
    def _():
        for h in range(hb):
            st_ref[h] = sn_ref[h].T


def _rec_prompt(kind, proj, aux, heads, hb, kdim, vdim, c, q_blk, k_blk, v_blk):
    nt = SEQ // REC_TT
    row = lambda b, g, t: b * nt + t
    if kind == "hgrn":
        aux_spec = pl.BlockSpec((1, hb * kdim), lambda b, g, t: (0, g))
    else:
        aux_spec = pl.BlockSpec((REC_TT, hb * kdim), lambda b, g, t: (row(b, g, t), g))
    return pl.pallas_call(
        functools.partial(_rec_prompt_body, kind, hb, kdim, vdim, c),
        grid=(BATCH, heads // hb, nt),
        in_specs=[
            pl.BlockSpec((REC_TT, hb * kdim), lambda b, g, t: (row(b, g, t), q_blk + g)),
            pl.BlockSpec((REC_TT, hb * kdim), lambda b, g, t: (row(b, g, t), k_blk + g)),
            pl.BlockSpec((REC_TT, hb * vdim), lambda b, g, t: (row(b, g, t), v_blk + g)),
            aux_spec,
        ],
        out_specs=[
            pl.BlockSpec((REC_TT, hb * vdim), lambda b, g, t: (row(b, g, t), g)),
            pl.BlockSpec((None, hb, kdim, vdim), lambda b, g, t: (b, g, 0, 0)),
        ],
        out_shape=[
            jax.ShapeDtypeStruct((N_PROMPT, heads * vdim), F32),
            jax.ShapeDtypeStruct((BATCH, heads, kdim, vdim), F32),
        ],
        scratch_shapes=[pltpu.VMEM((hb, vdim, kdim), F32)] * 2
        + [pltpu.VMEM((hb, REC_TT, kdim), F32)] * 3,
        compiler_params=_params("parallel", "parallel", "arbitrary"),
        name=kind + "_rec_prompt",
    )(proj, proj, proj, aux)


def _rec_sample_body(kind, heads, slot, q_ref, k_ref, v_ref, aux_ref, st_ref, *rest):
    o_ref, nst_ref = rest[-2:]

    for other in range(nst_ref.shape[0]):
        if other != slot:
            nst_ref[other] = jnp.zeros(nst_ref.shape[1:], F32)

    def item(i, carry):
        s = i // heads
        h = i % heads
        if kind == "hgrn":
            q, k, lf = _hgrn_gates(q_ref[s, h], k_ref[s, h], aux_ref[h])
        else:
            q = q_ref[s, h] * (GLA_HEAD_K ** -0.5)
            k = k_ref[s, h]
            lf = aux_ref[s, h]
        o, s_new = _chunk_step(q, k, v_ref[s, h], lf, st_ref[s, h], state_is_kv=True)
        o_ref[s, h] = o
        nst_ref[slot, s, h] = s_new
        return carry

    lax.fori_loop(0, SAMPLE_BS * heads, item, 0, unroll=16 if kind == "hgrn" else 4)


def _rec_sample(kind, q, k, v, aux, states, j, new_states, heads, kdim, vdim):
    blk = lambda last2: pl.BlockSpec((SAMPLE_BS, heads) + last2, lambda i: (i, 0, 0, 0))
    st_spec = pl.BlockSpec((None, SAMPLE_BS, heads, kdim, vdim), lambda i: (j, i, 0, 0, 0))
    if kind == "hgrn":
        aux_spec = pl.BlockSpec((heads, 1, kdim), lambda i: (0, 0, 0))
    else:
        aux_spec = blk((DEC_SEQ, kdim))
    in_specs = [blk((DEC_SEQ, kdim)), blk((DEC_SEQ, kdim)), blk((DEC_SEQ, vdim)), aux_spec, st_spec]
    args = [q, k, v, aux, states]
    if new_states is None:
        aliases, slot = {}, j
        new_spec = pl.BlockSpec((states.shape[0], SAMPLE_BS, heads, kdim, vdim), lambda i: (0, i, 0, 0, 0))
    else:
        in_specs.append(pl.BlockSpec(memory_space=pl.ANY))
        args.append(new_states)
        aliases, slot = {len(args) - 1: 1}, 0
        new_spec = pl.BlockSpec((1, SAMPLE_BS, heads, kdim, vdim), lambda i: (j, i, 0, 0, 0))
    return pl.pallas_call(
        functools.partial(_rec_sample_body, kind, heads, slot),
        grid=(DEC_BATCH // SAMPLE_BS,),
        in_specs=in_specs,
        out_specs=[blk((DEC_SEQ, vdim)), new_spec],
        out_shape=[
            jax.ShapeDtypeStruct((DEC_BATCH, heads, DEC_SEQ, vdim), F32),
            jax.ShapeDtypeStruct(states.shape, F32),
        ],
        input_output_aliases=aliases,
        compiler_params=_params("parallel"),
        name=kind + "_rec_sample",
    )(*args)


def _sample_heads(proj, col0, heads, hdim):
    a = proj[N_PROMPT:, col0:col0 + heads * hdim]
    return a.reshape(DEC_BATCH, DEC_SEQ, heads, hdim).transpose(0, 2, 1, 3)


def _unsplit_heads(o_sample):
    return o_sample.transpose(0, 2, 1, 3).reshape(N_SAMPLE, D_MODEL)


def _prompt_tile(i, tile):
    return jnp.minimum(i, N_PROMPT // tile - 1)


def _sample_tile(i, tile):
    return jnp.maximum(i - N_PROMPT // tile, 0)


def _mix_out_body(norm_dim, op_ref, os_ref, gate_ref, gain_ref, x_ref, w_ref, y_ref):
    def run(o_ref):
        half = OUT_TM // 2
        for r0 in (0, half):
            rs = slice(r0, r0 + half)
            parts = []
            for c0 in range(0, D_MODEL, norm_dim):
                sl = slice(c0, c0 + norm_dim)
                on = _rms(o_ref[rs, sl], gain_ref[:, sl])
                parts.append((on * _silu(gate_ref[rs, sl])).astype(BF16))
            lhs = parts[0] if len(parts) == 1 else jnp.concatenate(parts, axis=1)
            y_ref[rs, :] = x_ref[rs, :] + jnp.dot(lhs, w_ref[...], preferred_element_type=F32)

    is_prompt = pl.program_id(0) < N_PROMPT // OUT_TM
    pl.when(is_prompt)(lambda: run(op_ref))
    pl.when(jnp.logical_not(is_prompt))(lambda: run(os_ref))


def _mix_out(o_prompt, o_sample, proj, gate_blk, gain, x, w_out, wi, norm_dim, name):
    m = x.shape[0]
    row = lambda i: (i, 0)
    return pl.pallas_call(
        functools.partial(_mix_out_body, norm_dim),
        grid=(m // OUT_TM,),
        in_specs=[
            pl.BlockSpec((OUT_TM, D_MODEL), lambda i: (_prompt_tile(i, OUT_TM), 0)),
            pl.BlockSpec((OUT_TM, D_MODEL), lambda i: (_sample_tile(i, OUT_TM), 0)),
            pl.BlockSpec((OUT_TM, D_MODEL), lambda i: (i, gate_blk)),
            pl.BlockSpec((1, D_MODEL), lambda i: (0, 0)),
            pl.BlockSpec((OUT_TM, D_MODEL), row),
            pl.BlockSpec((None, D_MODEL, D_MODEL), lambda i: (wi, 0, 0),
                         pipeline_mode=pl.Buffered(1)),
        ],
        out_specs=pl.BlockSpec((OUT_TM, D_MODEL), row),
        out_shape=jax.ShapeDtypeStruct((m, D_MODEL), F32),
        compiler_params=_params("arbitrary"),
        name=name,
    )(o_prompt, o_sample, proj, gain, x, w_out)


def _pool_body(ns, tt, n_prev, x_ref, g_ref, buf_ref, w_ref, sc_ref, y_ref, nb_ref, xe_ref):
    t = pl.program_id(1)
    pad = POOL_BUF + 1

    @pl.when(t == 0)
    def _():
        xe_ref[:, 0:1, :] = jnp.zeros((ns, 1, D_MODEL), F32)
        xe_ref[:, 1:pad, :] = buf_ref[...]

    @pl.when(t > 0)
    def _():
        xe_ref[:, 0:pad, :] = xe_ref[:, tt:tt + pad, :]

    x = x_ref[...]
    xn = _rms(x, g_ref[...])
    xe_ref[:, pad:pad + tt, :] = xn.reshape(ns, tt, D_MODEL)
    pos = t * tt + lax.broadcasted_iota(jnp.int32, (1, tt, 1), 1)
    for gi, w in enumerate(POOL_WINDOWS):
        sl = slice(gi * POOL_GROUP_DIM, (gi + 1) * POOL_GROUP_DIM)
        cur = xn[:, sl].reshape(ns, tt, POOL_GROUP_DIM)
        win = cur
        for d in range(1, w):
            win = win + xe_ref[:, pad - d:pad - d + tt, sl]
        cnt = jnp.minimum(w, pos + 1 + n_prev).astype(F32)
        y = (win / cnt - cur).reshape(ns * tt, POOL_GROUP_DIM).astype(BF16)
        h = jnp.dot(y, w_ref[gi], preferred_element_type=F32) * sc_ref[:, sl]
        y_ref[:, sl] = x[:, sl] + h

    @pl.when(t == pl.num_programs(1) - 1)
    def _():
        nb_ref[...] = xe_ref[:, tt + 1:tt + pad, :]


def _pool(x, gains, li, buf, w_group, scale, n_seq, seq_len, ns, tt, row0, n_prev):
    nt = seq_len // tt
    assert ns == 1 or nt == 1
    r0 = row0 // (ns * tt)
    row = lambda b, t: (r0 + b * nt + t, 0)
    return pl.pallas_call(
        functools.partial(_pool_body, ns, tt, n_prev),
        grid=(n_seq // ns, nt),
        in_specs=[
            pl.BlockSpec((ns * tt, D_MODEL), row),
            pl.BlockSpec((None, 1, D_MODEL), lambda b, t: (li, 0, 0)),
            pl.BlockSpec((ns, POOL_BUF, D_MODEL), lambda b, t: (b, 0, 0)),
            pl.BlockSpec((len(POOL_WINDOWS), POOL_GROUP_DIM, POOL_GROUP_DIM), lambda b, t: (0, 0, 0)),
            pl.BlockSpec((1, D_MODEL), lambda b, t: (0, 0)),
        ],
        out_specs=[
            pl.BlockSpec((ns * tt, D_MODEL), row),
            pl.BlockSpec((ns, POOL_BUF, D_MODEL), lambda b, t: (b, 0, 0)),
        ],
        out_shape=[
            jax.ShapeDtypeStruct(x.shape, F32),
            jax.ShapeDtypeStruct((n_seq, POOL_BUF, D_MODEL), F32),
        ],
        scratch_shapes=[pltpu.VMEM((ns, tt + POOL_BUF + 1, D_MODEL), F32)],
        input_output_aliases={0: 0},
        compiler_params=_params("parallel", "arbitrary"),
        name="pool_mixer",
    )(x, gains, buf, w_group, scale)


GROUP_TM = 1024


def _final_norm_body(x_ref, g_ref, p_ref, s_ref):
    is_prompt = pl.program_id(0) < N_PROMPT // GROUP_TM

    @pl.when(is_prompt)
    def _():
        p_ref[...] = _rms(x_ref[...], g_ref[...])

    @pl.when(jnp.logical_not(is_prompt))
    def _():
        s_ref[...] = _rms(x_ref[...], g_ref[...])


def _final_norm(x, gain):
    return pl.pallas_call(
        _final_norm_body,
        grid=(N_TOKENS // GROUP_TM,),
        in_specs=[pl.BlockSpec((GROUP_TM, D_MODEL), lambda i: (i, 0)),
                  pl.BlockSpec((1, D_MODEL), lambda i: (0, 0))],
        out_specs=[pl.BlockSpec((GROUP_TM, D_MODEL), lambda i: (_prompt_tile(i, GROUP_TM), 0)),
                   pl.BlockSpec((GROUP_TM, D_MODEL), lambda i: (_sample_tile(i, GROUP_TM), 0))],
        out_shape=[jax.ShapeDtypeStruct((N_PROMPT, D_MODEL), F32),
                   jax.ShapeDtypeStruct((N_SAMPLE, D_MODEL), F32)],
        compiler_params=_params("arbitrary"),
        name="final_norm",
    )(x, gain)


def kernel(x_prompt, x_sample, state_hgrn, state_gla, state_pool, norm_ffn1, ffn1_w_gate, ffn1_w_up, ffn1_w_down, norm_mix, norm_ffn2, ffn2_w_gate, ffn2_w_up, ffn2_w_down, hgrn_lb_logits, hgrn_w_in, hgrn_o_norm, hgrn_w_out, gla_w_in, gla_w_gate_up, gla_b_gate, gla_o_norm, gla_w_out, pool_w_group, pool_scale, final_norm):
    x_parts = (x_prompt.reshape(N_PROMPT, D_MODEL), x_sample.reshape(N_SAMPLE, D_MODEL))
    gains = lambda a: a.reshape(a.shape[0], 1, D_MODEL)
    norm_ffn1, norm_mix, norm_ffn2 = gains(norm_ffn1), gains(norm_mix), gains(norm_ffn2)
    ffn1 = (ffn1_w_gate, ffn1_w_up, ffn1_w_down)
    ffn2 = (ffn2_w_gate, ffn2_w_up, ffn2_w_down)
    hgrn_w_out_b = hgrn_w_out.astype(BF16)
    gla_w_out_b = gla_w_out.astype(BF16)
    lb = _lower_bounds(hgrn_lb_logits)

    new_hgrn_p, new_gla_p, new_pool_p, new_pool_s = [], [], [], []
    new_hgrn_s = new_gla_s = None
    for li in range(DEPTH):
        j = li // N_MIXERS
        kind = li % N_MIXERS
        x = _ffn(x_parts if li == 0 else (x,), norm_ffn1, *ffn1, li)
        if kind == 0:
            proj = _norm_proj(x, norm_mix, li, hgrn_w_in, j, 4 * D_MODEL, "hgrn_in_proj")
            lb_l = lb[li:li + 1]
            groups = HG_HEADS // HGRN_HB
            o_p, st_p = _rec_prompt("hgrn", proj, lb_l, HG_HEADS, HGRN_HB, HG_FORGET_DIM, HG_HEAD_V,
                                    HGRN_SEG[j], 0, groups, 2 * groups)
            o_s, new_hgrn_s = _rec_sample(
                "hgrn",
                _sample_heads(proj, 0, HG_HEADS, HG_FORGET_DIM),
                _sample_heads(proj, D_MODEL, HG_HEADS, HG_FORGET_DIM),
                _sample_heads(proj, 2 * D_MODEL, HG_HEADS, HG_HEAD_V),
                lb_l.reshape(HG_HEADS, 1, HG_FORGET_DIM), state_hgrn, j, new_hgrn_s,
                HG_HEADS, HG_FORGET_DIM, HG_HEAD_V)
            new_hgrn_p.append(st_p)
            x = _mix_out(o_p, _unsplit_heads(o_s), proj, 3, hgrn_o_norm[j].reshape(1, D_MODEL), x,
                         hgrn_w_out_b, j, D_MODEL, "hgrn_out")
        elif kind == 1:
            n_main = 2 * GLA_DK + 2 * GLA_DV
            w_low = jnp.pad(gla_w_in[j, :, n_main:], ((0, 0), (0, LANES - GLA_GATE_RANK))).astype(BF16)
            w_up = jnp.pad(gla_w_gate_up[j], ((0, LANES - GLA_GATE_RANK), (0, 0))).astype(BF16)
            proj, lf = _norm_proj_gate(x, norm_mix, li, gla_w_in, j, n_main, w_low, w_up,
                                       gla_b_gate[j].reshape(1, GLA_DK), "gla_in_proj")
            kb = GLA_DK // (GLA_HB * GLA_HEAD_K)
            vb = 2 * GLA_DK // (GLA_HB * GLA_HEAD_V)
            o_p, st_p = _rec_prompt("gla", proj, lf, GLA_HEADS, GLA_HB, GLA_HEAD_K, GLA_HEAD_V,
                                    GLA_SEG, 0, kb, vb)
            o_s, new_gla_s = _rec_sample(
                "gla",
                _sample_heads(proj, 0, GLA_HEADS, GLA_HEAD_K),
                _sample_heads(proj, GLA_DK, GLA_HEADS, GLA_HEAD_K),
                _sample_heads(proj, 2 * GLA_DK, GLA_HEADS, GLA_HEAD_V),
                _sample_heads(lf, 0, GLA_HEADS, GLA_HEAD_K), state_gla, j, new_gla_s,
                GLA_HEADS, GLA_HEAD_K, GLA_HEAD_V)
            new_gla_p.append(st_p)
            x = _mix_out(o_p, _unsplit_heads(o_s), proj, (2 * GLA_DK + GLA_DV) // D_MODEL,
                         gla_o_norm[j].reshape(1, GLA_DV), x, gla_w_out_b, j,
                         GLA_HEAD_V, "gla_out")
        else:
            w_group = pool_w_group[j].astype(BF16)
            scale = pool_scale[j].reshape(1, D_MODEL)
            x, nb_p = _pool(x, norm_mix, li, jnp.zeros((BATCH, POOL_BUF, D_MODEL), F32), w_group, scale,
                            BATCH, SEQ, 1, POOL_TT, 0, 0)
            x, nb_s = _pool(x, norm_mix, li, state_pool[j], w_group, scale,
                            DEC_BATCH, DEC_SEQ, POOL_SAMPLE_NS, DEC_SEQ, N_PROMPT, min(POOL_BUF, PAST_LEN))
            new_pool_p.append(nb_p)
            new_pool_s.append(nb_s)
        x = _ffn((x,), norm_ffn2, *ffn2, li)

    y_p, y_s = _final_norm(x, final_norm.reshape(1, D_MODEL))
    return (y_p.reshape(BATCH, SEQ, D_MODEL), y_s.reshape(DEC_BATCH, DEC_SEQ, D_MODEL),
            jnp.stack(new_hgrn_p), jnp.stack(new_gla_p), jnp.stack(new_pool_p),
            new_hgrn_s, new_gla_s, jnp.stack(new_pool_s))
```

```python
import functools

import jax
import jax.numpy as jnp
from jax import lax
from jax.experimental import pallas as pl
from jax.experimental.pallas import tpu as pltpu

F32 = jnp.float32
BF16 = jnp.bfloat16

D_MODEL = 2048
BATCH = 4
SEQ = 2048
DEPTH = 4
DEC_BATCH = 128
DEC_SEQ = 8
PAST_LEN = 16384
N_MIXERS = 3

HG_FORGET_DIM = 128
HG_HEADS = D_MODEL // HG_FORGET_DIM
HG_HEAD_V = D_MODEL // HG_HEADS

GLA_HEADS = 4
GLA_DK = D_MODEL // 2
GLA_DV = D_MODEL
GLA_HEAD_K = GLA_DK // GLA_HEADS
GLA_HEAD_V = GLA_DV // GLA_HEADS
GLA_GATE_RANK = 16
GLA_GATE_NORM = 16.0

POOL_WINDOWS = (2, 4, 8, 16)
POOL_GROUP_DIM = D_MODEL // len(POOL_WINDOWS)
POOL_BUF = max(POOL_WINDOWS) - 1

D_FF = 5632
EPS = 1e-6

N_PROMPT = BATCH * SEQ
N_SAMPLE = DEC_BATCH * DEC_SEQ
N_TOKENS = N_PROMPT + N_SAMPLE

LANES = 128
VMEM_LIMIT = 56 * 1024 * 1024
FFN_VMEM_LIMIT = 61 * 1024 * 1024

FFN_TM, FFN_TF = 1024, 512
FFN_SUB = 256
FFN_X_CHUNKS = 8
FFN_NORM_ROWS = 16
PROJ_TM, PROJ_TN = 1024, 1024
OUT_TM = 512
REC_TT = 256
REC_C = 16
HGRN_HB = 16
HGRN_SEG = (32, 128)
GLA_HB, GLA_SEG = 4, 128
MXU_EXP_RANGE = 60.0
MXU_KEY_RANGE = 1e8
SAMPLE_BS = 4
POOL_TT = 256
POOL_SAMPLE_NS = 16


def _params(*sem, vmem=VMEM_LIMIT):
    return pltpu.CompilerParams(dimension_semantics=sem, vmem_limit_bytes=vmem)


def _rms(xf, g):
    return xf * lax.rsqrt(jnp.mean(xf * xf, axis=-1, keepdims=True) + EPS) * g


def _silu(a):
    return a * jax.nn.sigmoid(a)


def _log1p_exp_neg_abs(a):
    return jnp.log(1.0 + jnp.exp(-jnp.abs(a)))


def _log_sigmoid(a):
    return jnp.minimum(a, 0.0) - _log1p_exp_neg_abs(a)


def _ffn_body(src_tiles, *refs):
    x_srcs = refs[:len(src_tiles)]
    g_ref, wg_ref, wu_ref, wd_ref, o_ref, xs_ref, xn_ref, sems = refs[len(src_tiles):]
    i = pl.program_id(0)
    f = pl.program_id(1)
    rows = FFN_TM // FFN_X_CHUNKS

    def chunk(tile, c, action):
        first = 0
        for src, n in zip(x_srcs, src_tiles):
            def go(src=src, first=first):
                copy = pltpu.make_async_copy(
                    src.at[pl.ds((tile - first) * FFN_TM + c * rows, rows), :],
                    xs_ref.at[pl.ds(c * rows, rows), :], sems.at[c])
                copy.start() if action == "start" else copy.wait()

            pl.when((tile >= first) & (tile < first + n))(go)
            first += n

    @pl.when((i == 0) & (f == 0))
    def _():
        for c in range(FFN_X_CHUNKS):
            chunk(i, c, "start")

    @pl.when(f == 0)
    def _():
        for c in range(FFN_X_CHUNKS):
            chunk(i, c, "wait")
        for r0 in range(0, FFN_TM, FFN_NORM_ROWS):
            rs = slice(r0, r0 + FFN_NORM_ROWS)
            x = xs_ref[rs, :]
            o_ref[rs, :] = x
            inv = lax.rsqrt(jnp.mean(x * x, axis=-1, keepdims=True) + EPS)
            xn_ref[rs, :] = (o_ref[rs, :] * inv * g_ref[...]).astype(BF16)

    for c in range(FFN_X_CHUNKS):
        @pl.when(f == c + 1)
        def _():
            chunk(i + 1, c, "start")

    xn = xn_ref[...]
    for c0 in range(0, FFN_TF, FFN_SUB):
        cs = slice(c0, c0 + FFN_SUB)
        a = jnp.dot(xn, wg_ref[:, cs].astype(BF16), preferred_element_type=F32)
        u = jnp.dot(xn, wu_ref[:, cs].astype(BF16), preferred_element_type=F32)
        h = (0.5 * _silu(a) * u).astype(BF16)
        o_ref[...] += jnp.dot(h, wd_ref[cs, :].astype(BF16), preferred_element_type=F32)


def _ffn(x_parts, gains, w_gate, w_up, w_down, li):
    src_tiles = tuple(p.shape[0] // FFN_TM for p in x_parts)
    assert all(p.shape[0] % FFN_TM == 0 for p in x_parts)
    assert D_FF // FFN_TF > FFN_X_CHUNKS
    m = sum(p.shape[0] for p in x_parts)
    return pl.pallas_call(
        functools.partial(_ffn_body, src_tiles),
        grid=(m // FFN_TM, D_FF // FFN_TF),
        in_specs=[pl.BlockSpec(memory_space=pl.ANY)] * len(x_parts) + [
            pl.BlockSpec((None, 1, D_MODEL), lambda i, f: (li, 0, 0)),
            pl.BlockSpec((None, D_MODEL, FFN_TF), lambda i, f: (li, 0, f)),
            pl.BlockSpec((None, D_MODEL, FFN_TF), lambda i, f: (li, 0, f)),
            pl.BlockSpec((None, FFN_TF, D_MODEL), lambda i, f: (li, f, 0)),
        ],
        out_specs=pl.BlockSpec((FFN_TM, D_MODEL), lambda i, f: (i, 0)),
        out_shape=jax.ShapeDtypeStruct((m, D_MODEL), F32),
        scratch_shapes=[pltpu.VMEM((FFN_TM, D_MODEL), F32),
                        pltpu.VMEM((FFN_TM, D_MODEL), BF16),
                        pltpu.SemaphoreType.DMA((FFN_X_CHUNKS,))],
        compiler_params=_params("arbitrary", "arbitrary", vmem=FFN_VMEM_LIMIT),
        name="ffn",
    )(*x_parts, gains, w_gate, w_up, w_down)


def _proj_body(x_ref, g_ref, w_ref, o_ref, xn_ref):
    @pl.when(pl.program_id(1) == 0)
    def _():
        xn_ref[...] = _rms(x_ref[...], g_ref[...]).astype(BF16)

    o_ref[...] = jnp.dot(xn_ref[...], w_ref[...].astype(BF16), preferred_element_type=F32)


def _norm_proj(x, gains, li, w, wi, n, name):
    m = x.shape[0]
    return pl.pallas_call(
        _proj_body,
        grid=(m // PROJ_TM, n // PROJ_TN),
        in_specs=[
            pl.BlockSpec((PROJ_TM, D_MODEL), lambda i, j: (i, 0)),
            pl.BlockSpec((None, 1, D_MODEL), lambda i, j: (li, 0, 0)),
            pl.BlockSpec((None, D_MODEL, PROJ_TN), lambda i, j: (wi, 0, j)),
        ],
        out_specs=pl.BlockSpec((PROJ_TM, PROJ_TN), lambda i, j: (i, j)),
        out_shape=jax.ShapeDtypeStruct((m, n), F32),
        scratch_shapes=[pltpu.VMEM((PROJ_TM, D_MODEL), BF16)],
        compiler_params=_params("parallel", "arbitrary"),
        name=name,
    )(x, gains, w)


def _proj_gate_body(x_ref, g_ref, w_ref, wl_ref, wu_ref, b_ref, o_ref, lf_ref, xn_ref):
    j = pl.program_id(1)
    gate_step = pl.num_programs(1) - 1

    @pl.when(j == 0)
    def _():
        xn_ref[...] = _rms(x_ref[...], g_ref[...]).astype(BF16)

    @pl.when(j < gate_step)
    def _():
        o_ref[...] = jnp.dot(xn_ref[...], w_ref[...].astype(BF16), preferred_element_type=F32)

    @pl.when(j == gate_step)
    def _():
        low = jnp.dot(xn_ref[...], wl_ref[...], preferred_element_type=F32)
        z = jnp.dot(low.astype(BF16), wu_ref[...], preferred_element_type=F32) + b_ref[...]
        lf_ref[...] = _log_sigmoid(z) / GLA_GATE_NORM


def _norm_proj_gate(x, gains, li, w, wi, n, w_low, w_up, bias, name):
    m = x.shape[0]
    nb = n // PROJ_TN
    col = lambda j: jnp.minimum(j, nb - 1)
    return pl.pallas_call(
        _proj_gate_body,
        grid=(m // PROJ_TM, nb + 1),
        in_specs=[
            pl.BlockSpec((PROJ_TM, D_MODEL), lambda i, j: (i, 0)),
            pl.BlockSpec((None, 1, D_MODEL), lambda i, j: (li, 0, 0)),
            pl.BlockSpec((None, D_MODEL, PROJ_TN), lambda i, j: (wi, 0, col(j))),
            pl.BlockSpec((D_MODEL, LANES), lambda i, j: (0, 0)),
            pl.BlockSpec((LANES, GLA_DK), lambda i, j: (0, 0)),
            pl.BlockSpec((1, GLA_DK), lambda i, j: (0, 0)),
        ],
        out_specs=[pl.BlockSpec((PROJ_TM, PROJ_TN), lambda i, j: (i, col(j))),
                   pl.BlockSpec((PROJ_TM, GLA_DK), lambda i, j: (i, 0))],
        out_shape=[jax.ShapeDtypeStruct((m, n), F32), jax.ShapeDtypeStruct((m, GLA_DK), F32)],
        scratch_shapes=[pltpu.VMEM((PROJ_TM, D_MODEL), BF16)],
        compiler_params=_params("parallel", "arbitrary", vmem=FFN_VMEM_LIMIT),
        name=name,
    )(x, gains, w, w_low, w_up, bias)


def _lb_body(l_ref, o_ref):
    l = l_ref[...]
    e = jnp.exp(l - jnp.max(l, axis=0, keepdims=True))
    p = e / jnp.sum(e, axis=0, keepdims=True)
    gamma0 = p[0:1]
    gamma = gamma0
    o_ref[0:1, :] = gamma - gamma0
    for i in range(1, DEPTH):
        gamma = gamma + p[i:i + 1]
        o_ref[i:i + 1, :] = gamma - gamma0


def _lower_bounds(logits):
    return pl.pallas_call(
        _lb_body,
        out_shape=jax.ShapeDtypeStruct((DEPTH, D_MODEL), F32),
        name="hgrn_lower_bounds",
    )(logits)


def _chunk_step(q, k, v, lf, s_t, state_is_kv=False):
    c, kdim = q.shape
    rows = lax.broadcasted_iota(jnp.int32, (c, kdim), 0)
    b = lf
    d = 1
    while d < c:
        b = b + jnp.where(rows >= d, pltpu.roll(b, d, axis=0), 0.0)
        d *= 2
    b_last = b[c - 1:c, :]

    state_contract = 0 if state_is_kv else 1
    o = lax.dot_general((q * jnp.exp(b)).astype(BF16), s_t.astype(BF16),
                        (((1,), (state_contract,)), ((), ())), preferred_element_type=F32)

    rows_c = lax.broadcasted_iota(jnp.int32, (c, 1), 0)
    for s in range(c):
        w = jnp.exp(b - b[s:s + 1, :]) * (q * k[s:s + 1, :])
        col = jnp.sum(w, axis=-1, keepdims=True)
        col = jnp.where(rows_c >= s, col, 0.0)
        o = o + col * v[s:s + 1, :]

    kd = (k * jnp.exp(b_last - b)).astype(BF16)
    contract_rows = (((0,), (0,)), ((), ()))
    if state_is_kv:
        upd = lax.dot_general(kd, v.astype(BF16), contract_rows, preferred_element_type=F32)
        decay = jnp.broadcast_to(jnp.exp(b_last), (LANES, kdim)).T
        return o, s_t * jnp.tile(decay, (1, v.shape[1] // LANES)) + upd
    upd = lax.dot_general(v.astype(BF16), kd, contract_rows, preferred_element_type=F32)
    return o, s_t * jnp.exp(b_last) + upd


def _hgrn_gates(q_raw, f_raw, lb):
    q = _silu(q_raw) * (HG_FORGET_DIM ** -0.5)
    gate = jnp.log1p(-lb) + _log_sigmoid(f_raw)
    log_lb = jnp.log(lb)
    lf = jnp.maximum(log_lb, gate) + _log1p_exp_neg_abs(log_lb - gate)
    k = (1.0 - lb) * jax.nn.sigmoid(-f_raw)
    return q, k, lf


def _seg_cumsum(x, c):
    r = lax.broadcasted_iota(jnp.int32, x.shape, 0) & (c - 1)
    d = 1
    while d < c:
        x = x + jnp.where(r >= d, pltpu.roll(x, d, axis=0), 0.0)
        d *= 2
    return x


def _rec_prompt_body(kind, hb, kdim, vdim, c, q_ref, k_ref, v_ref, aux_ref, o_ref, st_ref,
                     s_ref, sn_ref, qs_ref, ks_ref, ls_ref):
    t = pl.program_id(2)
    tt = REC_TT
    nseg = tt // c

    @pl.when(t == 0)
    def _():
        s_ref[...] = jnp.zeros_like(s_ref)

    ri = lax.broadcasted_iota(jnp.int32, (tt, tt), 0)
    ci = lax.broadcasted_iota(jnp.int32, (tt, tt), 1)
    shift = c.bit_length() - 1
    mask = ((ri >> shift) == (ci >> shift)) & (ci <= ri)
    segs = [slice(j * c, (j + 1) * c) for j in range(nseg)]
    contract_last = (((1,), (1,)), ((), ()))
    contract_rows = (((0,), (0,)), ((), ()))

    worst = jnp.zeros((tt, kdim), F32)
    for h in range(hb):
        ksl = slice(h * kdim, (h + 1) * kdim)
        vsl = slice(h * vdim, (h + 1) * vdim)
        if kind == "hgrn":
            q, k, lf = _hgrn_gates(q_ref[:, ksl], k_ref[:, ksl], aux_ref[:, ksl])
        else:
            q = q_ref[:, ksl] * (GLA_HEAD_K ** -0.5)
            k = k_ref[:, ksl]
            lf = aux_ref[:, ksl]
        b = _seg_cumsum(lf, c)
        qs_ref[h] = q
        ks_ref[h] = k
        ls_ref[h] = lf
        worst = jnp.maximum(worst, -b)
        if kind != "hgrn":
            worst = jnp.maximum(worst, jnp.abs(k) * (MXU_EXP_RANGE / MXU_KEY_RANGE))

        bl3 = b.reshape(nseg, c, kdim)[:, c - 1:c, :]
        e = jnp.exp(bl3)
        qt = (q * jnp.exp(b)).astype(BF16)
        k_grown = k * jnp.exp(-b)
        kt = k_grown.astype(BF16)
        kd = (k_grown.reshape(nseg, c, kdim) * e).reshape(tt, kdim).astype(BF16)
        vb = v_ref[:, vsl].astype(BF16)
        a = lax.dot_general(qt, kt, contract_last, preferred_element_type=F32)
        att = jnp.where(mask, a, 0.0).astype(BF16)
        o = jnp.dot(att, vb, preferred_element_type=F32)
        upd = [lax.dot_general(vb[rs], kd[rs], contract_rows, preferred_element_type=F32) for rs in segs]
        s_t = s_ref[h]
        states = []
        for j in range(nseg):
            states.append(s_t.astype(BF16))
            s_t = s_t * e[j] + upd[j]
        sn_ref[h] = s_t
        for j, rs in enumerate(segs):
            o_ref[rs, vsl] = o[rs] + lax.dot_general(qt[rs], states[j], contract_last,
                                                     preferred_element_type=F32)

    in_range = jnp.max(worst) <= MXU_EXP_RANGE

    @pl.when(jnp.logical_not(in_range))
    def _():
        for h in range(hb):
            vsl = slice(h * vdim, (h + 1) * vdim)
            sn_ref[h] = s_ref[h]

            def sub(j, carry):
                rs = pl.ds(pl.multiple_of(j * REC_C, REC_C), REC_C)
                o, s_new = _chunk_step(qs_ref[h, rs, :], ks_ref[h, rs, :], v_ref[rs, vsl],
                                       ls_ref[h, rs, :], sn_ref[h])
                sn_ref[h] = s_new
                o_ref[rs, vsl] = o
                return carry

            lax.fori_loop(0, tt // REC_C, sub, 0)

    s_ref[...] = sn_ref[...]

    @pl.when(t == pl.num_programs(2) - 1)
    def _():
        for h in range(hb):
            st_ref[h] = sn_ref[h].T


def _rec_prompt(kind, proj, aux, heads, hb, kdim, vdim, c, q_blk, k_blk, v_blk):
    nt = SEQ // REC_TT
    row = lambda b, g, t: b * nt + t
    if kind == "hgrn":
        aux_spec = pl.BlockSpec((1, hb * kdim), lambda b, g, t: (0, g))
    else:
        aux_spec = pl.BlockSpec((REC_TT, hb * kdim), lambda b, g, t: (row(b, g, t), g))
    return pl.pallas_call(
        functools.partial(_rec_prompt_body, kind, hb, kdim, vdim, c),
        grid=(BATCH, heads // hb, nt),
        in_specs=[
            pl.BlockSpec((REC_TT, hb * kdim), lambda b, g, t: (row(b, g, t), q_blk + g)),
            pl.BlockSpec((REC_TT, hb * kdim), lambda b, g, t: (row(b, g, t), k_blk + g)),
            pl.BlockSpec((REC_TT, hb * vdim), lambda b, g, t: (row(b, g, t), v_blk + g)),
            aux_spec,
        ],
        out_specs=[
            pl.BlockSpec((REC_TT, hb * vdim), lambda b, g, t: (row(b, g, t), g)),
            pl.BlockSpec((None, hb, kdim, vdim), lambda b, g, t: (b, g, 0, 0)),
        ],
        out_shape=[
            jax.ShapeDtypeStruct((N_PROMPT, heads * vdim), F32),
            jax.ShapeDtypeStruct((BATCH, heads, kdim, vdim), F32),
        ],
        scratch_shapes=[pltpu.VMEM((hb, vdim, kdim), F32)] * 2
        + [pltpu.VMEM((hb, REC_TT, kdim), F32)] * 3,
        compiler_params=_params("parallel", "parallel", "arbitrary"),
        name=kind + "_rec_prompt",
    )(proj, proj, proj, aux)


def _rec_sample_body(kind, heads, slot, q_ref, k_ref, v_ref, aux_ref, st_ref, *rest):
    o_ref, nst_ref = rest[-2:]

    for other in range(nst_ref.shape[0]):
        if other != slot:
            nst_ref[other] = jnp.zeros(nst_ref.shape[1:], F32)

    def item(i, carry):
        s = i // heads
        h = i % heads
        if kind == "hgrn":
            q, k, lf = _hgrn_gates(q_ref[s, h], k_ref[s, h], aux_ref[h])
        else:
            q = q_ref[s, h] * (GLA_HEAD_K ** -0.5)
            k = k_ref[s, h]
            lf = aux_ref[s, h]
        o, s_new = _chunk_step(q, k, v_ref[s, h], lf, st_ref[s, h], state_is_kv=True)
        o_ref[s, h] = o
        nst_ref[slot, s, h] = s_new
        return carry

    lax.fori_loop(0, SAMPLE_BS * heads, item, 0, unroll=16 if kind == "hgrn" else 4)


def _rec_sample(kind, q, k, v, aux, states, j, new_states, heads, kdim, vdim):
    blk = lambda last2: pl.BlockSpec((SAMPLE_BS, heads) + last2, lambda i: (i, 0, 0, 0))
    st_spec = pl.BlockSpec((None, SAMPLE_BS, heads, kdim, vdim), lambda i: (j, i, 0, 0, 0))
    if kind == "hgrn":
        aux_spec = pl.BlockSpec((heads, 1, kdim), lambda i: (0, 0, 0))
    else:
        aux_spec = blk((DEC_SEQ, kdim))
    in_specs = [blk((DEC_SEQ, kdim)), blk((DEC_SEQ, kdim)), blk((DEC_SEQ, vdim)), aux_spec, st_spec]
    args = [q, k, v, aux, states]
    if new_states is None:
        aliases, slot = {}, j
        new_spec = pl.BlockSpec((states.shape[0], SAMPLE_BS, heads, kdim, vdim), lambda i: (0, i, 0, 0, 0))
    else:
        in_specs.append(pl.BlockSpec(memory_space=pl.ANY))
        args.append(new_states)
        aliases, slot = {len(args) - 1: 1}, 0
        new_spec = pl.BlockSpec((1, SAMPLE_BS, heads, kdim, vdim), lambda i: (j, i, 0, 0, 0))
    return pl.pallas_call(
        functools.partial(_rec_sample_body, kind, heads, slot),
        grid=(DEC_BATCH // SAMPLE_BS,),
        in_specs=in_specs,
        out_specs=[blk((DEC_SEQ, vdim)), new_spec],
        out_shape=[
            jax.ShapeDtypeStruct((DEC_BATCH, heads, DEC_SEQ, vdim), F32),
            jax.ShapeDtypeStruct(states.shape, F32),
        ],
        input_output_aliases=aliases,
        compiler_params=_params("parallel"),
        name=kind + "_rec_sample",
    )(*args)


def _sample_heads(proj, col0, heads, hdim):
    a = proj[N_PROMPT:, col0:col0 + heads * hdim]
    return a.reshape(DEC_BATCH, DEC_SEQ, heads, hdim).transpose(0, 2, 1, 3)


def _unsplit_heads(o_sample):
    return o_sample.transpose(0, 2, 1, 3).reshape(N_SAMPLE, D_MODEL)


def _prompt_tile(i, tile):
    return jnp.minimum(i, N_PROMPT // tile - 1)


def _sample_tile(i, tile):
    return jnp.maximum(i - N_PROMPT // tile, 0)


def _mix_out_body(norm_dim, op_ref, os_ref, gate_ref, gain_ref, x_ref, w_ref, y_ref):
    def run(o_ref):
        half = OUT_TM // 2
        for r0 in (0, half):
            rs = slice(r0, r0 + half)
            parts = []
            for c0 in range(0, D_MODEL, norm_dim):
                sl = slice(c0, c0 + norm_dim)
                on = _rms(o_ref[rs, sl], gain_ref[:, sl])
                parts.append((on * _silu(gate_ref[rs, sl])).astype(BF16))
            lhs = parts[0] if len(parts) == 1 else jnp.concatenate(parts, axis=1)
            y_ref[rs, :] = x_ref[rs, :] + jnp.dot(lhs, w_ref[...], preferred_element_type=F32)

    is_prompt = pl.program_id(0) < N_PROMPT // OUT_TM
    pl.when(is_prompt)(lambda: run(op_ref))
    pl.when(jnp.logical_not(is_prompt))(lambda: run(os_ref))


def _mix_out(o_prompt, o_sample, proj, gate_blk, gain, x, w_out, wi, norm_dim, name):
    m = x.shape[0]
    row = lambda i: (i, 0)
    return pl.pallas_call(
        functools.partial(_mix_out_body, norm_dim),
        grid=(m // OUT_TM,),
        in_specs=[
            pl.BlockSpec((OUT_TM, D_MODEL), lambda i: (_prompt_tile(i, OUT_TM), 0)),
            pl.BlockSpec((OUT_TM, D_MODEL), lambda i: (_sample_tile(i, OUT_TM), 0)),
            pl.BlockSpec((OUT_TM, D_MODEL), lambda i: (i, gate_blk)),
            pl.BlockSpec((1, D_MODEL), lambda i: (0, 0)),
            pl.BlockSpec((OUT_TM, D_MODEL), row),
            pl.BlockSpec((None, D_MODEL, D_MODEL), lambda i: (wi, 0, 0),
                         pipeline_mode=pl.Buffered(1)),
        ],
        out_specs=pl.BlockSpec((OUT_TM, D_MODEL), row),
        out_shape=jax.ShapeDtypeStruct((m, D_MODEL), F32),
        compiler_params=_params("arbitrary"),
        name=name,
    )(o_prompt, o_sample, proj, gain, x, w_out)


def _pool_body(ns, tt, n_prev, x_ref, g_ref, buf_ref, w_ref, sc_ref, y_ref, nb_ref, xe_ref):
    t = pl.program_id(1)
    pad = POOL_BUF + 1

    @pl.when(t == 0)
    def _():
        xe_ref[:, 0:1, :] = jnp.zeros((ns, 1, D_MODEL), F32)
        xe_ref[:, 1:pad, :] = buf_ref[...]

    @pl.when(t > 0)
    def _():
        xe_ref[:, 0:pad, :] = xe_ref[:, tt:tt + pad, :]

    x = x_ref[...]
    xn = _rms(x, g_ref[...])
    xe_ref[:, pad:pad + tt, :] = xn.reshape(ns, tt, D_MODEL)
    pos = t * tt + lax.broadcasted_iota(jnp.int32, (1, tt, 1), 1)
    for gi, w in enumerate(POOL_WINDOWS):
        sl = slice(gi * POOL_GROUP_DIM, (gi + 1) * POOL_GROUP_DIM)
        cur = xn[:, sl].reshape(ns, tt, POOL_GROUP_DIM)
        win = cur
        for d in range(1, w):
            win = win + xe_ref[:, pad - d:pad - d + tt, sl]
        cnt = jnp.minimum(w, pos + 1 + n_prev).astype(F32)
        y = (win / cnt - cur).reshape(ns * tt, POOL_GROUP_DIM).astype(BF16)
        h = jnp.dot(y, w_ref[gi], preferred_element_type=F32) * sc_ref[:, sl]
        y_ref[:, sl] = x[:, sl] + h

    @pl.when(t == pl.num_programs(1) - 1)
    def _():
        nb_ref[...] = xe_ref[:, tt + 1:tt + pad, :]


def _pool(x, gains, li, buf, w_group, scale, n_seq, seq_len, ns, tt, row0, n_prev):
    nt = seq_len // tt
    assert ns == 1 or nt == 1
    r0 = row0 // (ns * tt)
    row = lambda b, t: (r0 + b * nt + t, 0)
    return pl.pallas_call(
        functools.partial(_pool_body, ns, tt, n_prev),
        grid=(n_seq // ns, nt),
        in_specs=[
            pl.BlockSpec((ns * tt, D_MODEL), row),
            pl.BlockSpec((None, 1, D_MODEL), lambda b, t: (li, 0, 0)),
            pl.BlockSpec((ns, POOL_BUF, D_MODEL), lambda b, t: (b, 0, 0)),
            pl.BlockSpec((len(POOL_WINDOWS), POOL_GROUP_DIM, POOL_GROUP_DIM), lambda b, t: (0, 0, 0)),
            pl.BlockSpec((1, D_MODEL), lambda b, t: (0, 0)),
        ],
        out_specs=[
            pl.BlockSpec((ns * tt, D_MODEL), row),
            pl.BlockSpec((ns, POOL_BUF, D_MODEL), lambda b, t: (b, 0, 0)),
        ],
        out_shape=[
            jax.ShapeDtypeStruct(x.shape, F32),
            jax.ShapeDtypeStruct((n_seq, POOL_BUF, D_MODEL), F32),
        ],
        scratch_shapes=[pltpu.VMEM((ns, tt + POOL_BUF + 1, D_MODEL), F32)],
        input_output_aliases={0: 0},
        compiler_params=_params("parallel", "arbitrary"),
        name="pool_mixer",
    )(x, gains, buf, w_group, scale)


GROUP_TM = 1024


def _final_norm_body(x_ref, g_ref, p_ref, s_ref):
    is_prompt = pl.program_id(0) < N_PROMPT // GROUP_TM

    @pl.when(is_prompt)
    def _():
        p_ref[...] = _rms(x_ref[...], g_ref[...])

    @pl.when(jnp.logical_not(is_prompt))
    def _():
        s_ref[...] = _rms(x_ref[...], g_ref[...])


def _final_norm(x, gain):
    return pl.pallas_call(
        _final_norm_body,
        grid=(N_TOKENS // GROUP_TM,),
        in_specs=[pl.BlockSpec((GROUP_TM, D_MODEL), lambda i: (i, 0)),
                  pl.BlockSpec((1, D_MODEL), lambda i: (0, 0))],
        out_specs=[pl.BlockSpec((GROUP_TM, D_MODEL), lambda i: (_prompt_tile(i, GROUP_TM), 0)),
                   pl.BlockSpec((GROUP_TM, D_MODEL), lambda i: (_sample_tile(i, GROUP_TM), 0))],
        out_shape=[jax.ShapeDtypeStruct((N_PROMPT, D_MODEL), F32),
                   jax.ShapeDtypeStruct((N_SAMPLE, D_MODEL), F32)],
        compiler_params=_params("arbitrary"),
        name="final_norm",
    )(x, gain)


def kernel(x_prompt, x_sample, state_hgrn, state_gla, state_pool, norm_ffn1, ffn1_w_gate, ffn1_w_up, ffn1_w_down, norm_mix, norm_ffn2, ffn2_w_gate, ffn2_w_up, ffn2_w_down, hgrn_lb_logits, hgrn_w_in, hgrn_o_norm, hgrn_w_out, gla_w_in, gla_w_gate_up, gla_b_gate, gla_o_norm, gla_w_out, pool_w_group, pool_scale, final_norm):
    x_parts = (x_prompt.reshape(N_PROMPT, D_MODEL), x_sample.reshape(N_SAMPLE, D_MODEL))
    gains = lambda a: a.reshape(a.shape[0], 1, D_MODEL)
    norm_ffn1, norm_mix, norm_ffn2 = gains(norm_ffn1), gains(norm_mix), gains(norm_ffn2)
    ffn1 = (ffn1_w_gate, ffn1_w_up, ffn1_w_down)
    ffn2 = (ffn2_w_gate, ffn2_w_up, ffn2_w_down)
    hgrn_w_out_b = hgrn_w_out.astype(BF16)
    gla_w_out_b = gla_w_out.astype(BF16)
    lb = _lower_bounds(hgrn_lb_logits)

    new_hgrn_p, new_gla_p, new_pool_p, new_pool_s = [], [], [], []
    new_hgrn_s = new_gla_s = None
    for li in range(DEPTH):
        j = li // N_MIXERS
        kind = li % N_MIXERS
        x = _ffn(x_parts if li == 0 else (x,), norm_ffn1, *ffn1, li)
        if kind == 0:
            proj = _norm_proj(x, norm_mix, li, hgrn_w_in, j, 4 * D_MODEL, "hgrn_in_proj")
            lb_l = lb[li:li + 1]
            groups = HG_HEADS // HGRN_HB
            o_p, st_p = _rec_prompt("hgrn", proj, lb_l, HG_HEADS, HGRN_HB, HG_FORGET_DIM, HG_HEAD_V,
                                    HGRN_SEG[j], 0, groups, 2 * groups)
            o_s, new_hgrn_s = _rec_sample(
                "hgrn",
                _sample_heads(proj, 0, HG_HEADS, HG_FORGET_DIM),
                _sample_heads(proj, D_MODEL, HG_HEADS, HG_FORGET_DIM),
                _sample_heads(proj, 2 * D_MODEL, HG_HEADS, HG_HEAD_V),
                lb_l.reshape(HG_HEADS, 1, HG_FORGET_DIM), state_hgrn, j, new_hgrn_s,
                HG_HEADS, HG_FORGET_DIM, HG_HEAD_V)
            new_hgrn_p.append(st_p)
            x = _mix_out(o_p, _unsplit_heads(o_s), proj, 3, hgrn_o_norm[j].reshape(1, D_MODEL), x,
                         hgrn_w_out_b, j, D_MODEL, "hgrn_out")
        elif kind == 1:
            n_main = 2 * GLA_DK + 2 * GLA_DV
            w_low = jnp.pad(gla_w_in[j, :, n_main:], ((0, 0), (0, LANES - GLA_GATE_RANK))).astype(BF16)
            w_up = jnp.pad(gla_w_gate_up[j], ((0, LANES - GLA_GATE_RANK), (0, 0))).astype(BF16)
            proj, lf = _norm_proj_gate(x, norm_mix, li, gla_w_in, j, n_main, w_low, w_up,
                                       gla_b_gate[j].reshape(1, GLA_DK), "gla_in_proj")
            kb = GLA_DK // (GLA_HB * GLA_HEAD_K)
            vb = 2 * GLA_DK // (GLA_HB * GLA_HEAD_V)
            o_p, st_p = _rec_prompt("gla", proj, lf, GLA_HEADS, GLA_HB, GLA_HEAD_K, GLA_HEAD_V,
                                    GLA_SEG, 0, kb, vb)
            o_s, new_gla_s = _rec_sample(
                "gla",
                _sample_heads(proj, 0, GLA_HEADS, GLA_HEAD_K),
                _sample_heads(proj, GLA_DK, GLA_HEADS, GLA_HEAD_K),
                _sample_heads(proj, 2 * GLA_DK, GLA_HEADS, GLA_HEAD_V),
                _sample_heads(lf, 0, GLA_HEADS, GLA_HEAD_K), state_gla, j, new_gla_s,
                GLA_HEADS, GLA_HEAD_K, GLA_HEAD_V)
            new_gla_p.append(st_p)
            x = _mix_out(o_p, _unsplit_heads(o_s), proj, (2 * GLA_DK + GLA_DV) // D_MODEL,
                         gla_o_norm[j].reshape(1, GLA_DV), x, gla_w_out_b, j,
                         GLA_HEAD_V, "gla_out")
        else:
            w_group = pool_w_group[j].astype(BF16)
            scale = pool_scale[j].reshape(1, D_MODEL)
            x, nb_p = _pool(x, norm_mix, li, jnp.zeros((BATCH, POOL_BUF, D_MODEL), F32), w_group, scale,
                            BATCH, SEQ, 1, POOL_TT, 0, 0)
            x, nb_s = _pool(x, norm_mix, li, state_pool[j], w_group, scale,
                            DEC_BATCH, DEC_SEQ, POOL_SAMPLE_NS, DEC_SEQ, N_PROMPT, min(POOL_BUF, PAST_LEN))
            new_pool_p.append(nb_p)
            new_pool_s.append(nb_s)
        x = _ffn((x,), norm_ffn2, *ffn2, li)

    y_p, y_s = _final_norm(x, final_norm.reshape(1, D_MODEL))
    return (y_p.reshape(BATCH, SEQ, D_MODEL), y_s.reshape(DEC_BATCH, DEC_SEQ, D_MODEL),
            jnp.stack(new_hgrn_p), jnp.stack(new_gla_p), jnp.stack(new_pool_p),
            new_hgrn_s, new_gla_s, jnp.stack(new_pool_s))
```
